```python
import math
import jax, jax.numpy as jnp
from jax import lax
import numpy as np

D_MODEL = 1024
BATCH = 2
SEQ = 8192
DEPTH = 4

N_MIXERS = 3
N_ATTN_LAYERS = (DEPTH + 2) // 3
N_RET_LAYERS = (DEPTH + 1) // 3
N_SSM_LAYERS = DEPTH // 3

ATTN_GROUPS = ((128, 1), (512, 4), (2048, 16))
ATTN_HEADS = 8
ATTN_HEAD_DIM = D_MODEL // ATTN_HEADS
ATTN_BLOCK = 128
RET_HEADS = 4
RET_KEY_DIM = D_MODEL // RET_HEADS
RET_VAL_DIM = 2 * RET_KEY_DIM
RET_CHUNK = 128
SSM_GROUP = 16
SSM_GROUPS = D_MODEL // SSM_GROUP
SSM_STATE = 64
N_EXPERTS = 64
TOP_K = 8
N_EXPERT_GROUPS = 8
TOPK_GROUPS = 4
EXPERT_DIM = 256
SHARED_DIM = 256
ROUTED_SCALE = 2.5
MOE_BLOCK = 128
DEEPNORM_ALPHA = (2 * DEPTH) ** 0.25
DEEPNORM_BETA = (8 * DEPTH) ** -0.25
LN_EPS = 1e-5

kernel_name = "hybrid_dilated_retention_s5_moe_deepnorm"

F32 = jnp.float32


def layer_norm(x, g, b):
    xf = x.astype(F32)
    mu = jnp.mean(xf, axis=-1, keepdims=True)
    var = jnp.mean(jnp.square(xf - mu), axis=-1, keepdims=True)
    y = (xf - mu) * lax.rsqrt(var + LN_EPS) * g.astype(F32) + b.astype(F32)
    return y.astype(x.dtype)


def alibi_slopes(n):
    return 2.0 ** (-8.0 * (jnp.arange(n, dtype=F32) + 1.0) / n)


def dilated_window_attention(q, k, v, window, dilation, slopes):
    b, s, h, e = q.shape
    steps = window // dilation
    blk = ATTN_BLOCK
    ls = -(-s // (dilation * blk)) * blk
    sp = ls * dilation
    nb = ls // blk

    def to_blocks(t):
        t = jnp.pad(t, ((0, 0), (0, sp - s), (0, 0), (0, 0)))
        t = t.reshape(b, ls, dilation, h, e).transpose(0, 2, 3, 1, 4)
        return t.reshape(b, dilation, h, nb, blk, e)

    def with_prev(t):
        prev = jnp.pad(t[:, :, :, :-1], ((0, 0), (0, 0), (0, 0), (1, 0), (0, 0), (0, 0)))
        return jnp.concatenate([prev, t], axis=-2)

    qb = to_blocks(q)
    kk = with_prev(to_blocks(k))
    vv = with_prev(to_blocks(v))
    qi = jnp.arange(blk)[:, None]
    kj = jnp.arange(2 * blk)[None, :]
    step = qi - kj + blk
    not_before_start = (jnp.arange(nb) > 0)[:, None, None] | (kj >= blk)[None]
    valid = (step >= 0) & (step <= steps) & not_before_start
    dist = (step * dilation).astype(F32)
    scores = jnp.einsum('bdhnqe,bdhnke->bdhnqk', qb, kk)
    scores = scores - slopes[:, None, None, None] * dist
    scores = jnp.where(valid, scores, -jnp.inf)
    m = jnp.max(scores, axis=-1, keepdims=True)
    p = jnp.exp(scores - m)
    l = jnp.sum(p, axis=-1, keepdims=True)
    o = jnp.einsum('bdhnqk,bdhnke->bdhnqe', p, vv) / l
    lse = (m + jnp.log(l))[..., 0]
    o = o.reshape(b, dilation, h, ls, e).transpose(0, 3, 1, 2, 4).reshape(b, sp, h, e)[:, :s]
    lse = lse.reshape(b, dilation, h, ls).transpose(0, 3, 1, 2).reshape(b, sp, h)[:, :s]
    return o, lse


def dilated_attention_mixer(x, w_in, w_out):
    b, s, _ = x.shape
    n_g = len(ATTN_GROUPS)
    qkv = (x @ w_in).astype(F32).reshape(b, s, n_g, 3, ATTN_HEADS, ATTN_HEAD_DIM)
    slopes = alibi_slopes(ATTN_HEADS)
    outs, lses = [], []
    for g, (window, dil) in enumerate(ATTN_GROUPS):
        q = qkv[:, :, g, 0] * ATTN_HEAD_DIM ** -0.5
        o, lse = dilated_window_attention(q, qkv[:, :, g, 1], qkv[:, :, g, 2], window, dil, slopes)
        outs.append(o)
        lses.append(lse)
    w = jax.nn.softmax(jnp.stack(lses, axis=0), axis=0)
    o = jnp.sum(w[..., None] * jnp.stack(outs, axis=0), axis=0)
    return o.reshape(b, s, ATTN_HEADS * ATTN_HEAD_DIM).astype(x.dtype) @ w_out


def retention_mixer(x, w_in, gn_g, gn_b, w_out):
    b, s, _ = x.shape
    h, dk, dv, c = RET_HEADS, RET_KEY_DIM, RET_VAL_DIM, RET_CHUNK
    nc = s // c
    proj = x @ w_in
    q, k, v, g = jnp.split(proj, [h * dk, 2 * h * dk, 2 * h * dk + h * dv], axis=-1)
    q = q.astype(F32).reshape(b, nc, c, h, dk)
    k = (k.astype(F32) * dk ** -0.5).reshape(b, nc, c, h, dk)
    v = v.astype(F32).reshape(b, nc, c, h, dv)
    log_gamma = jnp.log(1.0 - 2.0 ** (-5.0 - jnp.arange(h, dtype=F32)))
    pos = jnp.arange(c, dtype=F32)
    rel = pos[:, None] - pos[None, :]
    decay = jnp.where(rel >= 0, jnp.exp(log_gamma[:, None, None] * jnp.maximum(rel, 0.0)), 0.0)
    scores = jnp.einsum('bnihd,bnjhd->bnhij', q, k) * decay
    inner = jnp.einsum('bnhij,bnjhe->bnihe', scores, v)
    xi = jnp.exp(log_gamma[None, :] * (pos[:, None] + 1.0))
    zeta = jnp.exp(log_gamma[None, :] * (c - 1.0 - pos[:, None]))
    chunk_decay = jnp.exp(log_gamma * c)

    def chunk_step(state, inp):
        qc, kc, vc = inp
        cross = jnp.einsum('bihd,bhde->bihe', qc * xi[:, :, None], state)
        state = chunk_decay[None, :, None, None] * state + jnp.einsum('bjhd,bjhe->bhde', kc * zeta[:, :, None], vc)
        return state, cross

    state0 = jnp.zeros((b, h, dk, dv), F32)
    _, cross = lax.scan(chunk_step, state0, (q.transpose(1, 0, 2, 3, 4), k.transpose(1, 0, 2, 3, 4), v.transpose(1, 0, 2, 3, 4)))
    y = (inner + cross.transpose(1, 0, 2, 3, 4)).reshape(b, s, h, dv)
    mu = jnp.mean(y, axis=-1, keepdims=True)
    var = jnp.mean(jnp.square(y - mu), axis=-1, keepdims=True)
    y = ((y - mu) * lax.rsqrt(var + LN_EPS)).reshape(b, s, h * dv) * gn_g.astype(F32) + gn_b.astype(F32)
    y = jax.nn.silu(g.astype(F32)) * y
    return y.astype(x.dtype) @ w_out


def s5_mixer(x, a_re, a_im, b_re, b_im, c_re, c_im, d_skip, log_dt, w_glu):
    bsz, s, _ = x.shape
    u = x.astype(F32).reshape(bsz, s, SSM_GROUPS, SSM_GROUP)
    a_re, a_im = a_re.astype(F32), a_im.astype(F32)
    b_re, b_im = b_re.astype(F32), b_im.astype(F32)
    dt = jnp.exp(log_dt.astype(F32))[:, None]
    mag = jnp.exp(dt * a_re)
    ab_re = mag * jnp.cos(dt * a_im)
    ab_im = mag * jnp.sin(dt * a_im)
    den = a_re * a_re + a_im * a_im
    f_re = ((ab_re - 1.0) * a_re + ab_im * a_im) / den
    f_im = (ab_im * a_re - (ab_re - 1.0) * a_im) / den
    bb_re = f_re[..., None] * b_re - f_im[..., None] * b_im
    bb_im = f_re[..., None] * b_im + f_im[..., None] * b_re
    bu_re = jnp.einsum('bsgc,gpc->bsgp', u, bb_re)
    bu_im = jnp.einsum('bsgc,gpc->bsgp', u, bb_im)
    aa_re = jnp.broadcast_to(ab_re, bu_re.shape)
    aa_im = jnp.broadcast_to(ab_im, bu_im.shape)

    def combine(e1, e2):
        ar1, ai1, br1, bi1 = e1
        ar2, ai2, br2, bi2 = e2
        return (ar2 * ar1 - ai2 * ai1, ar2 * ai1 + ai2 * ar1,
                ar2 * br1 - ai2 * bi1 + br2, ar2 * bi1 + ai2 * br1 + bi2)

    _, _, h_re, h_im = lax.associative_scan(combine, (aa_re, aa_im, bu_re, bu_im), axis=1)
    y = (jnp.einsum('bsgp,gcp->bsgc', h_re, c_re.astype(F32))
         - jnp.einsum('bsgp,gcp->bsgc', h_im, c_im.astype(F32))
         + d_skip.astype(F32) * u)
    y = jax.nn.gelu(y).reshape(bsz, s, D_MODEL).astype(x.dtype)
    val, gate = jnp.split(y @ w_glu, 2, axis=-1)
    return val * jax.nn.sigmoid(gate)


def moe_ffn(x, w_router, router_bias, w_gate, w_up, w_down, sw_gate, sw_up, sw_down):
    bsz, s, d = x.shape
    t = bsz * s
    xf = x.reshape(t, d)
    scores = jax.nn.sigmoid((xf @ w_router).astype(F32))
    sel = scores + router_bias.astype(F32)
    grp = sel.reshape(t, N_EXPERT_GROUPS, N_EXPERTS // N_EXPERT_GROUPS)
    grp_score = jnp.sum(lax.top_k(grp, 2)[0], axis=-1)
    _, grp_idx = lax.top_k(grp_score, TOPK_GROUPS)
    grp_mask = jnp.sum(jax.nn.one_hot(grp_idx, N_EXPERT_GROUPS, dtype=F32), axis=1) > 0
    exp_mask = jnp.repeat(grp_mask, N_EXPERTS // N_EXPERT_GROUPS, axis=1)
    _, top_idx = lax.top_k(jnp.where(exp_mask, sel, -jnp.inf), TOP_K)
    top_w = jnp.take_along_axis(scores, top_idx, axis=1)
    gates = top_w / jnp.sum(top_w, axis=-1, keepdims=True) * ROUTED_SCALE
    tk = t * TOP_K
    flat_e = top_idx.reshape(-1).astype(jnp.int32)
    flat_tok = jnp.arange(tk, dtype=jnp.int32) // TOP_K
    flat_g = gates.reshape(-1)
    order = jnp.argsort(flat_e, stable=True)
    sorted_e = flat_e[order]
    counts = jnp.bincount(flat_e, length=N_EXPERTS)
    starts = jnp.cumsum(counts) - counts
    padded = (counts + MOE_BLOCK - 1) // MOE_BLOCK * MOE_BLOCK
    pad_ends = jnp.cumsum(padded)
    pad_starts = pad_ends - padded
    dest = pad_starts[sorted_e] + jnp.arange(tk, dtype=jnp.int32) - starts[sorted_e]
    cap = tk + N_EXPERTS * MOE_BLOCK
    n_blocks = cap // MOE_BLOCK
    buf_tok = jnp.full((cap,), t, jnp.int32).at[dest].set(flat_tok[order])
    buf_gate = jnp.zeros((cap,), F32).at[dest].set(flat_g[order])
    block_e = jnp.minimum(jnp.searchsorted(pad_ends, jnp.arange(n_blocks) * MOE_BLOCK, side='right'), N_EXPERTS - 1)
    xpad = jnp.concatenate([xf, jnp.zeros((1, d), xf.dtype)], axis=0)

    def block_step(acc, inp):
        e, tok, gate = inp
        xb = xpad[tok]
        hb = jax.nn.silu(xb @ w_gate[e]) * (xb @ w_up[e])
        yb = (hb @ w_down[e]).astype(F32) * gate[:, None]
        return acc.at[tok].add(yb), None

    acc, _ = lax.scan(block_step, jnp.zeros((t + 1, d), F32),
                      (block_e, buf_tok.reshape(n_blocks, MOE_BLOCK), buf_gate.reshape(n_blocks, MOE_BLOCK)))
    shared = (jax.nn.silu(xf @ sw_gate) * (xf @ sw_up)) @ sw_down
    return (acc[:t] + shared.astype(F32)).astype(x.dtype).reshape(bsz, s, d)


def setup_inputs(seed: int = 0) -> dict:
    key = jax.random.key(seed)
    ks = jax.random.split(key, 32)
    d = D_MODEL
    nrm = lambda k, shape, scale: jax.random.normal(k, shape, F32) * scale
    beta = DEEPNORM_BETA
    attn_in = 3 * len(ATTN_GROUPS) * ATTN_HEADS * ATTN_HEAD_DIM
    ret_in = 2 * RET_HEADS * RET_KEY_DIM + 2 * RET_HEADS * RET_VAL_DIM
    ret_v = RET_HEADS * RET_VAL_DIM
    ns, g, p, c = N_SSM_LAYERS, SSM_GROUPS, SSM_STATE, SSM_GROUP
    a_im0 = math.pi * jnp.arange(p, dtype=F32)
    return {
        'x': nrm(ks[0], (BATCH, SEQ, d), 1.0),
        'attn_w_in': nrm(ks[1], (N_ATTN_LAYERS, d, attn_in), d ** -0.5),
        'attn_w_out': nrm(ks[2], (N_ATTN_LAYERS, ATTN_HEADS * ATTN_HEAD_DIM, d), beta * (ATTN_HEADS * ATTN_HEAD_DIM) ** -0.5),
        'ret_w_in': nrm(ks[3], (N_RET_LAYERS, d, ret_in), d ** -0.5),
        'ret_gn_g': 1.0 + nrm(ks[4], (N_RET_LAYERS, ret_v), 0.01),
        'ret_gn_b': nrm(ks[5], (N_RET_LAYERS, ret_v), 0.01),
        'ret_w_out': nrm(ks[6], (N_RET_LAYERS, ret_v, d), beta * ret_v ** -0.5),
        'ssm_a_re': -0.5 + nrm(ks[7], (ns, g, p), 0.01),
        'ssm_a_im': a_im0 + nrm(ks[8], (ns, g, p), 0.01),
        'ssm_b_re': nrm(ks[9], (ns, g, p, c), (2.0 * c) ** -0.5),
        'ssm_b_im': nrm(ks[10], (ns, g, p, c), (2.0 * c) ** -0.5),
        'ssm_c_re': nrm(ks[11], (ns, g, c, p), (2.0 * p) ** -0.5),
        'ssm_c_im': nrm(ks[12], (ns, g, c, p), (2.0 * p) ** -0.5),
        'ssm_d': nrm(ks[13], (ns, g, c), 1.0),
        'ssm_log_dt': jax.random.uniform(ks[14], (ns, g), F32, math.log(1e-3), math.log(1e-1)),
        'ssm_w_glu': nrm(ks[15], (ns, d, 2 * d), beta * d ** -0.5),
        'moe_w_router': nrm(ks[16], (DEPTH, d, N_EXPERTS), d ** -0.5),
        'moe_router_bias': nrm(ks[17], (DEPTH, N_EXPERTS), 0.01),
        'moe_w_gate': nrm(ks[18], (DEPTH, N_EXPERTS, d, EXPERT_DIM), d ** -0.5),
        'moe_w_up': nrm(ks[19], (DEPTH, N_EXPERTS, d, EXPERT_DIM), d ** -0.5),
        'moe_w_down': nrm(ks[20], (DEPTH, N_EXPERTS, EXPERT_DIM, d), beta * EXPERT_DIM ** -0.5),
        'shared_w_gate': nrm(ks[21], (DEPTH, d, SHARED_DIM), d ** -0.5),
        'shared_w_up': nrm(ks[22], (DEPTH, d, SHARED_DIM), d ** -0.5),
        'shared_w_down': nrm(ks[23], (DEPTH, SHARED_DIM, d), beta * SHARED_DIM ** -0.5),
        'ln_g': 1.0 + nrm(ks[24], (DEPTH, 2, d), 0.01),
        'ln_b': nrm(ks[25], (DEPTH, 2, d), 0.01),
    }


def reference(x, attn_w_in, attn_w_out, ret_w_in, ret_gn_g, ret_gn_b, ret_w_out,
              ssm_a_re, ssm_a_im, ssm_b_re, ssm_b_im, ssm_c_re, ssm_c_im, ssm_d, ssm_log_dt, ssm_w_glu,
              moe_w_router, moe_router_bias, moe_w_gate, moe_w_up, moe_w_down,
              shared_w_gate, shared_w_up, shared_w_down, ln_g, ln_b):
    for i in range(DEPTH):
        j = i // N_MIXERS
        kind = i % N_MIXERS
        if kind == 0:
            h = dilated_attention_mixer(x, attn_w_in[j], attn_w_out[j])
        elif kind == 1:
            h = retention_mixer(x, ret_w_in[j], ret_gn_g[j], ret_gn_b[j], ret_w_out[j])
        else:
            h = s5_mixer(x, ssm_a_re[j], ssm_a_im[j], ssm_b_re[j], ssm_b_im[j], ssm_c_re[j], ssm_c_im[j],
                         ssm_d[j], ssm_log_dt[j], ssm_w_glu[j])
        x = layer_norm(DEEPNORM_ALPHA * x + h, ln_g[i, 0], ln_b[i, 0])
        h = moe_ffn(x, moe_w_router[i], moe_router_bias[i], moe_w_gate[i], moe_w_up[i], moe_w_down[i],
                    shared_w_gate[i], shared_w_up[i], shared_w_down[i])
        x = layer_norm(DEEPNORM_ALPHA * x + h, ln_g[i, 1], ln_b[i, 1])
    return x
```

```python
import functools
import math

import jax
import jax.numpy as jnp
from jax import lax
from jax.experimental import pallas as pl
from jax.experimental.pallas import tpu as pltpu

F32 = jnp.float32
BF16 = jnp.bfloat16

D_MODEL = 1024
DEPTH = 4
N_MIXERS = 3
ATTN_GROUPS = ((128, 1), (512, 4), (2048, 16))
ATTN_HEADS = 8
ATTN_HEAD_DIM = D_MODEL // ATTN_HEADS
ATTN_BLOCK = 128
RET_HEADS = 4
RET_KEY_DIM = D_MODEL // RET_HEADS
RET_VAL_DIM = 2 * RET_KEY_DIM
RET_CHUNK = 128
SSM_GROUP = 16
SSM_GROUPS = D_MODEL // SSM_GROUP
SSM_STATE = 64
N_EXPERTS = 64
TOP_K = 8
N_EXPERT_GROUPS = 8
TOPK_GROUPS = 4
EXPERT_DIM = 256
SHARED_DIM = 256
ROUTED_SCALE = 2.5
DEEPNORM_ALPHA = (2 * DEPTH) ** 0.25
LN_EPS = 1e-5

LANES = 128
SUBLANES = 8
MASK_NEG = -1e30
VMEM_LIMIT = 56 * 1024 * 1024

ROW_TILE = 512
PROJ_TM = 1024
PROJ_TN = 1024
MOE_TILE = 1024
MOE_SLOTS = 128
SSM_TIME = 128
SSM_COLS = 128


def _cparams():
    return pltpu.CompilerParams(vmem_limit_bytes=VMEM_LIMIT)


def _dot(a, b):
    return jnp.dot(a, b, preferred_element_type=F32)


def _dot_nt(a, b):
    return lax.dot_general(a, b, (((1,), (1,)), ((), ())), preferred_element_type=F32)


def _dot_tn(a, b):
    return lax.dot_general(a, b, (((0,), (0,)), ((), ())), preferred_element_type=F32)


def _sigmoid(x):
    return 1.0 / (1.0 + jnp.exp(-x))


def _layer_norm(y, g, b):
    mu = jnp.mean(y, axis=-1, keepdims=True)
    yc = y - mu
    var = jnp.mean(yc * yc, axis=-1, keepdims=True)
    return yc * lax.rsqrt(var + LN_EPS) * g + b


def _store_ln(y, g_ref, b_ref, o_ref, ob_ref):
    out = _layer_norm(y, g_ref[...], b_ref[...])
    o_ref[...] = out
    ob_ref[...] = out.astype(BF16)


def _proj_kernel(x_ref, w_ref, o_ref):
    o_ref[...] = _dot(x_ref[...], w_ref[...]).astype(o_ref.dtype)


def _proj(xb, w, out_dtype, name):
    m, k = xb.shape
    n = w.shape[1]
    tm = min(PROJ_TM, m)
    tn = min(PROJ_TN, n)
    return pl.pallas_call(
        _proj_kernel,
        grid=(m // tm, n // tn),
        in_specs=[pl.BlockSpec((tm, k), lambda i, j: (i, 0)),
                  pl.BlockSpec((k, tn), lambda i, j: (0, j))],
        out_specs=pl.BlockSpec((tm, tn), lambda i, j: (i, j)),
        out_shape=jax.ShapeDtypeStruct((m, n), out_dtype),
        compiler_params=_cparams(),
        name=name,
    )(xb, w)


def _attn_kernel(q_ref, kc_ref, kp_ref, vc_ref, vp_ref, bp_ref, bc_ref, o_ref, lse_ref):
    n = pl.program_id(2)
    first = jnp.where(n == 0, MASK_NEG, 0.0).astype(F32)
    scale = ATTN_HEAD_DIM ** -0.5
    lane = lax.broadcasted_iota(jnp.int32, (ATTN_BLOCK, LANES), 1)
    lse_tile = jnp.zeros((ATTN_BLOCK, LANES), F32)
    for h in range(ATTN_HEADS):
        sl = slice(h * ATTN_HEAD_DIM, (h + 1) * ATTN_HEAD_DIM)
        q = q_ref[0, :, sl]
        sp = _dot_nt(q, kp_ref[0, :, sl]) * scale + (bp_ref[h] + first)
        sc = _dot_nt(q, kc_ref[0, :, sl]) * scale + bc_ref[h]
        m = jnp.maximum(jnp.max(sp, axis=1, keepdims=True), jnp.max(sc, axis=1, keepdims=True))
        pp = jnp.exp(sp - m)
        pc = jnp.exp(sc - m)
        l = jnp.sum(pp, axis=1, keepdims=True) + jnp.sum(pc, axis=1, keepdims=True)
        o = _dot(pp.astype(BF16), vp_ref[0, :, sl]) + _dot(pc.astype(BF16), vc_ref[0, :, sl])
        o_ref[0, :, sl] = o / l
        lse_tile = jnp.where(lane == h, m + jnp.log(l), lse_tile)
    lse_ref[0] = lse_tile


def _attn_bias(window, dil):
    steps = window // dil
    assert steps <= ATTN_BLOCK
    qi = jnp.arange(ATTN_BLOCK)[:, None]
    kj = jnp.arange(ATTN_BLOCK)[None, :]
    slopes = 2.0 ** (-8.0 * (jnp.arange(ATTN_HEADS, dtype=F32) + 1.0) / ATTN_HEADS)

    def bias(step):
        valid = (step >= 0) & (step <= steps)
        dist = (step * dil).astype(F32)
        return jnp.where(valid[None], -(slopes[:, None, None] * dist[None]), MASK_NEG).astype(F32)

    return bias(qi - kj + ATTN_BLOCK), bias(qi - kj)


def _attn_group(qkv, gi, window, dil):
    b, s, width = qkv.shape
    hd = ATTN_HEADS * ATTN_HEAD_DIM
    assert s % (dil * ATTN_BLOCK) == 0 and width % hd == 0
    wb = width // hd
    ls = s // dil
    nb = ls // ATTN_BLOCK
    view = qkv.reshape(b, ls, dil * width)
    bias_prev, bias_cur = _attn_bias(window, dil)
    base = gi * 3

    def cur(c):
        return pl.BlockSpec((1, ATTN_BLOCK, hd), lambda bi, r, n: (bi, n, r * wb + base + c))

    def prev(c):
        return pl.BlockSpec((1, ATTN_BLOCK, hd),
                            lambda bi, r, n: (bi, jnp.maximum(n - 1, 0), r * wb + base + c))

    const = pl.BlockSpec((ATTN_HEADS, ATTN_BLOCK, ATTN_BLOCK), lambda bi, r, n: (0, 0, 0))
    o, lse = pl.pallas_call(
        _attn_kernel,
        grid=(b, dil, nb),
        in_specs=[cur(0), cur(1), prev(1), cur(2), prev(2), const, const],
        out_specs=[pl.BlockSpec((1, ATTN_BLOCK, hd), lambda bi, r, n: (bi, n, r)),
                   pl.BlockSpec((1, ATTN_BLOCK, LANES), lambda bi, r, n: (bi, n, r))],
        out_shape=[jax.ShapeDtypeStruct((b, ls, dil * hd), F32),
                   jax.ShapeDtypeStruct((b, ls, dil * LANES), F32)],
        compiler_params=_cparams(),
        name=f"attn_g{gi}",
    )(view, view, view, view, view, bias_prev, bias_cur)
    return o.reshape(b * s, hd), lse.reshape(b * s, LANES)


def _attn_out_kernel(o1_ref, o2_ref, o3_ref, l1_ref, l2_ref, l3_ref, x_ref, w_ref, g_ref, b_ref,
                     o_ref, ob_ref):
    l1, l2, l3 = l1_ref[...], l2_ref[...], l3_ref[...]
    mx = jnp.maximum(jnp.maximum(l1, l2), l3)
    e1, e2, e3 = jnp.exp(l1 - mx), jnp.exp(l2 - mx), jnp.exp(l3 - mx)
    den = e1 + e2 + e3
    w1, w2, w3 = e1 / den, e2 / den, e3 / den
    cols = []
    for h in range(ATTN_HEADS):
        sl = slice(h * ATTN_HEAD_DIM, (h + 1) * ATTN_HEAD_DIM)
        oh = (w1[:, h:h + 1] * o1_ref[:, sl] + w2[:, h:h + 1] * o2_ref[:, sl]
              + w3[:, h:h + 1] * o3_ref[:, sl])
        cols.append(oh.astype(BF16))
    o = jnp.concatenate(cols, axis=1)
    y = _dot(o, w_ref[...]) + DEEPNORM_ALPHA * x_ref[...]
    _store_ln(y, g_ref, b_ref, o_ref, ob_ref)


def _row_spec(tm, width):
    return pl.BlockSpec((tm, width), lambda i: (i, 0))


def _const_spec(shape):
    return pl.BlockSpec(shape, lambda i: tuple(0 for _ in shape))


def _x_outs(t, tm):
    return dict(
        out_specs=[_row_spec(tm, D_MODEL), _row_spec(tm, D_MODEL)],
        out_shape=[jax.ShapeDtypeStruct((t, D_MODEL), F32), jax.ShapeDtypeStruct((t, D_MODEL), BF16)],
    )


def _attention_layer(x, xb, w_in, w_out, ln_g, ln_b, bsz, seq):
    t = x.shape[0]
    qkv = _proj(xb, w_in.astype(BF16), BF16, "attn_proj").reshape(bsz, seq, -1)
    outs = [_attn_group(qkv, gi, window, dil) for gi, (window, dil) in enumerate(ATTN_GROUPS)]
    tm = min(ROW_TILE, t)
    hd = ATTN_HEADS * ATTN_HEAD_DIM
    return pl.pallas_call(
        _attn_out_kernel,
        grid=(t // tm,),
        in_specs=[_row_spec(tm, hd)] * 3 + [_row_spec(tm, LANES)] * 3
                 + [_row_spec(tm, D_MODEL), _const_spec((hd, D_MODEL)),
                    _const_spec((1, D_MODEL)), _const_spec((1, D_MODEL))],
        compiler_params=_cparams(),
        name="attn_out",
        **_x_outs(t, tm),
    )(outs[0][0], outs[1][0], outs[2][0], outs[0][1], outs[1][1], outs[2][1],
      x, w_out.astype(BF16), ln_g[None], ln_b[None])


def _ret_kernel(q_ref, k_ref, v_ref, g_ref, dec_ref, xi_ref, zeta_ref, cd_ref, gg_ref, gb_ref,
                y_ref, state_ref):
    n = pl.program_id(2)

    @pl.when(n == 0)
    def _():
        state_ref[...] = jnp.zeros_like(state_ref)

    kscale = RET_KEY_DIM ** -0.5
    q = q_ref[0]
    k = k_ref[0]
    v = v_ref[0]
    scores = _dot_nt(q, k) * kscale * dec_ref[0]
    inner = _dot(scores.astype(BF16), v)
    st = state_ref[...]
    cross = _dot((q.astype(F32) * xi_ref[0]).astype(BF16), st.astype(BF16))
    kz = (k.astype(F32) * kscale * zeta_ref[0]).astype(BF16)
    state_ref[...] = cd_ref[0, 0:1, 0:1] * st + _dot_tn(kz, v)
    y = inner + cross
    mu = jnp.mean(y, axis=1, keepdims=True)
    yc = y - mu
    var = jnp.mean(yc * yc, axis=1, keepdims=True)
    yn = yc * lax.rsqrt(var + LN_EPS) * gg_ref[...] + gb_ref[...]
    g = g_ref[0]
    y_ref[0] = ((g * _sigmoid(g)) * yn).astype(y_ref.dtype)


def _ret_out_kernel(y_ref, x_ref, w_ref, g_ref, b_ref, o_ref, ob_ref):
    y = _dot(y_ref[...], w_ref[...]) + DEEPNORM_ALPHA * x_ref[...]
    _store_ln(y, g_ref, b_ref, o_ref, ob_ref)


def _retention_layer(x, xb, w_in, gn_g, gn_b, w_out, ln_g, ln_b, bsz, seq):
    t = x.shape[0]
    h, dk, dv, c = RET_HEADS, RET_KEY_DIM, RET_VAL_DIM, RET_CHUNK
    nqkv = 2 * h * dk + h * dv
    wb = w_in.astype(BF16)
    qkv = _proj(xb, wb[:, :nqkv], BF16, "ret_proj_qkv").reshape(bsz, seq, nqkv)
    gate = _proj(xb, wb[:, nqkv:], F32, "ret_proj_gate").reshape(bsz, seq, h * dv)
    log_gamma = jnp.log(1.0 - 2.0 ** (-5.0 - jnp.arange(h, dtype=F32)))
    pos = jnp.arange(c, dtype=F32)
    rel = pos[:, None] - pos[None, :]
    decay = jnp.where(rel >= 0, jnp.exp(log_gamma[:, None, None] * jnp.maximum(rel, 0.0)), 0.0)
    xi = jnp.exp(log_gamma[:, None] * (pos[None, :] + 1.0))
    zeta = jnp.exp(log_gamma[:, None] * (c - 1.0 - pos[None, :]))
    xi = jnp.broadcast_to(xi[:, :, None], (h, c, dk))
    zeta = jnp.broadcast_to(zeta[:, :, None], (h, c, dk))
    cd = jnp.broadcast_to(jnp.exp(log_gamma * c)[:, None, None], (h, SUBLANES, LANES))
    kq = (h * dk) // dk
    kv = (2 * h * dk) // dv
    head = lambda width: pl.BlockSpec((1, c, width), lambda bi, hi, n: (hi, 0, 0))
    y = pl.pallas_call(
        _ret_kernel,
        grid=(bsz, h, seq // c),
        in_specs=[pl.BlockSpec((1, c, dk), lambda bi, hi, n: (bi, n, hi)),
                  pl.BlockSpec((1, c, dk), lambda bi, hi, n: (bi, n, kq + hi)),
                  pl.BlockSpec((1, c, dv), lambda bi, hi, n: (bi, n, kv + hi)),
                  pl.BlockSpec((1, c, dv), lambda bi, hi, n: (bi, n, hi)),
                  head(c), head(dk), head(dk),
                  pl.BlockSpec((1, SUBLANES, LANES), lambda bi, hi, n: (hi, 0, 0)),
                  pl.BlockSpec((1, dv), lambda bi, hi, n: (0, hi)),
                  pl.BlockSpec((1, dv), lambda bi, hi, n: (0, hi))],
        out_specs=pl.BlockSpec((1, c, dv), lambda bi, hi, n: (bi, n, hi)),
        out_shape=jax.ShapeDtypeStruct((bsz, seq, h * dv), BF16),
        scratch_shapes=[pltpu.VMEM((dk, dv), F32)],
        compiler_params=_cparams(),
        name="ret_core",
    )(qkv, qkv, qkv, gate, decay, xi, zeta, cd, gn_g[None], gn_b[None])
    tm = min(ROW_TILE, t)
    return pl.pallas_call(
        _ret_out_kernel,
        grid=(t // tm,),
        in_specs=[_row_spec(tm, h * dv), _row_spec(tm, D_MODEL), _const_spec((h * dv, D_MODEL)),
                  _const_spec((1, D_MODEL)), _const_spec((1, D_MODEL))],
        compiler_params=_cparams(),
        name="ret_out",
        **_x_outs(t, tm),
    )(y.reshape(t, h * dv), x, w_out.astype(BF16), ln_g[None], ln_b[None])


def _gelu_tanh(y):
    return 0.5 * y * (1.0 + jnp.tanh(math.sqrt(2.0 / math.pi) * (y + 0.044715 * (y * y * y))))


def _s5_kernel(x_ref, wb_ref, wc_ref, pr_ref, pi_ref, d_ref, y_ref, cr_ref, ci_ref):
    n = pl.program_id(2)

    @pl.when(n == 0)
    def _():
        cr_ref[...] = jnp.zeros_like(cr_ref)
        ci_ref[...] = jnp.zeros_like(ci_ref)

    nch = pr_ref.shape[2]
    slabs = SSM_TIME // SUBLANES
    u = x_ref[0]
    bu = _dot(u.astype(BF16), wb_ref[0])
    re = bu[:, :nch].reshape(slabs, SUBLANES, nch)
    im = bu[:, nch:].reshape(slabs, SUBLANES, nch)
    pr = pr_ref[0]
    pi = pi_ref[0]
    row = lax.broadcasted_iota(jnp.int32, (1, SUBLANES, nch), 1)
    for k in (1, 2, 4):
        ar = pr[k - 1:k, :][None]
        ai = pi[k - 1:k, :][None]
        sre = pltpu.roll(re, k, 1)
        sim = pltpu.roll(im, k, 1)
        keep = row >= k
        re = re + jnp.where(keep, ar * sre - ai * sim, 0.0)
        im = im + jnp.where(keep, ar * sim + ai * sre, 0.0)
    cr = cr_ref[...]
    ci = ci_ref[...]
    hre, him = [], []
    for j in range(slabs):
        hr = re[j] + (pr * cr - pi * ci)
        hi = im[j] + (pr * ci + pi * cr)
        hre.append(hr)
        him.append(hi)
        cr = jnp.broadcast_to(hr[SUBLANES - 1:SUBLANES, :], (SUBLANES, nch))
        ci = jnp.broadcast_to(hi[SUBLANES - 1:SUBLANES, :], (SUBLANES, nch))
    cr_ref[...] = cr
    ci_ref[...] = ci
    hcat = jnp.concatenate([jnp.concatenate(hre, axis=0), jnp.concatenate(him, axis=0)], axis=1)
    y = _dot(hcat.astype(BF16), wc_ref[0]) + d_ref[...] * u
    y_ref[0] = _gelu_tanh(y).astype(y_ref.dtype)


def _s5_out_kernel(y_ref, x_ref, w_ref, g_ref, b_ref, o_ref, ob_ref):
    vg = _dot(y_ref[...], w_ref[...])
    val = vg[:, :D_MODEL]
    gate = vg[:, D_MODEL:]
    y = val * _sigmoid(gate) + DEEPNORM_ALPHA * x_ref[...]
    _store_ln(y, g_ref, b_ref, o_ref, ob_ref)


def _block_diag(m):
    nb, g, r, c = m.shape
    eye = jnp.eye(g, dtype=m.dtype)
    return (m[:, :, :, None, :] * eye[None, :, None, :, None]).reshape(nb, g * r, g * c)


def _s5_layer(x, a_re, a_im, b_re, b_im, c_re, c_im, d_skip, log_dt, w_glu, ln_g, ln_b, bsz, seq):
    t = x.shape[0]
    gpb = SSM_COLS // SSM_GROUP
    ncb = D_MODEL // SSM_COLS
    nch = gpb * SSM_STATE
    dt = jnp.exp(log_dt)[:, None]
    mag = jnp.exp(dt * a_re)
    ab_re = mag * jnp.cos(dt * a_im)
    ab_im = mag * jnp.sin(dt * a_im)
    den = a_re * a_re + a_im * a_im
    f_re = ((ab_re - 1.0) * a_re + ab_im * a_im) / den
    f_im = (ab_im * a_re - (ab_re - 1.0) * a_im) / den
    bb_re = f_re[..., None] * b_re - f_im[..., None] * b_im
    bb_im = f_re[..., None] * b_im + f_im[..., None] * b_re
    tb = lambda m: jnp.swapaxes(m, 1, 2).reshape(ncb, gpb, SSM_GROUP, SSM_STATE)
    w_b = jnp.concatenate([_block_diag(tb(bb_re)), _block_diag(tb(bb_im))], axis=2).astype(BF16)
    tc = lambda m: jnp.swapaxes(m, 1, 2).reshape(ncb, gpb, SSM_STATE, SSM_GROUP)
    w_c = jnp.concatenate([_block_diag(tc(c_re)), -_block_diag(tc(c_im))], axis=1).astype(BF16)
    pr, pi = [ab_re], [ab_im]
    for _ in range(SUBLANES - 1):
        pr, pi = pr + [pr[-1] * ab_re - pi[-1] * ab_im], pi + [pr[-1] * ab_im + pi[-1] * ab_re]
    pw_re = jnp.stack(pr, axis=0).reshape(SUBLANES, ncb, nch).transpose(1, 0, 2)
    pw_im = jnp.stack(pi, axis=0).reshape(SUBLANES, ncb, nch).transpose(1, 0, 2)
    x3 = x.reshape(bsz, seq, D_MODEL)
    blk = lambda shape: pl.BlockSpec((1,) + shape, lambda bi, cb, n: (cb, 0, 0))
    y = pl.pallas_call(
        _s5_kernel,
        grid=(bsz, ncb, seq // SSM_TIME),
        in_specs=[pl.BlockSpec((1, SSM_TIME, SSM_COLS), lambda bi, cb, n: (bi, n, cb)),
                  blk((SSM_COLS, 2 * nch)), blk((2 * nch, SSM_COLS)),
                  blk((SUBLANES, nch)), blk((SUBLANES, nch)),
                  pl.BlockSpec((1, SSM_COLS), lambda bi, cb, n: (0, cb))],
        out_specs=pl.BlockSpec((1, SSM_TIME, SSM_COLS), lambda bi, cb, n: (bi, n, cb)),
        out_shape=jax.ShapeDtypeStruct((bsz, seq, D_MODEL), BF16),
        scratch_shapes=[pltpu.VMEM((SUBLANES, nch), F32), pltpu.VMEM((SUBLANES, nch), F32)],
        compiler_params=_cparams(),
        name="s5_core",
    )(x3, w_b, w_c, pw_re, pw_im, d_skip.reshape(1, D_MODEL))
    tm = min(ROW_TILE, t)
    return pl.pallas_call(
        _s5_out_kernel,
        grid=(t // tm,),
        in_specs=[_row_spec(tm, D_MODEL), _row_spec(tm, D_MODEL), _const_spec((D_MODEL, 2 * D_MODEL)),
                  _const_spec((1, D_MODEL)), _const_spec((1, D_MODEL))],
        compiler_params=_cparams(),
        name="s5_out",
        **_x_outs(t, tm),
    )(y.reshape(t, D_MODEL), x, w_glu.astype(BF16), ln_g[None], ln_b[None])


def _first_index(eq, iota, size, axis):
    return jnp.min(jnp.where(eq, iota, size), axis=axis, keepdims=True)


def _router_kernel(x_ref, w_ref, b_ref, tri_ref, rank_ref, gate_ref, cnt_ref):
    x = x_ref[...]
    w = w_ref[...]
    x_hi = x.astype(BF16)
    x_lo = (x - x_hi.astype(F32)).astype(BF16)
    w_hi = w.astype(BF16)
    w_lo = (w - w_hi.astype(F32)).astype(BF16)
    logits = _dot_nt(w_hi, x_hi) + (_dot_nt(w_hi, x_lo) + _dot_nt(w_lo, x_hi))
    scores = _sigmoid(logits)
    sel = scores + b_ref[...]
    tt = sel.shape[1]
    gsz = N_EXPERTS // N_EXPERT_GROUPS
    grp = sel.reshape(N_EXPERT_GROUPS, gsz, tt)
    wi = lax.broadcasted_iota(jnp.int32, grp.shape, 1)
    m1 = jnp.max(grp, axis=1, keepdims=True)
    i1 = _first_index(grp == m1, wi, gsz, 1)
    m2 = jnp.max(jnp.where(wi == i1, -jnp.inf, grp), axis=1, keepdims=True)
    gs = (m1 + m2).reshape(N_EXPERT_GROUPS, tt)
    gi = lax.broadcasted_iota(jnp.int32, gs.shape, 0)
    gmask = jnp.zeros(gs.shape, jnp.bool_)
    for _ in range(TOPK_GROUPS):
        gm = jnp.max(gs, axis=0, keepdims=True)
        pick = gi == _first_index(gs == gm, gi, N_EXPERT_GROUPS, 0)
        gmask = gmask | pick
        gs = jnp.where(pick, -jnp.inf, gs)
    emask = jnp.broadcast_to(gmask[:, None, :], grp.shape).reshape(N_EXPERTS, tt)
    cand = jnp.where(emask, sel, -jnp.inf)
    ei = lax.broadcasted_iota(jnp.int32, cand.shape, 0)
    chosen = jnp.zeros(cand.shape, jnp.bool_)
    for _ in range(TOP_K):
        cm = jnp.max(cand, axis=0, keepdims=True)
        pick = ei == _first_index(cand == cm, ei, N_EXPERTS, 0)
        chosen = chosen | pick
        cand = jnp.where(pick, -jnp.inf, cand)
    top_w = jnp.where(chosen, scores, 0.0)
    denom = jnp.sum(top_w, axis=0, keepdims=True)
    gate_ref[...] = top_w / denom * ROUTED_SCALE
    chosen_f = chosen.astype(F32)
    rank = _dot(chosen_f.astype(BF16), tri_ref[...])
    rank_ref[...] = jnp.where(chosen, rank, -1.0).astype(jnp.int32)
    cnt = jnp.sum(chosen_f, axis=1, keepdims=True)
    cnt_ref[0] = jnp.broadcast_to(cnt, (N_EXPERTS, LANES)).astype(jnp.int32)


def _moe_kernel(cnt_ref, x_ref, rank_ref, gate_ref, wgu_ref, wd_ref, o_ref):
    j = pl.program_id(0)
    e = pl.program_id(1)

    @pl.when(e == 0)
    def _():
        o_ref[...] = jnp.zeros_like(o_ref)

    tt = x_ref.shape[0]
    slot = lax.broadcasted_iota(jnp.int32, (MOE_SLOTS, tt), 0)
    n_chunks = (cnt_ref[j * N_EXPERTS + e] + MOE_SLOTS - 1) // MOE_SLOTS

    def chunk(c, carry):
        hit = (rank_ref[0] - c * MOE_SLOTS) == slot
        onehot = hit.astype(F32).astype(BF16)
        xc = _dot(onehot, x_ref[...]).astype(BF16)
        hgu = _dot(xc, wgu_ref[0])
        hg = hgu[:, :EXPERT_DIM]
        hb = ((hg * _sigmoid(hg)) * hgu[:, EXPERT_DIM:]).astype(BF16)
        y = _dot(hb, wd_ref[0])
        gsel = jnp.sum(jnp.where(hit, gate_ref[0], 0.0), axis=1, keepdims=True)
        o_ref[...] += _dot_tn(onehot, (y * gsel).astype(BF16))
        return carry

    lax.fori_loop(0, n_chunks, chunk, 0)


def _moe_out_kernel(x_ref, xb_ref, acc_ref, wgu_ref, wd_ref, g_ref, b_ref, o_ref, ob_ref):
    hgu = _dot(xb_ref[...], wgu_ref[...])
    hg = hgu[:, :SHARED_DIM]
    hb = ((hg * _sigmoid(hg)) * hgu[:, SHARED_DIM:]).astype(BF16)
    shared = _dot(hb, wd_ref[...])
    y = DEEPNORM_ALPHA * x_ref[...] + (acc_ref[...] + shared)
    _store_ln(y, g_ref, b_ref, o_ref, ob_ref)


def _moe_layer(x, xb, w_router, router_bias, w_gate, w_up, w_down, sw_gate, sw_up, sw_down,
               ln_g, ln_b):
    t = x.shape[0]
    tt = min(MOE_TILE, t)
    nj = t // tt
    tri = jnp.triu(jnp.ones((tt, tt), F32), 1).astype(BF16)
    rank, gate, cnt = pl.pallas_call(
        _router_kernel,
        grid=(nj,),
        in_specs=[_row_spec(tt, D_MODEL), _const_spec((N_EXPERTS, D_MODEL)),
                  _const_spec((N_EXPERTS, 1)), _const_spec((tt, tt))],
        out_specs=[pl.BlockSpec((N_EXPERTS, tt), lambda i: (0, i)),
                   pl.BlockSpec((N_EXPERTS, tt), lambda i: (0, i)),
                   pl.BlockSpec((1, N_EXPERTS, LANES), lambda i: (i, 0, 0))],
        out_shape=[jax.ShapeDtypeStruct((N_EXPERTS, t), jnp.int32),
                   jax.ShapeDtypeStruct((N_EXPERTS, t), F32),
                   jax.ShapeDtypeStruct((nj, N_EXPERTS, LANES), jnp.int32)],
        compiler_params=_cparams(),
        name="moe_router",
    )(x, w_router.T, router_bias[:, None], tri)
    wgu = jnp.concatenate([w_gate, w_up], axis=2).astype(BF16)
    acc = pl.pallas_call(
        _moe_kernel,
        grid_spec=pltpu.PrefetchScalarGridSpec(
            num_scalar_prefetch=1,
            grid=(nj, N_EXPERTS),
            in_specs=[pl.BlockSpec((tt, D_MODEL), lambda j, e, c: (j, 0)),
                      pl.BlockSpec((1, 1, tt), lambda j, e, c: (e, 0, j)),
                      pl.BlockSpec((1, 1, tt), lambda j, e, c: (e, 0, j)),
                      pl.BlockSpec((1, D_MODEL, 2 * EXPERT_DIM), lambda j, e, c: (e, 0, 0)),
                      pl.BlockSpec((1, EXPERT_DIM, D_MODEL), lambda j, e, c: (e, 0, 0))],
            out_specs=pl.BlockSpec((tt, D_MODEL), lambda j, e, c: (j, 0)),
        ),
        out_shape=jax.ShapeDtypeStruct((t, D_MODEL), F32),
        compiler_params=_cparams(),
        name="moe_experts",
    )(cnt[:, :, 0].reshape(-1), xb, rank.reshape(N_EXPERTS, 1, t), gate.reshape(N_EXPERTS, 1, t),
      wgu, w_down.astype(BF16))
    tm = min(ROW_TILE, t)
    sgu = jnp.concatenate([sw_gate, sw_up], axis=1).astype(BF16)
    return pl.pallas_call(
        _moe_out_kernel,
        grid=(t // tm,),
        in_specs=[_row_spec(tm, D_MODEL), _row_spec(tm, D_MODEL), _row_spec(tm, D_MODEL),
                  _const_spec((D_MODEL, 2 * SHARED_DIM)), _const_spec((SHARED_DIM, D_MODEL)),
                  _const_spec((1, D_MODEL)), _const_spec((1, D_MODEL))],
        compiler_params=_cparams(),
        name="moe_out",
        **_x_outs(t, tm),
    )(x, xb, acc, sgu, sw_down.astype(BF16), ln_g[None], ln_b[None])


def kernel(x, attn_w_in, attn_w_out, ret_w_in, ret_gn_g, ret_gn_b, ret_w_out, ssm_a_re, ssm_a_im, ssm_b_re, ssm_b_im, ssm_c_re, ssm_c_im, ssm_d, ssm_log_dt, ssm_w_glu, moe_w_router, moe_router_bias, moe_w_gate, moe_w_up, moe_w_down, shared_w_gate, shared_w_up, shared_w_down, ln_g, ln_b):
    bsz, seq, d = x.shape
    depth = ln_g.shape[0]
    xf = x.reshape(bsz * seq, d)
    xb = xf.astype(BF16)
    for i in range(depth):
        j = i // N_MIXERS
        kind = i % N_MIXERS
        if kind == 0:
            xf, xb = _attention_layer(xf, xb, attn_w_in[j], attn_w_out[j], ln_g[i, 0], ln_b[i, 0],
                                      bsz, seq)
        elif kind == 1:
            xf, xb = _retention_layer(xf, xb, ret_w_in[j], ret_gn_g[j], ret_gn_b[j], ret_w_out[j],
                                      ln_g[i, 0], ln_b[i, 0], bsz, seq)
        else:
            xf, xb = _s5_layer(xf, ssm_a_re[j], ssm_a_im[j], ssm_b_re[j], ssm_b_im[j], ssm_c_re[j],
                               ssm_c_im[j], ssm_d[j], ssm_log_dt[j], ssm_w_glu[j],
                               ln_g[i, 0], ln_b[i, 0], bsz, seq)
        xf, xb = _moe_layer(xf, xb, moe_w_router[i], moe_router_bias[i], moe_w_gate[i], moe_w_up[i],
                            moe_w_down[i], shared_w_gate[i], shared_w_up[i], shared_w_down[i],
                            ln_g[i, 1], ln_b[i, 1])
    return xf.reshape(bsz, seq, d)
```

```python
import functools
import math

import jax
import jax.numpy as jnp
from jax import lax
from jax.experimental import pallas as pl
from jax.experimental.pallas import tpu as pltpu

F32 = jnp.float32
BF16 = jnp.bfloat16

D_MODEL = 1024
DEPTH = 4
N_MIXERS = 3
ATTN_GROUPS = ((128, 1), (512, 4), (2048, 16))
ATTN_HEADS = 8
ATTN_HEAD_DIM = D_MODEL // ATTN_HEADS
ATTN_BLOCK = 128
RET_HEADS = 4
RET_KEY_DIM = D_MODEL // RET_HEADS
RET_VAL_DIM = 2 * RET_KEY_DIM
RET_CHUNK = 128
SSM_GROUP = 16
SSM_GROUPS = D_MODEL // SSM_GROUP
SSM_STATE = 64
N_EXPERTS = 64
TOP_K = 8
N_EXPERT_GROUPS = 8
TOPK_GROUPS = 4
EXPERT_DIM = 256
SHARED_DIM = 256
ROUTED_SCALE = 2.5
DEEPNORM_ALPHA = (2 * DEPTH) ** 0.25
LN_EPS = 1e-5

LANES = 128
SUBLANES = 8
MASK_NEG = -1e30
VMEM_LIMIT = 56 * 1024 * 1024

ROW_TILE = 512
PROJ_TM = 1024
PROJ_TN = 1024
MOE_TILE = 1024
MOE_SLOTS = 160
MOE_GROUP = 16
SSM_TIME = 128
SSM_COLS = 128


def _cparams():
    return pltpu.CompilerParams(vmem_limit_bytes=VMEM_LIMIT)


def _dot(a, b):
    return jnp.dot(a, b, preferred_element_type=F32)


def _dot_nt(a, b):
    return lax.dot_general(a, b, (((1,), (1,)), ((), ())), preferred_element_type=F32)


def _dot_tn(a, b):
    return lax.dot_general(a, b, (((0,), (0,)), ((), ())), preferred_element_type=F32)


def _sigmoid(x):
    return 1.0 / (1.0 + jnp.exp(-x))


def _layer_norm(y, g, b):
    mu = jnp.mean(y, axis=-1, keepdims=True)
    yc = y - mu
    var = jnp.mean(yc * yc, axis=-1, keepdims=True)
    return yc * lax.rsqrt(var + LN_EPS) * g + b


def _store_ln(y, g_ref, b_ref, o_ref, ob_ref):
    out = _layer_norm(y, g_ref[...], b_ref[...])
    o_ref[...] = out
    ob_ref[...] = out.astype(BF16)


def _proj_kernel(x_ref, w_ref, o_ref):
    o_ref[...] = _dot(x_ref[...], w_ref[...]).astype(o_ref.dtype)


def _proj(xb, w, out_dtype, name):
    m, k = xb.shape
    n = w.shape[1]
    tm = min(PROJ_TM, m)
    tn = min(PROJ_TN, n)
    return pl.pallas_call(
        _proj_kernel,
        grid=(m // tm, n // tn),
        in_specs=[pl.BlockSpec((tm, k), lambda i, j: (i, 0)),
                  pl.BlockSpec((k, tn), lambda i, j: (0, j))],
        out_specs=pl.BlockSpec((tm, tn), lambda i, j: (i, j)),
        out_shape=jax.ShapeDtypeStruct((m, n), out_dtype),
        compiler_params=_cparams(),
        name=name,
    )(xb, w)


def _proj_perm_kernel(x_ref, w_ref, o_ref, xs_ref, col_ref, *, dil):
    rows = x_ref.shape[0] // dil

    @pl.when(pl.program_id(1) == 0)
    def _():
        for c in range(x_ref.shape[1] // LANES):
            cs = slice(c * LANES, (c + 1) * LANES)
            col_ref[...] = x_ref[:, cs]
            for r in range(dil):
                xs_ref[r * rows:(r + 1) * rows, cs] = col_ref[pl.ds(r, rows, stride=dil), :].astype(BF16)

    res = _dot(xs_ref[...], w_ref[...])
    o_ref[0] = res.reshape(dil, rows, res.shape[1]).astype(o_ref.dtype)


def _proj_perm(xf, w, dil, bsz, seq, name):
    k = xf.shape[1]
    n = w.shape[1]
    tm = min(PROJ_TM, seq)
    tn = min(PROJ_TN, n)
    assert seq % tm == 0 and tm % (dil * 2 * SUBLANES) == 0
    per_b = seq // tm
    rows = tm // dil
    return pl.pallas_call(
        functools.partial(_proj_perm_kernel, dil=dil),
        grid=(bsz * per_b, n // tn),
        in_specs=[pl.BlockSpec((tm, k), lambda i, j: (i, 0)),
                  pl.BlockSpec((k, tn), lambda i, j: (0, j))],
        out_specs=pl.BlockSpec((1, dil, rows, tn), lambda i, j: (i // per_b, 0, i % per_b, j)),
        out_shape=jax.ShapeDtypeStruct((bsz, dil, seq // dil, n), BF16),
        scratch_shapes=[pltpu.VMEM((tm, k), BF16), pltpu.VMEM((tm, LANES), F32)],
        compiler_params=_cparams(),
        name=name,
    )(xf, w)


def _attn_kernel(q_ref, kc_ref, kp_ref, vc_ref, vp_ref, bp_ref, bc_ref, o_ref, lse_ref,
                 o_scr, lse_scr, mix_scr, *, dil):
    n = pl.program_id(1)
    r = pl.program_id(2)
    first = jnp.where(n == 0, MASK_NEG, 0.0).astype(F32)
    scale = ATTN_HEAD_DIM ** -0.5
    lane = lax.broadcasted_iota(jnp.int32, (ATTN_BLOCK, LANES), 1)
    lse_tile = jnp.zeros((ATTN_BLOCK, LANES), F32)
    for h in range(ATTN_HEADS):
        sl = slice(h * ATTN_HEAD_DIM, (h + 1) * ATTN_HEAD_DIM)
        q = q_ref[0, 0, :, sl]
        sp = _dot_nt(q, kp_ref[0, 0, :, sl]) * scale + (bp_ref[h] + first)
        sc = _dot_nt(q, kc_ref[0, 0, :, sl]) * scale + bc_ref[h]
        m = jnp.maximum(jnp.max(sp, axis=1, keepdims=True), jnp.max(sc, axis=1, keepdims=True))
        pp = jnp.exp(sp - m)
        pc = jnp.exp(sc - m)
        l = jnp.sum(pp, axis=1, keepdims=True) + jnp.sum(pc, axis=1, keepdims=True)
        o = _dot(pp.astype(BF16), vp_ref[0, 0, :, sl]) + _dot(pc.astype(BF16), vc_ref[0, 0, :, sl])
        o_scr[h, r] = o / l
        lse_tile = jnp.where(lane == h, m + jnp.log(l), lse_tile)
    lse_scr[r] = lse_tile

    @pl.when(r == dil - 1)
    def _():
        for rr in range(dil):
            lse_ref[0, pl.ds(rr, ATTN_BLOCK, stride=dil), :] = lse_scr[rr]
        for h in range(ATTN_HEADS):
            for rr in range(dil):
                mix_scr[pl.ds(rr, ATTN_BLOCK, stride=dil), :] = o_scr[h, rr]
            o_ref[0, :, h * ATTN_HEAD_DIM:(h + 1) * ATTN_HEAD_DIM] = mix_scr[...]


def _attn_bias(window, dil):
    steps = window // dil
    assert steps <= ATTN_BLOCK
    qi = jnp.arange(ATTN_BLOCK)[:, None]
    kj = jnp.arange(ATTN_BLOCK)[None, :]
    slopes = 2.0 ** (-8.0 * (jnp.arange(ATTN_HEADS, dtype=F32) + 1.0) / ATTN_HEADS)

    def bias(step):
        valid = (step >= 0) & (step <= steps)
        dist = (step * dil).astype(F32)
        return jnp.where(valid[None], -(slopes[:, None, None] * dist[None]), MASK_NEG).astype(F32)

    return bias(qi - kj + ATTN_BLOCK), bias(qi - kj)


def _attn_group(qkv, gi, window, dil):
    b, _, ls, width = qkv.shape
    s = ls * dil
    hd = ATTN_HEADS * ATTN_HEAD_DIM
    assert ls % ATTN_BLOCK == 0 and width == 3 * hd
    nb = ls // ATTN_BLOCK
    bias_prev, bias_cur = _attn_bias(window, dil)

    def cur(c):
        return pl.BlockSpec((1, 1, ATTN_BLOCK, hd), lambda bi, n, r: (bi, r, n, c))

    def prev(c):
        return pl.BlockSpec((1, 1, ATTN_BLOCK, hd),
                            lambda bi, n, r: (bi, r, jnp.maximum(n - 1, 0), c))

    const = pl.BlockSpec((ATTN_HEADS, ATTN_BLOCK, ATTN_BLOCK), lambda bi, n, r: (0, 0, 0))
    o, lse = pl.pallas_call(
        functools.partial(_attn_kernel, dil=dil),
        grid=(b, nb, dil),
        in_specs=[cur(0), cur(1), prev(1), cur(2), prev(2), const, const],
        out_specs=[pl.BlockSpec((1, ATTN_BLOCK * dil, hd), lambda bi, n, r: (bi, n, 0)),
                   pl.BlockSpec((1, ATTN_BLOCK * dil, LANES), lambda bi, n, r: (bi, n, 0))],
        out_shape=[jax.ShapeDtypeStruct((b, s, hd), F32),
                   jax.ShapeDtypeStruct((b, s, LANES), F32)],
        scratch_shapes=[pltpu.VMEM((ATTN_HEADS, dil, ATTN_BLOCK, ATTN_HEAD_DIM), F32),
                        pltpu.VMEM((dil, ATTN_BLOCK, LANES), F32),
                        pltpu.VMEM((dil * ATTN_BLOCK, ATTN_HEAD_DIM), F32)],
        compiler_params=_cparams(),
        name=f"attn_g{gi}",
    )(qkv, qkv, qkv, qkv, qkv, bias_prev, bias_cur)
    return o.reshape(b * s, hd), lse.reshape(b * s, LANES)


def _attn_out_kernel(o1_ref, o2_ref, o3_ref, l1_ref, l2_ref, l3_ref, x_ref, w_ref, g_ref, b_ref,
                     o_ref, ob_ref):
    l1, l2, l3 = l1_ref[...], l2_ref[...], l3_ref[...]
    mx = jnp.maximum(jnp.maximum(l1, l2), l3)
    e1, e2, e3 = jnp.exp(l1 - mx), jnp.exp(l2 - mx), jnp.exp(l3 - mx)
    den = e1 + e2 + e3
    w1, w2, w3 = e1 / den, e2 / den, e3 / den
    cols = []
    for h in range(ATTN_HEADS):
        sl = slice(h * ATTN_HEAD_DIM, (h + 1) * ATTN_HEAD_DIM)
        oh = (w1[:, h:h + 1] * o1_ref[:, sl] + w2[:, h:h + 1] * o2_ref[:, sl]
              + w3[:, h:h + 1] * o3_ref[:, sl])
        cols.append(oh.astype(BF16))
    o = jnp.concatenate(cols, axis=1)
    y = _dot(o, w_ref[...]) + DEEPNORM_ALPHA * x_ref[...]
    _store_ln(y, g_ref, b_ref, o_ref, ob_ref)


def _row_spec(tm, width):
    return pl.BlockSpec((tm, width), lambda i: (i, 0))


def _const_spec(shape):
    return pl.BlockSpec(shape, lambda i: tuple(0 for _ in shape))


def _x_outs(t, tm):
    return dict(
        out_specs=[_row_spec(tm, D_MODEL), _row_spec(tm, D_MODEL)],
        out_shape=[jax.ShapeDtypeStruct((t, D_MODEL), F32), jax.ShapeDtypeStruct((t, D_MODEL), BF16)],
    )


def _attention_layer(x, xb, w_in, w_out, ln_g, ln_b, bsz, seq):
    t = x.shape[0]
    hd3 = 3 * ATTN_HEADS * ATTN_HEAD_DIM
    wb = w_in.astype(BF16)
    outs = []
    for gi, (window, dil) in enumerate(ATTN_GROUPS):
        qkv = _proj_perm(x, wb[:, gi * hd3:(gi + 1) * hd3], dil, bsz, seq, f"attn_proj_g{gi}")
        outs.append(_attn_group(qkv, gi, window, dil))
    tm = min(ROW_TILE, t)
    hd = ATTN_HEADS * ATTN_HEAD_DIM
    return pl.pallas_call(
        _attn_out_kernel,
        grid=(t // tm,),
        in_specs=[_row_spec(tm, hd)] * 3 + [_row_spec(tm, LANES)] * 3
                 + [_row_spec(tm, D_MODEL), _const_spec((hd, D_MODEL)),
                    _const_spec((1, D_MODEL)), _const_spec((1, D_MODEL))],
        compiler_params=_cparams(),
        name="attn_out",
        **_x_outs(t, tm),
    )(outs[0][0], outs[1][0], outs[2][0], outs[0][1], outs[1][1], outs[2][1],
      x, w_out.astype(BF16), ln_g[None], ln_b[None])


def _ret_kernel(q_ref, k_ref, v_ref, g_ref, dec_ref, xi_ref, zeta_ref, cd_ref, gg_ref, gb_ref,
                y_ref, state_ref):
    n = pl.program_id(2)

    @pl.when(n == 0)
    def _():
        state_ref[...] = jnp.zeros_like(state_ref)

    kscale = RET_KEY_DIM ** -0.5
    q = q_ref[0]
    k = k_ref[0]
    v = v_ref[0]
    scores = _dot_nt(q, k) * kscale * dec_ref[0]
    inner = _dot(scores.astype(BF16), v)
    st = state_ref[...]
    cross = _dot((q.astype(F32) * xi_ref[0]).astype(BF16), st.astype(BF16))
    kz = (k.astype(F32) * kscale * zeta_ref[0]).astype(BF16)
    state_ref[...] = cd_ref[0, 0:1, 0:1] * st + _dot_tn(kz, v)
    y = inner + cross
    mu = jnp.mean(y, axis=1, keepdims=True)
    yc = y - mu
    var = jnp.mean(yc * yc, axis=1, keepdims=True)
    yn = yc * lax.rsqrt(var + LN_EPS) * gg_ref[...] + gb_ref[...]
    g = g_ref[0]
    y_ref[0] = ((g * _sigmoid(g)) * yn).astype(y_ref.dtype)


def _ret_out_kernel(y_ref, x_ref, w_ref, g_ref, b_ref, o_ref, ob_ref):
    y = _dot(y_ref[...], w_ref[...]) + DEEPNORM_ALPHA * x_ref[...]
    _store_ln(y, g_ref, b_ref, o_ref, ob_ref)


def _retention_layer(x, xb, w_in, gn_g, gn_b, w_out, ln_g, ln_b, bsz, seq):
    t = x.shape[0]
    h, dk, dv, c = RET_HEADS, RET_KEY_DIM, RET_VAL_DIM, RET_CHUNK
    nqkv = 2 * h * dk + h * dv
    wb = w_in.astype(BF16)
    qkv = _proj(xb, wb[:, :nqkv], BF16, "ret_proj_qkv").reshape(bsz, seq, nqkv)
    gate = _proj(xb, wb[:, nqkv:], F32, "ret_proj_gate").reshape(bsz, seq, h * dv)
    log_gamma = jnp.log(1.0 - 2.0 ** (-5.0 - jnp.arange(h, dtype=F32)))
    pos = jnp.arange(c, dtype=F32)
    rel = pos[:, None] - pos[None, :]
    decay = jnp.where(rel >= 0, jnp.exp(log_gamma[:, None, None] * jnp.maximum(rel, 0.0)), 0.0)
    xi = jnp.exp(log_gamma[:, None] * (pos[None, :] + 1.0))
    zeta = jnp.exp(log_gamma[:, None] * (c - 1.0 - pos[None, :]))
    xi = jnp.broadcast_to(xi[:, :, None], (h, c, dk))
    zeta = jnp.broadcast_to(zeta[:, :, None], (h, c, dk))
    cd = jnp.broadcast_to(jnp.exp(log_gamma * c)[:, None, None], (h, SUBLANES, LANES))
    kq = (h * dk) // dk
    kv = (2 * h * dk) // dv
    head = lambda width: pl.BlockSpec((1, c, width), lambda bi, hi, n: (hi, 0, 0))
    y = pl.pallas_call(
        _ret_kernel,
        grid=(bsz, h, seq // c),
        in_specs=[pl.BlockSpec((1, c, dk), lambda bi, hi, n: (bi, n, hi)),
                  pl.BlockSpec((1, c, dk), lambda bi, hi, n: (bi, n, kq + hi)),
                  pl.BlockSpec((1, c, dv), lambda bi, hi, n: (bi, n, kv + hi)),
                  pl.BlockSpec((1, c, dv), lambda bi, hi, n: (bi, n, hi)),
                  head(c), head(dk), head(dk),
                  pl.BlockSpec((1, SUBLANES, LANES), lambda bi, hi, n: (hi, 0, 0)),
                  pl.BlockSpec((1, dv), lambda bi, hi, n: (0, hi)),
                  pl.BlockSpec((1, dv), lambda bi, hi, n: (0, hi))],
        out_specs=pl.BlockSpec((1, c, dv), lambda bi, hi, n: (bi, n, hi)),
        out_shape=jax.ShapeDtypeStruct((bsz, seq, h * dv), BF16),
        scratch_shapes=[pltpu.VMEM((dk, dv), F32)],
        compiler_params=_cparams(),
        name="ret_core",
    )(qkv, qkv, qkv, gate, decay, xi, zeta, cd, gn_g[None], gn_b[None])
    tm = min(ROW_TILE, t)
    return pl.pallas_call(
        _ret_out_kernel,
        grid=(t // tm,),
        in_specs=[_row_spec(tm, h * dv), _row_spec(tm, D_MODEL), _const_spec((h * dv, D_MODEL)),
                  _const_spec((1, D_MODEL)), _const_spec((1, D_MODEL))],
        compiler_params=_cparams(),
        name="ret_out",
        **_x_outs(t, tm),
    )(y.reshape(t, h * dv), x, w_out.astype(BF16), ln_g[None], ln_b[None])


def _gelu_tanh(y):
    return 0.5 * y * (1.0 + jnp.tanh(math.sqrt(2.0 / math.pi) * (y + 0.044715 * (y * y * y))))


def _s5_kernel(x_ref, wb_ref, wc_ref, pr_ref, pi_ref, d_ref, y_ref, cr_ref, ci_ref):
    n = pl.program_id(2)

    @pl.when(n == 0)
    def _():
        cr_ref[...] = jnp.zeros_like(cr_ref)
        ci_ref[...] = jnp.zeros_like(ci_ref)

    nch = pr_ref.shape[2]
    slabs = SSM_TIME // SUBLANES
    u = x_ref[0]
    bu = _dot(u.astype(BF16), wb_ref[0])
    re = bu[:, :nch].reshape(slabs, SUBLANES, nch)
    im = bu[:, nch:].reshape(slabs, SUBLANES, nch)
    pr = pr_ref[0]
    pi = pi_ref[0]
    row = lax.broadcasted_iota(jnp.int32, (1, SUBLANES, nch), 1)
    for k in (1, 2, 4):
        ar = pr[k - 1:k, :][None]
        ai = pi[k - 1:k, :][None]
        sre = pltpu.roll(re, k, 1)
        sim = pltpu.roll(im, k, 1)
        keep = row >= k
        re = re + jnp.where(keep, ar * sre - ai * sim, 0.0)
        im = im + jnp.where(keep, ar * sim + ai * sre, 0.0)
    cr = cr_ref[...]
    ci = ci_ref[...]
    hre, him = [], []
    for j in range(slabs):
        hr = re[j] + (pr * cr - pi * ci)
        hi = im[j] + (pr * ci + pi * cr)
        hre.append(hr)
        him.append(hi)
        cr = jnp.broadcast_to(hr[SUBLANES - 1:SUBLANES, :], (SUBLANES, nch))
        ci = jnp.broadcast_to(hi[SUBLANES - 1:SUBLANES, :], (SUBLANES, nch))
    cr_ref[...] = cr
    ci_ref[...] = ci
    hcat = jnp.concatenate([jnp.concatenate(hre, axis=0), jnp.concatenate(him, axis=0)], axis=1)
    y = _dot(hcat.astype(BF16), wc_ref[0]) + d_ref[...] * u
    y_ref[0] = _gelu_tanh(y).astype(y_ref.dtype)


def _s5_out_kernel(y_ref, x_ref, w_ref, g_ref, b_ref, o_ref, ob_ref):
    vg = _dot(y_ref[...], w_ref[...])
    val = vg[:, :D_MODEL]
    gate = vg[:, D_MODEL:]
    y = val * _sigmoid(gate) + DEEPNORM_ALPHA * x_ref[...]
    _store_ln(y, g_ref, b_ref, o_ref, ob_ref)


def _block_diag(m):
    nb, g, r, c = m.shape
    eye = jnp.eye(g, dtype=m.dtype)
    return (m[:, :, :, None, :] * eye[None, :, None, :, None]).reshape(nb, g * r, g * c)


def _s5_layer(x, a_re, a_im, b_re, b_im, c_re, c_im, d_skip, log_dt, w_glu, ln_g, ln_b, bsz, seq):
    t = x.shape[0]
    gpb = SSM_COLS // SSM_GROUP
    ncb = D_MODEL // SSM_COLS
    nch = gpb * SSM_STATE
    dt = jnp.exp(log_dt)[:, None]
    mag = jnp.exp(dt * a_re)
    ab_re = mag * jnp.cos(dt * a_im)
    ab_im = mag * jnp.sin(dt * a_im)
    den = a_re * a_re + a_im * a_im
    f_re = ((ab_re - 1.0) * a_re + ab_im * a_im) / den
    f_im = (ab_im * a_re - (ab_re - 1.0) * a_im) / den
    bb_re = f_re[..., None] * b_re - f_im[..., None] * b_im
    bb_im = f_re[..., None] * b_im + f_im[..., None] * b_re
    tb = lambda m: jnp.swapaxes(m, 1, 2).reshape(ncb, gpb, SSM_GROUP, SSM_STATE)
    w_b = jnp.concatenate([_block_diag(tb(bb_re)), _block_diag(tb(bb_im))], axis=2).astype(BF16)
    tc = lambda m: jnp.swapaxes(m, 1, 2).reshape(ncb, gpb, SSM_STATE, SSM_GROUP)
    w_c = jnp.concatenate([_block_diag(tc(c_re)), -_block_diag(tc(c_im))], axis=1).astype(BF16)
    pr, pi = [ab_re], [ab_im]
    for _ in range(SUBLANES - 1):
        pr, pi = pr + [pr[-1] * ab_re - pi[-1] * ab_im], pi + [pr[-1] * ab_im + pi[-1] * ab_re]
    pw_re = jnp.stack(pr, axis=0).reshape(SUBLANES, ncb, nch).transpose(1, 0, 2)
    pw_im = jnp.stack(pi, axis=0).reshape(SUBLANES, ncb, nch).transpose(1, 0, 2)
    x3 = x.reshape(bsz, seq, D_MODEL)
    blk = lambda shape: pl.BlockSpec((1,) + shape, lambda bi, cb, n: (cb, 0, 0))
    y = pl.pallas_call(
        _s5_kernel,
        grid=(bsz, ncb, seq // SSM_TIME),
        in_specs=[pl.BlockSpec((1, SSM_TIME, SSM_COLS), lambda bi, cb, n: (bi, n, cb)),
                  blk((SSM_COLS, 2 * nch)), blk((2 * nch, SSM_COLS)),
                  blk((SUBLANES, nch)), blk((SUBLANES, nch)),
                  pl.BlockSpec((1, SSM_COLS), lambda bi, cb, n: (0, cb))],
        out_specs=pl.BlockSpec((1, SSM_TIME, SSM_COLS), lambda bi, cb, n: (bi, n, cb)),
        out_shape=jax.ShapeDtypeStruct((bsz, seq, D_MODEL), BF16),
        scratch_shapes=[pltpu.VMEM((SUBLANES, nch), F32), pltpu.VMEM((SUBLANES, nch), F32)],
        compiler_params=_cparams(),
        name="s5_core",
    )(x3, w_b, w_c, pw_re, pw_im, d_skip.reshape(1, D_MODEL))
    tm = min(ROW_TILE, t)
    return pl.pallas_call(
        _s5_out_kernel,
        grid=(t // tm,),
        in_specs=[_row_spec(tm, D_MODEL), _row_spec(tm, D_MODEL), _const_spec((D_MODEL, 2 * D_MODEL)),
                  _const_spec((1, D_MODEL)), _const_spec((1, D_MODEL))],
        compiler_params=_cparams(),
        name="s5_out",
        **_x_outs(t, tm),
    )(y.reshape(t, D_MODEL), x, w_glu.astype(BF16), ln_g[None], ln_b[None])


def _cast_kernel(*refs):
    o_ref = refs[-1]
    o_ref[...] = jnp.concatenate([r[...].astype(o_ref.dtype) for r in refs[:-1]], axis=-1)


def _cast_bf16(*ws, name):
    e, r, c = ws[0].shape
    return pl.pallas_call(
        _cast_kernel,
        grid=(e,),
        in_specs=[pl.BlockSpec((1, r, c), lambda i: (i, 0, 0)) for _ in ws],
        out_specs=pl.BlockSpec((1, r, c * len(ws)), lambda i: (i, 0, 0)),
        out_shape=jax.ShapeDtypeStruct((e, r, c * len(ws)), BF16),
        compiler_params=_cparams(),
        name=name,
    )(*ws)


def _first_index(eq, iota, size, axis):
    return jnp.min(jnp.where(eq, iota, size), axis=axis, keepdims=True)


def _router_kernel(x_ref, w_ref, b_ref, tri_ref, rank_ref, gate_ref, cnt_ref):
    x = x_ref[...]
    w = w_ref[...]
    x_hi = x.astype(BF16)
    x_lo = (x - x_hi.astype(F32)).astype(BF16)
    w_hi = w.astype(BF16)
    w_lo = (w - w_hi.astype(F32)).astype(BF16)
    logits = _dot_nt(w_hi, x_hi) + (_dot_nt(w_hi, x_lo) + _dot_nt(w_lo, x_hi))
    scores = _sigmoid(logits)
    sel = scores + b_ref[...]
    tt = sel.shape[1]
    gsz = N_EXPERTS // N_EXPERT_GROUPS
    grp = sel.reshape(N_EXPERT_GROUPS, gsz, tt)
    wi = lax.broadcasted_iota(jnp.int32, grp.shape, 1)
    m1 = jnp.max(grp, axis=1, keepdims=True)
    i1 = _first_index(grp == m1, wi, gsz, 1)
    m2 = jnp.max(jnp.where(wi == i1, -jnp.inf, grp), axis=1, keepdims=True)
    gs = (m1 + m2).reshape(N_EXPERT_GROUPS, tt)
    gi = lax.broadcasted_iota(jnp.int32, gs.shape, 0)
    gmask = jnp.zeros(gs.shape, jnp.bool_)
    for _ in range(TOPK_GROUPS):
        gm = jnp.max(gs, axis=0, keepdims=True)
        pick = gi == _first_index(gs == gm, gi, N_EXPERT_GROUPS, 0)
        gmask = gmask | pick
        gs = jnp.where(pick, -jnp.inf, gs)
    emask = jnp.broadcast_to(gmask[:, None, :], grp.shape).reshape(N_EXPERTS, tt)
    cand = jnp.where(emask, sel, -jnp.inf)
    ei = lax.broadcasted_iota(jnp.int32, cand.shape, 0)
    chosen = jnp.zeros(cand.shape, jnp.bool_)
    for _ in range(TOP_K):
        cm = jnp.max(cand, axis=0, keepdims=True)
        pick = ei == _first_index(cand == cm, ei, N_EXPERTS, 0)
        chosen = chosen | pick
        cand = jnp.where(pick, -jnp.inf, cand)
    top_w = jnp.where(chosen, scores, 0.0)
    denom = jnp.sum(top_w, axis=0, keepdims=True)
    gate_ref[...] = top_w / denom * ROUTED_SCALE
    chosen_f = chosen.astype(F32)
    rank = _dot(chosen_f.astype(BF16), tri_ref[...])
    rank_ref[...] = jnp.where(chosen, rank, -1.0).astype(jnp.int32)
    cnt = jnp.sum(chosen_f, axis=1, keepdims=True)
    cnt_ref[0] = jnp.broadcast_to(cnt, (N_EXPERTS, LANES)).astype(jnp.int32)


def _moe_kernel(cnt_ref, x_ref, rank_ref, gate_ref, wgu_ref, wd_ref, o_ref, oh_scr, yg_scr):
    j = pl.program_id(0)
    e = pl.program_id(1)

    @pl.when(e == 0)
    def _():
        o_ref[...] = jnp.zeros_like(o_ref)

    tt = x_ref.shape[0]
    slot = lax.broadcasted_iota(jnp.int32, (MOE_SLOTS, tt), 0)

    def expert_chunk(c):
        hit = (rank_ref[0] - c * MOE_SLOTS) == slot
        onehot = hit.astype(F32).astype(BF16)
        xc = _dot(onehot, x_ref[...]).astype(BF16)
        hgu = _dot(xc, wgu_ref[0])
        hg = hgu[:, :EXPERT_DIM]
        hb = ((hg * _sigmoid(hg)) * hgu[:, EXPERT_DIM:]).astype(BF16)
        y = _dot(hb, wd_ref[0])
        gsel = jnp.sum(jnp.where(hit, gate_ref[0], 0.0), axis=1, keepdims=True)
        return onehot, (y * gsel).astype(BF16)

    g = e % MOE_GROUP
    row0 = pl.multiple_of(g * MOE_SLOTS, MOE_SLOTS)
    onehot, yg = expert_chunk(0)
    oh_scr[pl.ds(row0, MOE_SLOTS), :] = onehot
    yg_scr[pl.ds(row0, MOE_SLOTS), :] = yg

    n_chunks = (cnt_ref[j * N_EXPERTS + e] + MOE_SLOTS - 1) // MOE_SLOTS

    def overflow(c, carry):
        onehot_c, yg_c = expert_chunk(c)
        o_ref[...] += _dot_tn(onehot_c, yg_c)
        return carry

    lax.fori_loop(1, n_chunks, overflow, 0)

    @pl.when(g == MOE_GROUP - 1)
    def _():
        o_ref[...] += _dot_tn(oh_scr[...], yg_scr[...])


def _moe_out_kernel(x_ref, xb_ref, acc_ref, wgu_ref, wd_ref, g_ref, b_ref, o_ref, ob_ref):
    hgu = _dot(xb_ref[...], wgu_ref[...])
    hg = hgu[:, :SHARED_DIM]
    hb = ((hg * _sigmoid(hg)) * hgu[:, SHARED_DIM:]).astype(BF16)
    shared = _dot(hb, wd_ref[...])
    y = DEEPNORM_ALPHA * x_ref[...] + (acc_ref[...] + shared)
    _store_ln(y, g_ref, b_ref, o_ref, ob_ref)


def _moe_layer(x, xb, w_router, router_bias, w_gate, w_up, w_down, sw_gate, sw_up, sw_down,
               ln_g, ln_b):
    t = x.shape[0]
    tt = min(MOE_TILE, t)
    nj = t // tt
    tri = jnp.triu(jnp.ones((tt, tt), F32), 1).astype(BF16)
    rank, gate, cnt = pl.pallas_call(
        _router_kernel,
        grid=(nj,),
        in_specs=[_row_spec(tt, D_MODEL), _const_spec((N_EXPERTS, D_MODEL)),
                  _const_spec((N_EXPERTS, 1)), _const_spec((tt, tt))],
        out_specs=[pl.BlockSpec((N_EXPERTS, tt), lambda i: (0, i)),
                   pl.BlockSpec((N_EXPERTS, tt), lambda i: (0, i)),
                   pl.BlockSpec((1, N_EXPERTS, LANES), lambda i: (i, 0, 0))],
        out_shape=[jax.ShapeDtypeStruct((N_EXPERTS, t), jnp.int32),
                   jax.ShapeDtypeStruct((N_EXPERTS, t), F32),
                   jax.ShapeDtypeStruct((nj, N_EXPERTS, LANES), jnp.int32)],
        compiler_params=_cparams(),
        name="moe_router",
    )(x, w_router.T, router_bias[:, None], tri)
    wgu = _cast_bf16(w_gate, w_up, name="moe_cast_gu")
    wd = _cast_bf16(w_down, name="moe_cast_d")
    acc = pl.pallas_call(
        _moe_kernel,
        grid_spec=pltpu.PrefetchScalarGridSpec(
            num_scalar_prefetch=1,
            grid=(nj, N_EXPERTS),
            in_specs=[pl.BlockSpec((tt, D_MODEL), lambda j, e, c: (j, 0)),
                      pl.BlockSpec((1, 1, tt), lambda j, e, c: (e, 0, j)),
                      pl.BlockSpec((1, 1, tt), lambda j, e, c: (e, 0, j)),
                      pl.BlockSpec((1, D_MODEL, 2 * EXPERT_DIM), lambda j, e, c: (e, 0, 0)),
                      pl.BlockSpec((1, EXPERT_DIM, D_MODEL), lambda j, e, c: (e, 0, 0))],
            out_specs=pl.BlockSpec((tt, D_MODEL), lambda j, e, c: (j, 0)),
            scratch_shapes=[pltpu.VMEM((MOE_GROUP * MOE_SLOTS, tt), BF16),
                            pltpu.VMEM((MOE_GROUP * MOE_SLOTS, D_MODEL), BF16)],
        ),
        out_shape=jax.ShapeDtypeStruct((t, D_MODEL), F32),
        compiler_params=_cparams(),
        name="moe_experts",
    )(cnt[:, :, 0].reshape(-1), xb, rank.reshape(N_EXPERTS, 1, t), gate.reshape(N_EXPERTS, 1, t),
      wgu, wd)
    tm = min(ROW_TILE, t)
    sgu = jnp.concatenate([sw_gate, sw_up], axis=1).astype(BF16)
    return pl.pallas_call(
        _moe_out_kernel,
        grid=(t // tm,),
        in_specs=[_row_spec(tm, D_MODEL), _row_spec(tm, D_MODEL), _row_spec(tm, D_MODEL),
                  _const_spec((D_MODEL, 2 * SHARED_DIM)), _const_spec((SHARED_DIM, D_MODEL)),
                  _const_spec((1, D_MODEL)), _const_spec((1, D_MODEL))],
        compiler_params=_cparams(),
        name="moe_out",
        **_x_outs(t, tm),
    )(x, xb, acc, sgu, sw_down.astype(BF16), ln_g[None], ln_b[None])


def kernel(x, attn_w_in, attn_w_out, ret_w_in, ret_gn_g, ret_gn_b, ret_w_out, ssm_a_re, ssm_a_im, ssm_b_re, ssm_b_im, ssm_c_re, ssm_c_im, ssm_d, ssm_log_dt, ssm_w_glu, moe_w_router, moe_router_bias, moe_w_gate, moe_w_up, moe_w_down, shared_w_gate, shared_w_up, shared_w_down, ln_g, ln_b):
    bsz, seq, d = x.shape
    depth = ln_g.shape[0]
    xf = x.reshape(bsz * seq, d)
    xb = xf.astype(BF16)
    for i in range(depth):
        j = i // N_MIXERS
        kind = i % N_MIXERS
        if kind == 0:
            xf, xb = _attention_layer(xf, xb, attn_w_in[j], attn_w_out[j], ln_g[i, 0], ln_b[i, 0],
                                      bsz, seq)
        elif kind == 1:
            xf, xb = _retention_layer(xf, xb, ret_w_in[j], ret_gn_g[j], ret_gn_b[j], ret_w_out[j],
                                      ln_g[i, 0], ln_b[i, 0], bsz, seq)
        else:
            xf, xb = _s5_layer(xf, ssm_a_re[j], ssm_a_im[j], ssm_b_re[j], ssm_b_im[j], ssm_c_re[j],
                               ssm_c_im[j], ssm_d[j], ssm_log_dt[j], ssm_w_glu[j],
                               ln_g[i, 0], ln_b[i, 0], bsz, seq)
        xf, xb = _moe_layer(xf, xb, moe_w_router[i], moe_router_bias[i], moe_w_gate[i], moe_w_up[i],
                            moe_w_down[i], shared_w_gate[i], shared_w_up[i], shared_w_down[i],
                            ln_g[i, 1], ln_b[i, 1])
    return xf.reshape(bsz, seq, d)
```

```python
import functools
import math

import jax
import jax.numpy as jnp
from jax import lax
from jax.experimental import pallas as pl
from jax.experimental.pallas import tpu as pltpu

F32 = jnp.float32
BF16 = jnp.bfloat16

D_MODEL = 1024
DEPTH = 4
N_MIXERS = 3
ATTN_GROUPS = ((128, 1), (512, 4), (2048, 16))
ATTN_HEADS = 8
ATTN_HEAD_DIM = D_MODEL // ATTN_HEADS
ATTN_BLOCK = 128
RET_HEADS = 4
RET_KEY_DIM = D_MODEL // RET_HEADS
RET_VAL_DIM = 2 * RET_KEY_DIM
RET_CHUNK = 128
SSM_GROUP = 16
SSM_GROUPS = D_MODEL // SSM_GROUP
SSM_STATE = 64
N_EXPERTS = 64
TOP_K = 8
N_EXPERT_GROUPS = 8
TOPK_GROUPS = 4
EXPERT_DIM = 256
SHARED_DIM = 256
ROUTED_SCALE = 2.5
DEEPNORM_ALPHA = (2 * DEPTH) ** 0.25
LN_EPS = 1e-5

LANES = 128
SUBLANES = 8
MASK_NEG = -1e30
VMEM_LIMIT = 56 * 1024 * 1024

ROW_TILE = 512
PROJ_TM = 1024
PROJ_TN = 1024
MOE_TILE = 1024
MOE_SLOTS = 160
MOE_GROUP = 16
MOE_UNROLL = 2
SSM_TIME = 256
SSM_COLS = 128


def _cparams():
    return pltpu.CompilerParams(vmem_limit_bytes=VMEM_LIMIT)


def _dot(a, b):
    return jnp.dot(a, b, preferred_element_type=F32)


def _dot_nt(a, b):
    return lax.dot_general(a, b, (((1,), (1,)), ((), ())), preferred_element_type=F32)


def _dot_tn(a, b):
    return lax.dot_general(a, b, (((0,), (0,)), ((), ())), preferred_element_type=F32)


def _sigmoid(x):
    return 1.0 / (1.0 + jnp.exp(-x))


def _layer_norm(y, g, b):
    mu = jnp.mean(y, axis=-1, keepdims=True)
    yc = y - mu
    var = jnp.mean(yc * yc, axis=-1, keepdims=True)
    return yc * lax.rsqrt(var + LN_EPS) * g + b


def _store_ln(y, g_ref, b_ref, o_ref, ob_ref):
    out = _layer_norm(y, g_ref[...], b_ref[...])
    o_ref[...] = out
    ob_ref[...] = out.astype(BF16)


def _proj_kernel(x_ref, w_ref, o_ref):
    o_ref[...] = _dot(x_ref[...], w_ref[...]).astype(o_ref.dtype)


def _proj(xb, w, out_dtype, name):
    m, k = xb.shape
    n = w.shape[1]
    tm = min(PROJ_TM, m)
    tn = min(PROJ_TN, n)
    return pl.pallas_call(
        _proj_kernel,
        grid=(m // tm, n // tn),
        in_specs=[pl.BlockSpec((tm, k), lambda i, j: (i, 0)),
                  pl.BlockSpec((k, tn), lambda i, j: (0, j))],
        out_specs=pl.BlockSpec((tm, tn), lambda i, j: (i, j)),
        out_shape=jax.ShapeDtypeStruct((m, n), out_dtype),
        compiler_params=_cparams(),
        name=name,
    )(xb, w)


def _proj_perm_kernel(x_ref, w_ref, o_ref, xs_ref, col_ref, *, dil):
    rows = x_ref.shape[0] // dil

    @pl.when(pl.program_id(1) == 0)
    def _():
        for c in range(x_ref.shape[1] // LANES):
            cs = slice(c * LANES, (c + 1) * LANES)
            col_ref[...] = x_ref[:, cs]
            for r in range(dil):
                xs_ref[r * rows:(r + 1) * rows, cs] = col_ref[pl.ds(r, rows, stride=dil), :].astype(BF16)

    res = _dot(xs_ref[...], w_ref[...])
    o_ref[0] = res.reshape(dil, rows, res.shape[1]).astype(o_ref.dtype)


def _proj_perm(xf, w, dil, bsz, seq, name):
    k = xf.shape[1]
    n = w.shape[1]
    tm = min(PROJ_TM, seq)
    tn = min(PROJ_TN, n)
    assert seq % tm == 0 and tm % (dil * 2 * SUBLANES) == 0
    per_b = seq // tm
    rows = tm // dil
    return pl.pallas_call(
        functools.partial(_proj_perm_kernel, dil=dil),
        grid=(bsz * per_b, n // tn),
        in_specs=[pl.BlockSpec((tm, k), lambda i, j: (i, 0)),
                  pl.BlockSpec((k, tn), lambda i, j: (0, j))],
        out_specs=pl.BlockSpec((1, dil, rows, tn), lambda i, j: (i // per_b, 0, i % per_b, j)),
        out_shape=jax.ShapeDtypeStruct((bsz, dil, seq // dil, n), BF16),
        scratch_shapes=[pltpu.VMEM((tm, k), BF16), pltpu.VMEM((tm, LANES), F32)],
        compiler_params=_cparams(),
        name=name,
    )(xf, w)


def _attn_kernel(q_ref, kc_ref, kp_ref, vc_ref, vp_ref, bp_ref, bc_ref, o_ref, lse_ref,
                 o_scr, lse_scr, mix_scr, *, dil):
    n = pl.program_id(1)
    r = pl.program_id(2)
    first = jnp.where(n == 0, MASK_NEG, 0.0).astype(F32)
    scale = ATTN_HEAD_DIM ** -0.5
    lane = lax.broadcasted_iota(jnp.int32, (ATTN_BLOCK, LANES), 1)
    lse_tile = jnp.zeros((ATTN_BLOCK, LANES), F32)
    heads = range(ATTN_HEADS)
    sls = [slice(h * ATTN_HEAD_DIM, (h + 1) * ATTN_HEAD_DIM) for h in heads]
    sps = [_dot_nt(q_ref[0, 0, :, sls[h]], kp_ref[0, 0, :, sls[h]]) * scale + (bp_ref[h] + first)
           for h in heads]
    scs = [_dot_nt(q_ref[0, 0, :, sls[h]], kc_ref[0, 0, :, sls[h]]) * scale + bc_ref[h]
           for h in heads]
    ms = [jnp.max(jnp.maximum(sps[h], scs[h]), axis=1, keepdims=True) for h in heads]
    pps = [jnp.exp(sps[h] - ms[h]) for h in heads]
    pcs = [jnp.exp(scs[h] - ms[h]) for h in heads]
    ls = [jnp.sum(pps[h] + pcs[h], axis=1, keepdims=True) for h in heads]
    for h in heads:
        o = (_dot(pps[h].astype(BF16), vp_ref[0, 0, :, sls[h]])
             + _dot(pcs[h].astype(BF16), vc_ref[0, 0, :, sls[h]]))
        o_scr[h, r] = o / ls[h]
        lse_tile = jnp.where(lane == h, ms[h] + jnp.log(ls[h]), lse_tile)
    lse_scr[r] = lse_tile

    @pl.when(r == dil - 1)
    def _():
        for rr in range(dil):
            lse_ref[0, pl.ds(rr, ATTN_BLOCK, stride=dil), :] = lse_scr[rr]
        for h in range(ATTN_HEADS):
            for rr in range(dil):
                mix_scr[pl.ds(rr, ATTN_BLOCK, stride=dil), :] = o_scr[h, rr]
            o_ref[0, :, h * ATTN_HEAD_DIM:(h + 1) * ATTN_HEAD_DIM] = mix_scr[...]


def _attn_bias(window, dil):
    steps = window // dil
    assert steps <= ATTN_BLOCK
    qi = jnp.arange(ATTN_BLOCK)[:, None]
    kj = jnp.arange(ATTN_BLOCK)[None, :]
    slopes = 2.0 ** (-8.0 * (jnp.arange(ATTN_HEADS, dtype=F32) + 1.0) / ATTN_HEADS)

    def bias(step):
        valid = (step >= 0) & (step <= steps)
        dist = (step * dil).astype(F32)
        return jnp.where(valid[None], -(slopes[:, None, None] * dist[None]), MASK_NEG).astype(F32)

    return bias(qi - kj + ATTN_BLOCK), bias(qi - kj)


def _attn_group(qkv, gi, window, dil):
    b, _, ls, width = qkv.shape
    s = ls * dil
    hd = ATTN_HEADS * ATTN_HEAD_DIM
    assert ls % ATTN_BLOCK == 0 and width == 3 * hd
    nb = ls // ATTN_BLOCK
    bias_prev, bias_cur = _attn_bias(window, dil)

    def cur(c):
        return pl.BlockSpec((1, 1, ATTN_BLOCK, hd), lambda bi, n, r: (bi, r, n, c))

    def prev(c):
        return pl.BlockSpec((1, 1, ATTN_BLOCK, hd),
                            lambda bi, n, r: (bi, r, jnp.maximum(n - 1, 0), c))

    const = pl.BlockSpec((ATTN_HEADS, ATTN_BLOCK, ATTN_BLOCK), lambda bi, n, r: (0, 0, 0))
    o, lse = pl.pallas_call(
        functools.partial(_attn_kernel, dil=dil),
        grid=(b, nb, dil),
        in_specs=[cur(0), cur(1), prev(1), cur(2), prev(2), const, const],
        out_specs=[pl.BlockSpec((1, ATTN_BLOCK * dil, hd), lambda bi, n, r: (bi, n, 0)),
                   pl.BlockSpec((1, ATTN_BLOCK * dil, LANES), lambda bi, n, r: (bi, n, 0))],
        out_shape=[jax.ShapeDtypeStruct((b, s, hd), F32),
                   jax.ShapeDtypeStruct((b, s, LANES), F32)],
        scratch_shapes=[pltpu.VMEM((ATTN_HEADS, dil, ATTN_BLOCK, ATTN_HEAD_DIM), F32),
                        pltpu.VMEM((dil, ATTN_BLOCK, LANES), F32),
                        pltpu.VMEM((dil * ATTN_BLOCK, ATTN_HEAD_DIM), F32)],
        compiler_params=_cparams(),
        name=f"attn_g{gi}",
    )(qkv, qkv, qkv, qkv, qkv, bias_prev, bias_cur)
    return o.reshape(b * s, hd), lse.reshape(b * s, LANES)


def _attn_out_kernel(o1_ref, o2_ref, o3_ref, l1_ref, l2_ref, l3_ref, x_ref, w_ref, g_ref, b_ref,
                     o_ref, ob_ref):
    l1, l2, l3 = l1_ref[...], l2_ref[...], l3_ref[...]
    mx = jnp.maximum(jnp.maximum(l1, l2), l3)
    e1, e2, e3 = jnp.exp(l1 - mx), jnp.exp(l2 - mx), jnp.exp(l3 - mx)
    den = e1 + e2 + e3
    w1, w2, w3 = e1 / den, e2 / den, e3 / den
    cols = []
    for h in range(ATTN_HEADS):
        sl = slice(h * ATTN_HEAD_DIM, (h + 1) * ATTN_HEAD_DIM)
        oh = (w1[:, h:h + 1] * o1_ref[:, sl] + w2[:, h:h + 1] * o2_ref[:, sl]
              + w3[:, h:h + 1] * o3_ref[:, sl])
        cols.append(oh.astype(BF16))
    o = jnp.concatenate(cols, axis=1)
    y = _dot(o, w_ref[...]) + DEEPNORM_ALPHA * x_ref[...]
    _store_ln(y, g_ref, b_ref, o_ref, ob_ref)


def _row_spec(tm, width):
    return pl.BlockSpec((tm, width), lambda i: (i, 0))


def _const_spec(shape):
    return pl.BlockSpec(shape, lambda i: tuple(0 for _ in shape))


def _x_outs(t, tm):
    return dict(
        out_specs=[_row_spec(tm, D_MODEL), _row_spec(tm, D_MODEL)],
        out_shape=[jax.ShapeDtypeStruct((t, D_MODEL), F32), jax.ShapeDtypeStruct((t, D_MODEL), BF16)],
    )


def _attention_layer(x, xb, w_in, w_out, ln_g, ln_b, bsz, seq):
    t = x.shape[0]
    hd3 = 3 * ATTN_HEADS * ATTN_HEAD_DIM
    wb = w_in.astype(BF16)
    outs = []
    for gi, (window, dil) in enumerate(ATTN_GROUPS):
        qkv = _proj_perm(x, wb[:, gi * hd3:(gi + 1) * hd3], dil, bsz, seq, f"attn_proj_g{gi}")
        outs.append(_attn_group(qkv, gi, window, dil))
    tm = min(ROW_TILE, t)
    hd = ATTN_HEADS * ATTN_HEAD_DIM
    return pl.pallas_call(
        _attn_out_kernel,
        grid=(t // tm,),
        in_specs=[_row_spec(tm, hd)] * 3 + [_row_spec(tm, LANES)] * 3
                 + [_row_spec(tm, D_MODEL), _const_spec((hd, D_MODEL)),
                    _const_spec((1, D_MODEL)), _const_spec((1, D_MODEL))],
        compiler_params=_cparams(),
        name="attn_out",
        **_x_outs(t, tm),
    )(outs[0][0], outs[1][0], outs[2][0], outs[0][1], outs[1][1], outs[2][1],
      x, w_out.astype(BF16), ln_g[None], ln_b[None])


def _ret_kernel(q_ref, k_ref, v_ref, g_ref, dec_ref, xi_ref, zeta_ref, cd_ref, gg_ref, gb_ref,
                y_ref, state_ref):
    n = pl.program_id(2)

    @pl.when(n == 0)
    def _():
        state_ref[...] = jnp.zeros_like(state_ref)

    kscale = RET_KEY_DIM ** -0.5
    q = q_ref[0]
    k = k_ref[0]
    v = v_ref[0]
    scores = _dot_nt(q, k) * kscale * dec_ref[0]
    inner = _dot(scores.astype(BF16), v)
    st = state_ref[...]
    cross = _dot((q.astype(F32) * xi_ref[0]).astype(BF16), st.astype(BF16))
    kz = (k.astype(F32) * kscale * zeta_ref[0]).astype(BF16)
    state_ref[...] = cd_ref[0, 0:1, 0:1] * st + _dot_tn(kz, v)
    y = inner + cross
    mu = jnp.mean(y, axis=1, keepdims=True)
    yc = y - mu
    var = jnp.mean(yc * yc, axis=1, keepdims=True)
    yn = yc * lax.rsqrt(var + LN_EPS) * gg_ref[...] + gb_ref[...]
    g = g_ref[0]
    y_ref[0] = ((g * _sigmoid(g)) * yn).astype(y_ref.dtype)


def _ret_out_kernel(y_ref, x_ref, w_ref, g_ref, b_ref, o_ref, ob_ref):
    y = _dot(y_ref[...], w_ref[...]) + DEEPNORM_ALPHA * x_ref[...]
    _store_ln(y, g_ref, b_ref, o_ref, ob_ref)


def _retention_layer(x, xb, w_in, gn_g, gn_b, w_out, ln_g, ln_b, bsz, seq):
    t = x.shape[0]
    h, dk, dv, c = RET_HEADS, RET_KEY_DIM, RET_VAL_DIM, RET_CHUNK
    nqkv = 2 * h * dk + h * dv
    wb = w_in.astype(BF16)
    qkv = _proj(xb, wb[:, :nqkv], BF16, "ret_proj_qkv").reshape(bsz, seq, nqkv)
    gate = _proj(xb, wb[:, nqkv:], F32, "ret_proj_gate").reshape(bsz, seq, h * dv)
    log_gamma = jnp.log(1.0 - 2.0 ** (-5.0 - jnp.arange(h, dtype=F32)))
    pos = jnp.arange(c, dtype=F32)
    rel = pos[:, None] - pos[None, :]
    decay = jnp.where(rel >= 0, jnp.exp(log_gamma[:, None, None] * jnp.maximum(rel, 0.0)), 0.0)
    xi = jnp.exp(log_gamma[:, None] * (pos[None, :] + 1.0))
    zeta = jnp.exp(log_gamma[:, None] * (c - 1.0 - pos[None, :]))
    xi = jnp.broadcast_to(xi[:, :, None], (h, c, dk))
    zeta = jnp.broadcast_to(zeta[:, :, None], (h, c, dk))
    cd = jnp.broadcast_to(jnp.exp(log_gamma * c)[:, None, None], (h, SUBLANES, LANES))
    kq = (h * dk) // dk
    kv = (2 * h * dk) // dv
    head = lambda width: pl.BlockSpec((1, c, width), lambda bi, hi, n: (hi, 0, 0))
    y = pl.pallas_call(
        _ret_kernel,
        grid=(bsz, h, seq // c),
        in_specs=[pl.BlockSpec((1, c, dk), lambda bi, hi, n: (bi, n, hi)),
                  pl.BlockSpec((1, c, dk), lambda bi, hi, n: (bi, n, kq + hi)),
                  pl.BlockSpec((1, c, dv), lambda bi, hi, n: (bi, n, kv + hi)),
                  pl.BlockSpec((1, c, dv), lambda bi, hi, n: (bi, n, hi)),
                  head(c), head(dk), head(dk),
                  pl.BlockSpec((1, SUBLANES, LANES), lambda bi, hi, n: (hi, 0, 0)),
                  pl.BlockSpec((1, dv), lambda bi, hi, n: (0, hi)),
                  pl.BlockSpec((1, dv), lambda bi, hi, n: (0, hi))],
        out_specs=pl.BlockSpec((1, c, dv), lambda bi, hi, n: (bi, n, hi)),
        out_shape=jax.ShapeDtypeStruct((bsz, seq, h * dv), BF16),
        scratch_shapes=[pltpu.VMEM((dk, dv), F32)],
        compiler_params=_cparams(),
        name="ret_core",
    )(qkv, qkv, qkv, gate, decay, xi, zeta, cd, gn_g[None], gn_b[None])
    tm = min(ROW_TILE, t)
    return pl.pallas_call(
        _ret_out_kernel,
        grid=(t // tm,),
        in_specs=[_row_spec(tm, h * dv), _row_spec(tm, D_MODEL), _const_spec((h * dv, D_MODEL)),
                  _const_spec((1, D_MODEL)), _const_spec((1, D_MODEL))],
        compiler_params=_cparams(),
        name="ret_out",
        **_x_outs(t, tm),
    )(y.reshape(t, h * dv), x, w_out.astype(BF16), ln_g[None], ln_b[None])


def _gelu_tanh(y):
    return 0.5 * y * (1.0 + jnp.tanh(math.sqrt(2.0 / math.pi) * (y + 0.044715 * (y * y * y))))


def _s5_kernel(x_ref, wb_ref, wc_ref, pr_ref, pi_ref, d_ref, y_ref, cr_ref, ci_ref):
    n = pl.program_id(2)

    @pl.when(n == 0)
    def _():
        cr_ref[...] = jnp.zeros_like(cr_ref)
        ci_ref[...] = jnp.zeros_like(ci_ref)

    nch = pr_ref.shape[2]
    slabs = SSM_TIME // SUBLANES
    u = x_ref[0]
    bu = _dot(u.astype(BF16), wb_ref[0])
    re = bu[:, :nch].reshape(slabs, SUBLANES, nch)
    im = bu[:, nch:].reshape(slabs, SUBLANES, nch)
    pr = pr_ref[0]
    pi = pi_ref[0]
    row = lax.broadcasted_iota(jnp.int32, (1, SUBLANES, nch), 1)
    for k in (1, 2, 4):
        ar = pr[k - 1:k, :][None]
        ai = pi[k - 1:k, :][None]
        sre = pltpu.roll(re, k, 1)
        sim = pltpu.roll(im, k, 1)
        keep = row >= k
        re = re + jnp.where(keep, ar * sre - ai * sim, 0.0)
        im = im + jnp.where(keep, ar * sim + ai * sre, 0.0)
    cr = cr_ref[...]
    ci = ci_ref[...]
    hre, him = [], []
    for j in range(slabs):
        hr = re[j] + (pr * cr - pi * ci)
        hi = im[j] + (pr * ci + pi * cr)
        hre.append(hr)
        him.append(hi)
        cr = jnp.broadcast_to(hr[SUBLANES - 1:SUBLANES, :], (SUBLANES, nch))
        ci = jnp.broadcast_to(hi[SUBLANES - 1:SUBLANES, :], (SUBLANES, nch))
    cr_ref[...] = cr
    ci_ref[...] = ci
    hcat = jnp.concatenate([jnp.concatenate(hre, axis=0), jnp.concatenate(him, axis=0)], axis=1)
    y = _dot(hcat.astype(BF16), wc_ref[0]) + d_ref[...] * u
    y_ref[0] = _gelu_tanh(y).astype(y_ref.dtype)


def _s5_out_kernel(y_ref, x_ref, w_ref, g_ref, b_ref, o_ref, ob_ref):
    vg = _dot(y_ref[...], w_ref[...])
    val = vg[:, :D_MODEL]
    gate = vg[:, D_MODEL:]
    y = val * _sigmoid(gate) + DEEPNORM_ALPHA * x_ref[...]
    _store_ln(y, g_ref, b_ref, o_ref, ob_ref)


def _block_diag(m):
    nb, g, r, c = m.shape
    eye = jnp.eye(g, dtype=m.dtype)
    return (m[:, :, :, None, :] * eye[None, :, None, :, None]).reshape(nb, g * r, g * c)


def _s5_layer(x, a_re, a_im, b_re, b_im, c_re, c_im, d_skip, log_dt, w_glu, ln_g, ln_b, bsz, seq):
    t = x.shape[0]
    gpb = SSM_COLS // SSM_GROUP
    ncb = D_MODEL // SSM_COLS
    nch = gpb * SSM_STATE
    dt = jnp.exp(log_dt)[:, None]
    mag = jnp.exp(dt * a_re)
    ab_re = mag * jnp.cos(dt * a_im)
    ab_im = mag * jnp.sin(dt * a_im)
    den = a_re * a_re + a_im * a_im
    f_re = ((ab_re - 1.0) * a_re + ab_im * a_im) / den
    f_im = (ab_im * a_re - (ab_re - 1.0) * a_im) / den
    bb_re = f_re[..., None] * b_re - f_im[..., None] * b_im
    bb_im = f_re[..., None] * b_im + f_im[..., None] * b_re
    tb = lambda m: jnp.swapaxes(m, 1, 2).reshape(ncb, gpb, SSM_GROUP, SSM_STATE)
    w_b = jnp.concatenate([_block_diag(tb(bb_re)), _block_diag(tb(bb_im))], axis=2).astype(BF16)
    tc = lambda m: jnp.swapaxes(m, 1, 2).reshape(ncb, gpb, SSM_STATE, SSM_GROUP)
    w_c = jnp.concatenate([_block_diag(tc(c_re)), -_block_diag(tc(c_im))], axis=1).astype(BF16)
    pr, pi = [ab_re], [ab_im]
    for _ in range(SUBLANES - 1):
        pr, pi = pr + [pr[-1] * ab_re - pi[-1] * ab_im], pi + [pr[-1] * ab_im + pi[-1] * ab_re]
    pw_re = jnp.stack(pr, axis=0).reshape(SUBLANES, ncb, nch).transpose(1, 0, 2)
    pw_im = jnp.stack(pi, axis=0).reshape(SUBLANES, ncb, nch).transpose(1, 0, 2)
    x3 = x.reshape(bsz, seq, D_MODEL)
    blk = lambda shape: pl.BlockSpec((1,) + shape, lambda bi, cb, n: (cb, 0, 0))
    y = pl.pallas_call(
        _s5_kernel,
        grid=(bsz, ncb, seq // SSM_TIME),
        in_specs=[pl.BlockSpec((1, SSM_TIME, SSM_COLS), lambda bi, cb, n: (bi, n, cb)),
                  blk((SSM_COLS, 2 * nch)), blk((2 * nch, SSM_COLS)),
                  blk((SUBLANES, nch)), blk((SUBLANES, nch)),
                  pl.BlockSpec((1, SSM_COLS), lambda bi, cb, n: (0, cb))],
        out_specs=pl.BlockSpec((1, SSM_TIME, SSM_COLS), lambda bi, cb, n: (bi, n, cb)),
        out_shape=jax.ShapeDtypeStruct((bsz, seq, D_MODEL), BF16),
        scratch_shapes=[pltpu.VMEM((SUBLANES, nch), F32), pltpu.VMEM((SUBLANES, nch), F32)],
        compiler_params=_cparams(),
        name="s5_core",
    )(x3, w_b, w_c, pw_re, pw_im, d_skip.reshape(1, D_MODEL))
    tm = min(ROW_TILE, t)
    return pl.pallas_call(
        _s5_out_kernel,
        grid=(t // tm,),
        in_specs=[_row_spec(tm, D_MODEL), _row_spec(tm, D_MODEL), _const_spec((D_MODEL, 2 * D_MODEL)),
                  _const_spec((1, D_MODEL)), _const_spec((1, D_MODEL))],
        compiler_params=_cparams(),
        name="s5_out",
        **_x_outs(t, tm),
    )(y.reshape(t, D_MODEL), x, w_glu.astype(BF16), ln_g[None], ln_b[None])


def _cast_kernel(*refs):
    o_ref = refs[-1]
    o_ref[...] = jnp.concatenate([r[0].astype(o_ref.dtype) for r in refs[:-1]], axis=-1)


def _cast_bf16(*ws, layer, name):
    _, e, r, c = ws[0].shape
    return pl.pallas_call(
        _cast_kernel,
        grid=(e,),
        in_specs=[pl.BlockSpec((1, 1, r, c), lambda i: (layer, i, 0, 0)) for _ in ws],
        out_specs=pl.BlockSpec((1, r, c * len(ws)), lambda i: (i, 0, 0)),
        out_shape=jax.ShapeDtypeStruct((e, r, c * len(ws)), BF16),
        compiler_params=_cparams(),
        name=name,
    )(*ws)


def _first_index(eq, iota, size, axis):
    return jnp.min(jnp.where(eq, iota, size), axis=axis, keepdims=True)


def _router_kernel(x_ref, w_ref, b_ref, tri_ref, rank_ref, gate_ref, cnt_ref):
    x = x_ref[...]
    w = w_ref[...]
    x_hi = x.astype(BF16)
    x_lo = (x - x_hi.astype(F32)).astype(BF16)
    w_hi = w.astype(BF16)
    w_lo = (w - w_hi.astype(F32)).astype(BF16)
    logits = _dot_nt(w_hi, x_hi) + (_dot_nt(w_hi, x_lo) + _dot_nt(w_lo, x_hi))
    scores = _sigmoid(logits)
    sel = scores + b_ref[...]
    tt = sel.shape[1]
    gsz = N_EXPERTS // N_EXPERT_GROUPS
    grp = sel.reshape(N_EXPERT_GROUPS, gsz, tt)
    wi = lax.broadcasted_iota(jnp.int32, grp.shape, 1)
    m1 = jnp.max(grp, axis=1, keepdims=True)
    i1 = _first_index(grp == m1, wi, gsz, 1)
    m2 = jnp.max(jnp.where(wi == i1, -jnp.inf, grp), axis=1, keepdims=True)
    gs = (m1 + m2).reshape(N_EXPERT_GROUPS, tt)
    gi = lax.broadcasted_iota(jnp.int32, gs.shape, 0)
    gmask = jnp.zeros(gs.shape, jnp.bool_)
    for _ in range(TOPK_GROUPS):
        gm = jnp.max(gs, axis=0, keepdims=True)
        pick = gi == _first_index(gs == gm, gi, N_EXPERT_GROUPS, 0)
        gmask = gmask | pick
        gs = jnp.where(pick, -jnp.inf, gs)
    emask = jnp.broadcast_to(gmask[:, None, :], grp.shape).reshape(N_EXPERTS, tt)
    cand = jnp.where(emask, sel, -jnp.inf)
    ei = lax.broadcasted_iota(jnp.int32, cand.shape, 0)
    chosen = jnp.zeros(cand.shape, jnp.bool_)
    for _ in range(TOP_K):
        cm = jnp.max(cand, axis=0, keepdims=True)
        pick = ei == _first_index(cand == cm, ei, N_EXPERTS, 0)
        chosen = chosen | pick
        cand = jnp.where(pick, -jnp.inf, cand)
    top_w = jnp.where(chosen, scores, 0.0)
    denom = jnp.sum(top_w, axis=0, keepdims=True)
    gate_ref[...] = top_w / denom * ROUTED_SCALE
    chosen_f = chosen.astype(F32)
    rank = _dot(chosen_f.astype(BF16), tri_ref[...])
    rank_ref[...] = jnp.where(chosen, rank, -1.0).astype(jnp.int32)
    cnt = jnp.sum(chosen_f, axis=1, keepdims=True)
    cnt_ref[0] = jnp.broadcast_to(cnt, (N_EXPERTS, LANES)).astype(jnp.int32)


def _moe_kernel(cnt_ref, x_ref, rank_ref, gate_ref, wgu_ref, wd_ref, o_ref, oh_scr, yg_scr):
    j = pl.program_id(0)
    s = pl.program_id(1)

    @pl.when(s == 0)
    def _():
        o_ref[...] = jnp.zeros_like(o_ref)

    tt = x_ref.shape[0]
    slot = lax.broadcasted_iota(jnp.int32, (MOE_SLOTS, tt), 0)

    def expert_chunks(us, c):
        hits = [(rank_ref[u] - c * MOE_SLOTS) == slot for u in us]
        onehots = [hit.astype(F32).astype(BF16) for hit in hits]
        xcs = [_dot(oh, x_ref[...]).astype(BF16) for oh in onehots]
        hgus = [_dot(xc, wgu_ref[u]) for xc, u in zip(xcs, us)]
        hbs = [((h[:, :EXPERT_DIM] * _sigmoid(h[:, :EXPERT_DIM])) * h[:, EXPERT_DIM:]).astype(BF16)
               for h in hgus]
        ys = [_dot(hb, wd_ref[u]) for hb, u in zip(hbs, us)]
        gsels = [jnp.sum(jnp.where(hit, gate_ref[u], 0.0), axis=1, keepdims=True)
                 for hit, u in zip(hits, us)]
        return onehots, [(y * g).astype(BF16) for y, g in zip(ys, gsels)]

    steps_per_group = MOE_GROUP // MOE_UNROLL
    gs = s % steps_per_group
    us = list(range(MOE_UNROLL))
    onehots, ygs = expert_chunks(us, 0)
    for u in us:
        row0 = pl.multiple_of((gs * MOE_UNROLL + u) * MOE_SLOTS, MOE_SLOTS)
        oh_scr[pl.ds(row0, MOE_SLOTS), :] = onehots[u]
        yg_scr[pl.ds(row0, MOE_SLOTS), :] = ygs[u]

    for u in us:
        cnt = cnt_ref[j * N_EXPERTS + s * MOE_UNROLL + u]
        n_chunks = (cnt + MOE_SLOTS - 1) // MOE_SLOTS

        def overflow(c, carry, u=u):
            onehot_c, yg_c = expert_chunks([u], c)
            o_ref[...] += _dot_tn(onehot_c[0], yg_c[0])
            return carry

        lax.fori_loop(1, n_chunks, overflow, 0)

    @pl.when(gs == steps_per_group - 1)
    def _():
        o_ref[...] += _dot_tn(oh_scr[...], yg_scr[...])


def _moe_out_kernel(x_ref, xb_ref, acc_ref, wgu_ref, wd_ref, g_ref, b_ref, o_ref, ob_ref):
    hgu = _dot(xb_ref[...], wgu_ref[...])
    hg = hgu[:, :SHARED_DIM]
    hb = ((hg * _sigmoid(hg)) * hgu[:, SHARED_DIM:]).astype(BF16)
    shared = _dot(hb, wd_ref[...])
    y = DEEPNORM_ALPHA * x_ref[...] + (acc_ref[...] + shared)
    _store_ln(y, g_ref, b_ref, o_ref, ob_ref)


def _moe_layer(x, xb, w_router, router_bias, w_gate, w_up, w_down, sw_gate, sw_up, sw_down,
               ln_g, ln_b, layer):
    t = x.shape[0]
    tt = min(MOE_TILE, t)
    nj = t // tt
    tri = jnp.triu(jnp.ones((tt, tt), F32), 1).astype(BF16)
    rank, gate, cnt = pl.pallas_call(
        _router_kernel,
        grid=(nj,),
        in_specs=[_row_spec(tt, D_MODEL), _const_spec((N_EXPERTS, D_MODEL)),
                  _const_spec((N_EXPERTS, 1)), _const_spec((tt, tt))],
        out_specs=[pl.BlockSpec((N_EXPERTS, tt), lambda i: (0, i)),
                   pl.BlockSpec((N_EXPERTS, tt), lambda i: (0, i)),
                   pl.BlockSpec((1, N_EXPERTS, LANES), lambda i: (i, 0, 0))],
        out_shape=[jax.ShapeDtypeStruct((N_EXPERTS, t), jnp.int32),
                   jax.ShapeDtypeStruct((N_EXPERTS, t), F32),
                   jax.ShapeDtypeStruct((nj, N_EXPERTS, LANES), jnp.int32)],
        compiler_params=_cparams(),
        name="moe_router",
    )(x, w_router.T, router_bias[:, None], tri)
    wgu = _cast_bf16(w_gate, w_up, layer=layer, name="moe_cast_gu")
    wd = _cast_bf16(w_down, layer=layer, name="moe_cast_d")
    acc = pl.pallas_call(
        _moe_kernel,
        grid_spec=pltpu.PrefetchScalarGridSpec(
            num_scalar_prefetch=1,
            grid=(nj, N_EXPERTS // MOE_UNROLL),
            in_specs=[pl.BlockSpec((tt, D_MODEL), lambda j, e, c: (j, 0)),
                      pl.BlockSpec((MOE_UNROLL, 1, tt), lambda j, e, c: (e, 0, j)),
                      pl.BlockSpec((MOE_UNROLL, 1, tt), lambda j, e, c: (e, 0, j)),
                      pl.BlockSpec((MOE_UNROLL, D_MODEL, 2 * EXPERT_DIM), lambda j, e, c: (e, 0, 0)),
                      pl.BlockSpec((MOE_UNROLL, EXPERT_DIM, D_MODEL), lambda j, e, c: (e, 0, 0))],
            out_specs=pl.BlockSpec((tt, D_MODEL), lambda j, e, c: (j, 0)),
            scratch_shapes=[pltpu.VMEM((MOE_GROUP * MOE_SLOTS, tt), BF16),
                            pltpu.VMEM((MOE_GROUP * MOE_SLOTS, D_MODEL), BF16)],
        ),
        out_shape=jax.ShapeDtypeStruct((t, D_MODEL), F32),
        compiler_params=_cparams(),
        name="moe_experts",
    )(cnt[:, :, 0].reshape(-1), xb, rank.reshape(N_EXPERTS, 1, t), gate.reshape(N_EXPERTS, 1, t),
      wgu, wd)
    tm = min(ROW_TILE, t)
    sgu = jnp.concatenate([sw_gate, sw_up], axis=1).astype(BF16)
    return pl.pallas_call(
        _moe_out_kernel,
        grid=(t // tm,),
        in_specs=[_row_spec(tm, D_MODEL), _row_spec(tm, D_MODEL), _row_spec(tm, D_MODEL),
                  _const_spec((D_MODEL, 2 * SHARED_DIM)), _const_spec((SHARED_DIM, D_MODEL)),
                  _const_spec((1, D_MODEL)), _const_spec((1, D_MODEL))],
        compiler_params=_cparams(),
        name="moe_out",
        **_x_outs(t, tm),
    )(x, xb, acc, sgu, sw_down.astype(BF16), ln_g[None], ln_b[None])


def kernel(x, attn_w_in, attn_w_out, ret_w_in, ret_gn_g, ret_gn_b, ret_w_out, ssm_a_re, ssm_a_im, ssm_b_re, ssm_b_im, ssm_c_re, ssm_c_im, ssm_d, ssm_log_dt, ssm_w_glu, moe_w_router, moe_router_bias, moe_w_gate, moe_w_up, moe_w_down, shared_w_gate, shared_w_up, shared_w_down, ln_g, ln_b):
    bsz, seq, d = x.shape
    depth = ln_g.shape[0]
    xf = x.reshape(bsz * seq, d)
    xb = xf.astype(BF16)
    for i in range(depth):
        j = i // N_MIXERS
        kind = i % N_MIXERS
        if kind == 0:
            xf, xb = _attention_layer(xf, xb, attn_w_in[j], attn_w_out[j], ln_g[i, 0], ln_b[i, 0],
                                      bsz, seq)
        elif kind == 1:
            xf, xb = _retention_layer(xf, xb, ret_w_in[j], ret_gn_g[j], ret_gn_b[j], ret_w_out[j],
                                      ln_g[i, 0], ln_b[i, 0], bsz, seq)
        else:
            xf, xb = _s5_layer(xf, ssm_a_re[j], ssm_a_im[j], ssm_b_re[j], ssm_b_im[j], ssm_c_re[j],
                               ssm_c_im[j], ssm_d[j], ssm_log_dt[j], ssm_w_glu[j],
                               ln_g[i, 0], ln_b[i, 0], bsz, seq)
        xf, xb = _moe_layer(xf, xb, moe_w_router[i], moe_router_bias[i], moe_w_gate, moe_w_up,
                            moe_w_down, shared_w_gate[i], shared_w_up[i], shared_w_down[i],
                            ln_g[i, 1], ln_b[i, 1], i)
    return xf.reshape(bsz, seq, d)
```

```python
import functools
import math

import jax
import jax.numpy as jnp
from jax import lax
from jax.experimental import pallas as pl
from jax.experimental.pallas import tpu as pltpu

F32 = jnp.float32
BF16 = jnp.bfloat16

D_MODEL = 1024
DEPTH = 4
N_MIXERS = 3
ATTN_GROUPS = ((128, 1), (512, 4), (2048, 16))
ATTN_HEADS = 8
ATTN_HEAD_DIM = D_MODEL // ATTN_HEADS
ATTN_BLOCK = 128
RET_HEADS = 4
RET_KEY_DIM = D_MODEL // RET_HEADS
RET_VAL_DIM = 2 * RET_KEY_DIM
RET_CHUNK = 128
SSM_GROUP = 16
SSM_GROUPS = D_MODEL // SSM_GROUP
SSM_STATE = 64
N_EXPERTS = 64
TOP_K = 8
N_EXPERT_GROUPS = 8
TOPK_GROUPS = 4
EXPERT_DIM = 256
SHARED_DIM = 256
ROUTED_SCALE = 2.5
DEEPNORM_ALPHA = (2 * DEPTH) ** 0.25
LN_EPS = 1e-5

LANES = 128
SUBLANES = 8
MASK_NEG = -1e30
VMEM_LIMIT = 56 * 1024 * 1024

ROW_TILE = 512
PROJ_TM = 1024
PROJ_TN = 1024
MOE_TILE = 1024
MOE_SLOTS = 160
MOE_GROUP = 16
MOE_UNROLL = 4
SSM_TIME = 512
SSM_COLS = 128


def _cparams():
    return pltpu.CompilerParams(vmem_limit_bytes=VMEM_LIMIT)


def _dot(a, b):
    return jnp.dot(a, b, preferred_element_type=F32)


def _dot_nt(a, b):
    return lax.dot_general(a, b, (((1,), (1,)), ((), ())), preferred_element_type=F32)


def _dot_tn(a, b):
    return lax.dot_general(a, b, (((0,), (0,)), ((), ())), preferred_element_type=F32)


def _sigmoid(x):
    return 1.0 / (1.0 + jnp.exp(-x))


def _layer_norm(y, g, b):
    mu = jnp.mean(y, axis=-1, keepdims=True)
    yc = y - mu
    var = jnp.mean(yc * yc, axis=-1, keepdims=True)
    return yc * lax.rsqrt(var + LN_EPS) * g + b


def _store_ln(y, g_ref, b_ref, o_ref, ob_ref):
    out = _layer_norm(y, g_ref[...], b_ref[...])
    o_ref[...] = out
    ob_ref[...] = out.astype(BF16)


def _proj_kernel(x_ref, w_ref, o_ref):
    o_ref[...] = _dot(x_ref[...], w_ref[...]).astype(o_ref.dtype)


def _proj(xb, w, out_dtype, name):
    m, k = xb.shape
    n = w.shape[1]
    tm = min(PROJ_TM, m)
    tn = min(PROJ_TN, n)
    return pl.pallas_call(
        _proj_kernel,
        grid=(m // tm, n // tn),
        in_specs=[pl.BlockSpec((tm, k), lambda i, j: (i, 0)),
                  pl.BlockSpec((k, tn), lambda i, j: (0, j))],
        out_specs=pl.BlockSpec((tm, tn), lambda i, j: (i, j)),
        out_shape=jax.ShapeDtypeStruct((m, n), out_dtype),
        compiler_params=_cparams(),
        name=name,
    )(xb, w)


def _proj_perm_kernel(x_ref, w_ref, o_ref, xs_ref, col_ref, *, dil):
    rows = x_ref.shape[0] // dil

    @pl.when(pl.program_id(1) == 0)
    def _():
        for c in range(x_ref.shape[1] // LANES):
            cs = slice(c * LANES, (c + 1) * LANES)
            col_ref[...] = x_ref[:, cs]
            for r in range(dil):
                xs_ref[r * rows:(r + 1) * rows, cs] = col_ref[pl.ds(r, rows, stride=dil), :].astype(BF16)

    res = _dot(xs_ref[...], w_ref[...])
    o_ref[0] = res.reshape(dil, rows, res.shape[1]).astype(o_ref.dtype)


def _proj_perm(xf, w, dil, bsz, seq, name):
    k = xf.shape[1]
    n = w.shape[1]
    tm = min(PROJ_TM, seq)
    tn = min(PROJ_TN, n)
    assert seq % tm == 0 and tm % (dil * 2 * SUBLANES) == 0
    per_b = seq // tm
    rows = tm // dil
    return pl.pallas_call(
        functools.partial(_proj_perm_kernel, dil=dil),
        grid=(bsz * per_b, n // tn),
        in_specs=[pl.BlockSpec((tm, k), lambda i, j: (i, 0)),
                  pl.BlockSpec((k, tn), lambda i, j: (0, j))],
        out_specs=pl.BlockSpec((1, dil, rows, tn), lambda i, j: (i // per_b, 0, i % per_b, j)),
        out_shape=jax.ShapeDtypeStruct((bsz, dil, seq // dil, n), BF16),
        scratch_shapes=[pltpu.VMEM((tm, k), BF16), pltpu.VMEM((tm, LANES), F32)],
        compiler_params=_cparams(),
        name=name,
    )(xf, w)


def _attn_kernel(q_ref, kc_ref, kp_ref, vc_ref, vp_ref, bp_ref, bc_ref, o_ref, lse_ref,
                 o_scr, lse_scr, mix_scr, *, dil):
    n = pl.program_id(1)
    r = pl.program_id(2)
    first = jnp.where(n == 0, MASK_NEG, 0.0).astype(F32)
    scale = ATTN_HEAD_DIM ** -0.5
    lane = lax.broadcasted_iota(jnp.int32, (ATTN_BLOCK, LANES), 1)
    lse_tile = jnp.zeros((ATTN_BLOCK, LANES), F32)
    heads = range(ATTN_HEADS)
    sls = [slice(h * ATTN_HEAD_DIM, (h + 1) * ATTN_HEAD_DIM) for h in heads]
    sps = [_dot_nt(q_ref[0, 0, :, sls[h]], kp_ref[0, 0, :, sls[h]]) * scale + (bp_ref[h] + first)
           for h in heads]
    scs = [_dot_nt(q_ref[0, 0, :, sls[h]], kc_ref[0, 0, :, sls[h]]) * scale + bc_ref[h]
           for h in heads]
    ms = [jnp.max(jnp.maximum(sps[h], scs[h]), axis=1, keepdims=True) for h in heads]
    pps = [jnp.exp(sps[h] - ms[h]) for h in heads]
    pcs = [jnp.exp(scs[h] - ms[h]) for h in heads]
    ls = [jnp.sum(pps[h] + pcs[h], axis=1, keepdims=True) for h in heads]
    for h in heads:
        o = (_dot(pps[h].astype(BF16), vp_ref[0, 0, :, sls[h]])
             + _dot(pcs[h].astype(BF16), vc_ref[0, 0, :, sls[h]]))
        o_scr[h, r] = o / ls[h]
        lse_tile = jnp.where(lane == h, ms[h] + jnp.log(ls[h]), lse_tile)
    lse_scr[r] = lse_tile

    @pl.when(r == dil - 1)
    def _():
        for rr in range(dil):
            lse_ref[0, pl.ds(rr, ATTN_BLOCK, stride=dil), :] = lse_scr[rr]
        for h in range(ATTN_HEADS):
            for rr in range(dil):
                mix_scr[pl.ds(rr, ATTN_BLOCK, stride=dil), :] = o_scr[h, rr]
            o_ref[0, :, h * ATTN_HEAD_DIM:(h + 1) * ATTN_HEAD_DIM] = mix_scr[...]


def _attn_bias(window, dil):
    steps = window // dil
    assert steps <= ATTN_BLOCK
    qi = jnp.arange(ATTN_BLOCK)[:, None]
    kj = jnp.arange(ATTN_BLOCK)[None, :]
    slopes = 2.0 ** (-8.0 * (jnp.arange(ATTN_HEADS, dtype=F32) + 1.0) / ATTN_HEADS)

    def bias(step):
        valid = (step >= 0) & (step <= steps)
        dist = (step * dil).astype(F32)
        return jnp.where(valid[None], -(slopes[:, None, None] * dist[None]), MASK_NEG).astype(F32)

    return bias(qi - kj + ATTN_BLOCK), bias(qi - kj)


def _attn_group(qkv, gi, window, dil):
    b, _, ls, width = qkv.shape
    s = ls * dil
    hd = ATTN_HEADS * ATTN_HEAD_DIM
    assert ls % ATTN_BLOCK == 0 and width == 3 * hd
    nb = ls // ATTN_BLOCK
    bias_prev, bias_cur = _attn_bias(window, dil)

    def cur(c):
        return pl.BlockSpec((1, 1, ATTN_BLOCK, hd), lambda bi, n, r: (bi, r, n, c))

    def prev(c):
        return pl.BlockSpec((1, 1, ATTN_BLOCK, hd),
                            lambda bi, n, r: (bi, r, jnp.maximum(n - 1, 0), c))

    const = pl.BlockSpec((ATTN_HEADS, ATTN_BLOCK, ATTN_BLOCK), lambda bi, n, r: (0, 0, 0))
    o, lse = pl.pallas_call(
        functools.partial(_attn_kernel, dil=dil),
        grid=(b, nb, dil),
        in_specs=[cur(0), cur(1), prev(1), cur(2), prev(2), const, const],
        out_specs=[pl.BlockSpec((1, ATTN_BLOCK * dil, hd), lambda bi, n, r: (bi, n, 0)),
                   pl.BlockSpec((1, ATTN_BLOCK * dil, LANES), lambda bi, n, r: (bi, n, 0))],
        out_shape=[jax.ShapeDtypeStruct((b, s, hd), F32),
                   jax.ShapeDtypeStruct((b, s, LANES), F32)],
        scratch_shapes=[pltpu.VMEM((ATTN_HEADS, dil, ATTN_BLOCK, ATTN_HEAD_DIM), F32),
                        pltpu.VMEM((dil, ATTN_BLOCK, LANES), F32),
                        pltpu.VMEM((dil * ATTN_BLOCK, ATTN_HEAD_DIM), F32)],
        compiler_params=_cparams(),
        name=f"attn_g{gi}",
    )(qkv, qkv, qkv, qkv, qkv, bias_prev, bias_cur)
    return o.reshape(b * s, hd), lse.reshape(b * s, LANES)


def _attn_out_kernel(o1_ref, o2_ref, o3_ref, l1_ref, l2_ref, l3_ref, x_ref, w_ref, g_ref, b_ref,
                     o_ref, ob_ref):
    l1, l2, l3 = l1_ref[...], l2_ref[...], l3_ref[...]
    mx = jnp.maximum(jnp.maximum(l1, l2), l3)
    e1, e2, e3 = jnp.exp(l1 - mx), jnp.exp(l2 - mx), jnp.exp(l3 - mx)
    den = e1 + e2 + e3
    w1, w2, w3 = e1 / den, e2 / den, e3 / den
    cols = []
    for h in range(ATTN_HEADS):
        sl = slice(h * ATTN_HEAD_DIM, (h + 1) * ATTN_HEAD_DIM)
        oh = (w1[:, h:h + 1] * o1_ref[:, sl] + w2[:, h:h + 1] * o2_ref[:, sl]
              + w3[:, h:h + 1] * o3_ref[:, sl])
        cols.append(oh.astype(BF16))
    o = jnp.concatenate(cols, axis=1)
    y = _dot(o, w_ref[...]) + DEEPNORM_ALPHA * x_ref[...]
    _store_ln(y, g_ref, b_ref, o_ref, ob_ref)


def _row_spec(tm, width):
    return pl.BlockSpec((tm, width), lambda i: (i, 0))


def _const_spec(shape):
    return pl.BlockSpec(shape, lambda i: tuple(0 for _ in shape))


def _x_outs(t, tm):
    return dict(
        out_specs=[_row_spec(tm, D_MODEL), _row_spec(tm, D_MODEL)],
        out_shape=[jax.ShapeDtypeStruct((t, D_MODEL), F32), jax.ShapeDtypeStruct((t, D_MODEL), BF16)],
    )


def _attention_layer(x, xb, w_in, w_out, ln_g, ln_b, bsz, seq):
    t = x.shape[0]
    hd3 = 3 * ATTN_HEADS * ATTN_HEAD_DIM
    wb = w_in.astype(BF16)
    outs = []
    for gi, (window, dil) in enumerate(ATTN_GROUPS):
        qkv = _proj_perm(x, wb[:, gi * hd3:(gi + 1) * hd3], dil, bsz, seq, f"attn_proj_g{gi}")
        outs.append(_attn_group(qkv, gi, window, dil))
    tm = min(ROW_TILE, t)
    hd = ATTN_HEADS * ATTN_HEAD_DIM
    return pl.pallas_call(
        _attn_out_kernel,
        grid=(t // tm,),
        in_specs=[_row_spec(tm, hd)] * 3 + [_row_spec(tm, LANES)] * 3
                 + [_row_spec(tm, D_MODEL), _const_spec((hd, D_MODEL)),
                    _const_spec((1, D_MODEL)), _const_spec((1, D_MODEL))],
        compiler_params=_cparams(),
        name="attn_out",
        **_x_outs(t, tm),
    )(outs[0][0], outs[1][0], outs[2][0], outs[0][1], outs[1][1], outs[2][1],
      x, w_out.astype(BF16), ln_g[None], ln_b[None])


def _ret_kernel(q_ref, k_ref, v_ref, g_ref, dec_ref, xi_ref, zeta_ref, cd_ref, gg_ref, gb_ref,
                y_ref, state_ref):
    n = pl.program_id(1)

    @pl.when(n == 0)
    def _():
        state_ref[...] = jnp.zeros_like(state_ref)

    kscale = RET_KEY_DIM ** -0.5
    dk, dv = RET_KEY_DIM, RET_VAL_DIM
    hs = range(RET_HEADS)
    qs = [q_ref[0, :, h * dk:(h + 1) * dk] for h in hs]
    ks = [k_ref[0, :, h * dk:(h + 1) * dk] for h in hs]
    vs = [v_ref[0, :, h * dv:(h + 1) * dv] for h in hs]
    scores = [_dot_nt(qs[h], ks[h]) * kscale * dec_ref[h] for h in hs]
    inners = [_dot(scores[h].astype(BF16), vs[h]) for h in hs]
    sts = [state_ref[h] for h in hs]
    crosses = [_dot((qs[h].astype(F32) * xi_ref[h]).astype(BF16), sts[h].astype(BF16)) for h in hs]
    kzs = [(ks[h].astype(F32) * kscale * zeta_ref[h]).astype(BF16) for h in hs]
    for h in hs:
        state_ref[h] = cd_ref[h, 0:1, 0:1] * sts[h] + _dot_tn(kzs[h], vs[h])
    for h in hs:
        sl = slice(h * dv, (h + 1) * dv)
        y = inners[h] + crosses[h]
        mu = jnp.mean(y, axis=1, keepdims=True)
        yc = y - mu
        var = jnp.mean(yc * yc, axis=1, keepdims=True)
        yn = yc * lax.rsqrt(var + LN_EPS) * gg_ref[:, sl] + gb_ref[:, sl]
        g = g_ref[0, :, sl]
        y_ref[0, :, sl] = ((g * _sigmoid(g)) * yn).astype(y_ref.dtype)


def _ret_out_kernel(y_ref, x_ref, w_ref, g_ref, b_ref, o_ref, ob_ref):
    y = _dot(y_ref[...], w_ref[...]) + DEEPNORM_ALPHA * x_ref[...]
    _store_ln(y, g_ref, b_ref, o_ref, ob_ref)


def _retention_layer(x, xb, w_in, gn_g, gn_b, w_out, ln_g, ln_b, bsz, seq):
    t = x.shape[0]
    h, dk, dv, c = RET_HEADS, RET_KEY_DIM, RET_VAL_DIM, RET_CHUNK
    nqkv = 2 * h * dk + h * dv
    wb = w_in.astype(BF16)
    qkv = _proj(xb, wb[:, :nqkv], BF16, "ret_proj_qkv").reshape(bsz, seq, nqkv)
    gate = _proj(xb, wb[:, nqkv:], F32, "ret_proj_gate").reshape(bsz, seq, h * dv)
    log_gamma = jnp.log(1.0 - 2.0 ** (-5.0 - jnp.arange(h, dtype=F32)))
    pos = jnp.arange(c, dtype=F32)
    rel = pos[:, None] - pos[None, :]
    decay = jnp.where(rel >= 0, jnp.exp(log_gamma[:, None, None] * jnp.maximum(rel, 0.0)), 0.0)
    xi = jnp.exp(log_gamma[:, None] * (pos[None, :] + 1.0))
    zeta = jnp.exp(log_gamma[:, None] * (c - 1.0 - pos[None, :]))
    xi = jnp.broadcast_to(xi[:, :, None], (h, c, dk))
    zeta = jnp.broadcast_to(zeta[:, :, None], (h, c, dk))
    cd = jnp.broadcast_to(jnp.exp(log_gamma * c)[:, None, None], (h, SUBLANES, LANES))
    assert h * dv == 2 * h * dk
    full = lambda *shape: pl.BlockSpec(shape, lambda bi, n: tuple(0 for _ in shape))
    y = pl.pallas_call(
        _ret_kernel,
        grid=(bsz, seq // c),
        in_specs=[pl.BlockSpec((1, c, h * dk), lambda bi, n: (bi, n, 0)),
                  pl.BlockSpec((1, c, h * dk), lambda bi, n: (bi, n, 1)),
                  pl.BlockSpec((1, c, h * dv), lambda bi, n: (bi, n, 1)),
                  pl.BlockSpec((1, c, h * dv), lambda bi, n: (bi, n, 0)),
                  full(h, c, c), full(h, c, dk), full(h, c, dk), full(h, SUBLANES, LANES),
                  full(1, h * dv), full(1, h * dv)],
        out_specs=pl.BlockSpec((1, c, h * dv), lambda bi, n: (bi, n, 0)),
        out_shape=jax.ShapeDtypeStruct((bsz, seq, h * dv), BF16),
        scratch_shapes=[pltpu.VMEM((h, dk, dv), F32)],
        compiler_params=_cparams(),
        name="ret_core",
    )(qkv, qkv, qkv, gate, decay, xi, zeta, cd, gn_g[None], gn_b[None])
    tm = min(ROW_TILE, t)
    return pl.pallas_call(
        _ret_out_kernel,
        grid=(t // tm,),
        in_specs=[_row_spec(tm, h * dv), _row_spec(tm, D_MODEL), _const_spec((h * dv, D_MODEL)),
                  _const_spec((1, D_MODEL)), _const_spec((1, D_MODEL))],
        compiler_params=_cparams(),
        name="ret_out",
        **_x_outs(t, tm),
    )(y.reshape(t, h * dv), x, w_out.astype(BF16), ln_g[None], ln_b[None])


def _gelu_tanh(y):
    return 0.5 * y * (1.0 + jnp.tanh(math.sqrt(2.0 / math.pi) * (y + 0.044715 * (y * y * y))))


def _s5_kernel(x_ref, wb_ref, wc_ref, pr_ref, pi_ref, d_ref, y_ref, cr_ref, ci_ref):
    n = pl.program_id(2)

    @pl.when(n == 0)
    def _():
        cr_ref[...] = jnp.zeros_like(cr_ref)
        ci_ref[...] = jnp.zeros_like(ci_ref)

    nch = pr_ref.shape[2]
    slabs = SSM_TIME // SUBLANES
    u = x_ref[0]
    bu = _dot(u.astype(BF16), wb_ref[0])
    re = bu[:, :nch].reshape(slabs, SUBLANES, nch)
    im = bu[:, nch:].reshape(slabs, SUBLANES, nch)
    pr = pr_ref[0]
    pi = pi_ref[0]
    row = lax.broadcasted_iota(jnp.int32, (1, SUBLANES, nch), 1)
    for k in (1, 2, 4):
        ar = pr[k - 1:k, :][None]
        ai = pi[k - 1:k, :][None]
        sre = pltpu.roll(re, k, 1)
        sim = pltpu.roll(im, k, 1)
        keep = row >= k
        re = re + jnp.where(keep, ar * sre - ai * sim, 0.0)
        im = im + jnp.where(keep, ar * sim + ai * sre, 0.0)
    cr = cr_ref[...]
    ci = ci_ref[...]
    hre, him = [], []
    for j in range(slabs):
        hr = re[j] + (pr * cr - pi * ci)
        hi = im[j] + (pr * ci + pi * cr)
        hre.append(hr)
        him.append(hi)
        cr = jnp.broadcast_to(hr[SUBLANES - 1:SUBLANES, :], (SUBLANES, nch))
        ci = jnp.broadcast_to(hi[SUBLANES - 1:SUBLANES, :], (SUBLANES, nch))
    cr_ref[...] = cr
    ci_ref[...] = ci
    hcat = jnp.concatenate([jnp.concatenate(hre, axis=0), jnp.concatenate(him, axis=0)], axis=1)
    y = _dot(hcat.astype(BF16), wc_ref[0]) + d_ref[...] * u
    y_ref[0] = _gelu_tanh(y).astype(y_ref.dtype)


def _s5_out_kernel(y_ref, x_ref, w_ref, g_ref, b_ref, o_ref, ob_ref):
    vg = _dot(y_ref[...], w_ref[...])
    val = vg[:, :D_MODEL]
    gate = vg[:, D_MODEL:]
    y = val * _sigmoid(gate) + DEEPNORM_ALPHA * x_ref[...]
    _store_ln(y, g_ref, b_ref, o_ref, ob_ref)


def _block_diag(m):
    nb, g, r, c = m.shape
    eye = jnp.eye(g, dtype=m.dtype)
    return (m[:, :, :, None, :] * eye[None, :, None, :, None]).reshape(nb, g * r, g * c)


def _s5_layer(x, a_re, a_im, b_re, b_im, c_re, c_im, d_skip, log_dt, w_glu, ln_g, ln_b, bsz, seq):
    t = x.shape[0]
    gpb = SSM_COLS // SSM_GROUP
    ncb = D_MODEL // SSM_COLS
    nch = gpb * SSM_STATE
    dt = jnp.exp(log_dt)[:, None]
    mag = jnp.exp(dt * a_re)
    ab_re = mag * jnp.cos(dt * a_im)
    ab_im = mag * jnp.sin(dt * a_im)
    den = a_re * a_re + a_im * a_im
    f_re = ((ab_re - 1.0) * a_re + ab_im * a_im) / den
    f_im = (ab_im * a_re - (ab_re - 1.0) * a_im) / den
    bb_re = f_re[..., None] * b_re - f_im[..., None] * b_im
    bb_im = f_re[..., None] * b_im + f_im[..., None] * b_re
    tb = lambda m: jnp.swapaxes(m, 1, 2).reshape(ncb, gpb, SSM_GROUP, SSM_STATE)
    w_b = jnp.concatenate([_block_diag(tb(bb_re)), _block_diag(tb(bb_im))], axis=2).astype(BF16)
    tc = lambda m: jnp.swapaxes(m, 1, 2).reshape(ncb, gpb, SSM_STATE, SSM_GROUP)
    w_c = jnp.concatenate([_block_diag(tc(c_re)), -_block_diag(tc(c_im))], axis=1).astype(BF16)
    pr, pi = [ab_re], [ab_im]
    for _ in range(SUBLANES - 1):
        pr, pi = pr + [pr[-1] * ab_re - pi[-1] * ab_im], pi + [pr[-1] * ab_im + pi[-1] * ab_re]
    pw_re = jnp.stack(pr, axis=0).reshape(SUBLANES, ncb, nch).transpose(1, 0, 2)
    pw_im = jnp.stack(pi, axis=0).reshape(SUBLANES, ncb, nch).transpose(1, 0, 2)
    x3 = x.reshape(bsz, seq, D_MODEL)
    blk = lambda shape: pl.BlockSpec((1,) + shape, lambda bi, cb, n: (cb, 0, 0))
    y = pl.pallas_call(
        _s5_kernel,
        grid=(bsz, ncb, seq // SSM_TIME),
        in_specs=[pl.BlockSpec((1, SSM_TIME, SSM_COLS), lambda bi, cb, n: (bi, n, cb)),
                  blk((SSM_COLS, 2 * nch)), blk((2 * nch, SSM_COLS)),
                  blk((SUBLANES, nch)), blk((SUBLANES, nch)),
                  pl.BlockSpec((1, SSM_COLS), lambda bi, cb, n: (0, cb))],
        out_specs=pl.BlockSpec((1, SSM_TIME, SSM_COLS), lambda bi, cb, n: (bi, n, cb)),
        out_shape=jax.ShapeDtypeStruct((bsz, seq, D_MODEL), BF16),
        scratch_shapes=[pltpu.VMEM((SUBLANES, nch), F32), pltpu.VMEM((SUBLANES, nch), F32)],
        compiler_params=_cparams(),
        name="s5_core",
    )(x3, w_b, w_c, pw_re, pw_im, d_skip.reshape(1, D_MODEL))
    tm = min(ROW_TILE, t)
    return pl.pallas_call(
        _s5_out_kernel,
        grid=(t // tm,),
        in_specs=[_row_spec(tm, D_MODEL), _row_spec(tm, D_MODEL), _const_spec((D_MODEL, 2 * D_MODEL)),
                  _const_spec((1, D_MODEL)), _const_spec((1, D_MODEL))],
        compiler_params=_cparams(),
        name="s5_out",
        **_x_outs(t, tm),
    )(y.reshape(t, D_MODEL), x, w_glu.astype(BF16), ln_g[None], ln_b[None])


def _cast_kernel(*refs):
    o_ref = refs[-1]
    o_ref[...] = jnp.concatenate([r[0].astype(o_ref.dtype) for r in refs[:-1]], axis=-1)


def _cast_bf16(*ws, layer, name):
    _, e, r, c = ws[0].shape
    return pl.pallas_call(
        _cast_kernel,
        grid=(e,),
        in_specs=[pl.BlockSpec((1, 1, r, c), lambda i: (layer, i, 0, 0)) for _ in ws],
        out_specs=pl.BlockSpec((1, r, c * len(ws)), lambda i: (i, 0, 0)),
        out_shape=jax.ShapeDtypeStruct((e, r, c * len(ws)), BF16),
        compiler_params=_cparams(),
        name=name,
    )(*ws)


def _first_index(eq, iota, size, axis):
    return jnp.min(jnp.where(eq, iota, size), axis=axis, keepdims=True)


def _router_kernel(x_ref, w_ref, b_ref, tri_ref, rank_ref, gate_ref, cnt_ref):
    x = x_ref[...]
    w = w_ref[...]
    x_hi = x.astype(BF16)
    x_lo = (x - x_hi.astype(F32)).astype(BF16)
    w_hi = w.astype(BF16)
    w_lo = (w - w_hi.astype(F32)).astype(BF16)
    logits = _dot_nt(w_hi, x_hi) + (_dot_nt(w_hi, x_lo) + _dot_nt(w_lo, x_hi))
    scores = _sigmoid(logits)
    sel = scores + b_ref[...]
    tt = sel.shape[1]
    gsz = N_EXPERTS // N_EXPERT_GROUPS
    grp = sel.reshape(N_EXPERT_GROUPS, gsz, tt)
    wi = lax.broadcasted_iota(jnp.int32, grp.shape, 1)
    m1 = jnp.max(grp, axis=1, keepdims=True)
    i1 = _first_index(grp == m1, wi, gsz, 1)
    m2 = jnp.max(jnp.where(wi == i1, -jnp.inf, grp), axis=1, keepdims=True)
    gs = (m1 + m2).reshape(N_EXPERT_GROUPS, tt)
    gi = lax.broadcasted_iota(jnp.int32, gs.shape, 0)
    gmask = jnp.zeros(gs.shape, jnp.bool_)
    for _ in range(TOPK_GROUPS):
        gm = jnp.max(gs, axis=0, keepdims=True)
        pick = gi == _first_index(gs == gm, gi, N_EXPERT_GROUPS, 0)
        gmask = gmask | pick
        gs = jnp.where(pick, -jnp.inf, gs)
    emask = jnp.broadcast_to(gmask[:, None, :], grp.shape).reshape(N_EXPERTS, tt)
    cand = jnp.where(emask, sel, -jnp.inf)
    ei = lax.broadcasted_iota(jnp.int32, cand.shape, 0)
    chosen = jnp.zeros(cand.shape, jnp.bool_)
    for _ in range(TOP_K):
        cm = jnp.max(cand, axis=0, keepdims=True)
        pick = ei == _first_index(cand == cm, ei, N_EXPERTS, 0)
        chosen = chosen | pick
        cand = jnp.where(pick, -jnp.inf, cand)
    top_w = jnp.where(chosen, scores, 0.0)
    denom = jnp.sum(top_w, axis=0, keepdims=True)
    gate_ref[...] = top_w / denom * ROUTED_SCALE
    chosen_f = chosen.astype(F32)
    rank = _dot(chosen_f.astype(BF16), tri_ref[...])
    rank_ref[...] = jnp.where(chosen, rank, -1.0).astype(jnp.int32)
    cnt = jnp.sum(chosen_f, axis=1, keepdims=True)
    cnt_ref[0] = jnp.broadcast_to(cnt, (N_EXPERTS, LANES)).astype(jnp.int32)


def _moe_kernel(cnt_ref, x_ref, rank_ref, gate_ref, wgu_ref, wd_ref, o_ref, oh_scr, yg_scr):
    j = pl.program_id(0)
    s = pl.program_id(1)

    @pl.when(s == 0)
    def _():
        o_ref[...] = jnp.zeros_like(o_ref)

    tt = x_ref.shape[0]
    slot = lax.broadcasted_iota(jnp.int32, (MOE_SLOTS, tt), 0)

    def expert_chunks(us, c):
        hits = [(rank_ref[u] - c * MOE_SLOTS) == slot for u in us]
        onehots = [hit.astype(F32).astype(BF16) for hit in hits]
        xcs = [_dot(oh, x_ref[...]).astype(BF16) for oh in onehots]
        hgus = [_dot(xc, wgu_ref[u]) for xc, u in zip(xcs, us)]
        hbs = [((h[:, :EXPERT_DIM] * _sigmoid(h[:, :EXPERT_DIM])) * h[:, EXPERT_DIM:]).astype(BF16)
               for h in hgus]
        ys = [_dot(hb, wd_ref[u]) for hb, u in zip(hbs, us)]
        gsels = [jnp.sum(jnp.where(hit, gate_ref[u], 0.0), axis=1, keepdims=True)
                 for hit, u in zip(hits, us)]
        return onehots, [(y * g).astype(BF16) for y, g in zip(ys, gsels)]

    steps_per_group = MOE_GROUP // MOE_UNROLL
    gs = s % steps_per_group
    us = list(range(MOE_UNROLL))
    onehots, ygs = expert_chunks(us, 0)
    for u in us:
        row0 = pl.multiple_of((gs * MOE_UNROLL + u) * MOE_SLOTS, MOE_SLOTS)
        oh_scr[pl.ds(row0, MOE_SLOTS), :] = onehots[u]
        yg_scr[pl.ds(row0, MOE_SLOTS), :] = ygs[u]

    for u in us:
        cnt = cnt_ref[j * N_EXPERTS + s * MOE_UNROLL + u]
        n_chunks = (cnt + MOE_SLOTS - 1) // MOE_SLOTS

        def overflow(c, carry, u=u):
            onehot_c, yg_c = expert_chunks([u], c)
            o_ref[...] += _dot_tn(onehot_c[0], yg_c[0])
            return carry

        lax.fori_loop(1, n_chunks, overflow, 0)

    @pl.when(gs == steps_per_group - 1)
    def _():
        o_ref[...] += _dot_tn(oh_scr[...], yg_scr[...])


def _moe_out_kernel(x_ref, xb_ref, acc_ref, wgu_ref, wd_ref, g_ref, b_ref, o_ref, ob_ref):
    hgu = _dot(xb_ref[...], wgu_ref[...])
    hg = hgu[:, :SHARED_DIM]
    hb = ((hg * _sigmoid(hg)) * hgu[:, SHARED_DIM:]).astype(BF16)
    shared = _dot(hb, wd_ref[...])
    y = DEEPNORM_ALPHA * x_ref[...] + (acc_ref[...] + shared)
    _store_ln(y, g_ref, b_ref, o_ref, ob_ref)


def _moe_layer(x, xb, w_router, router_bias, w_gate, w_up, w_down, sw_gate, sw_up, sw_down,
               ln_g, ln_b, layer):
    t = x.shape[0]
    tt = min(MOE_TILE, t)
    nj = t // tt
    tri = jnp.triu(jnp.ones((tt, tt), F32), 1).astype(BF16)
    rank, gate, cnt = pl.pallas_call(
        _router_kernel,
        grid=(nj,),
        in_specs=[_row_spec(tt, D_MODEL), _const_spec((N_EXPERTS, D_MODEL)),
                  _const_spec((N_EXPERTS, 1)), _const_spec((tt, tt))],
        out_specs=[pl.BlockSpec((N_EXPERTS, tt), lambda i: (0, i)),
                   pl.BlockSpec((N_EXPERTS, tt), lambda i: (0, i)),
                   pl.BlockSpec((1, N_EXPERTS, LANES), lambda i: (i, 0, 0))],
        out_shape=[jax.ShapeDtypeStruct((N_EXPERTS, t), jnp.int32),
                   jax.ShapeDtypeStruct((N_EXPERTS, t), F32),
                   jax.ShapeDtypeStruct((nj, N_EXPERTS, LANES), jnp.int32)],
        compiler_params=_cparams(),
        name="moe_router",
    )(x, w_router.T, router_bias[:, None], tri)
    wgu = _cast_bf16(w_gate, w_up, layer=layer, name="moe_cast_gu")
    wd = _cast_bf16(w_down, layer=layer, name="moe_cast_d")
    acc = pl.pallas_call(
        _moe_kernel,
        grid_spec=pltpu.PrefetchScalarGridSpec(
            num_scalar_prefetch=1,
            grid=(nj, N_EXPERTS // MOE_UNROLL),
            in_specs=[pl.BlockSpec((tt, D_MODEL), lambda j, e, c: (j, 0)),
                      pl.BlockSpec((MOE_UNROLL, 1, tt), lambda j, e, c: (e, 0, j)),
                      pl.BlockSpec((MOE_UNROLL, 1, tt), lambda j, e, c: (e, 0, j)),
                      pl.BlockSpec((MOE_UNROLL, D_MODEL, 2 * EXPERT_DIM), lambda j, e, c: (e, 0, 0)),
                      pl.BlockSpec((MOE_UNROLL, EXPERT_DIM, D_MODEL), lambda j, e, c: (e, 0, 0))],
            out_specs=pl.BlockSpec((tt, D_MODEL), lambda j, e, c: (j, 0)),
            scratch_shapes=[pltpu.VMEM((MOE_GROUP * MOE_SLOTS, tt), BF16),
                            pltpu.VMEM((MOE_GROUP * MOE_SLOTS, D_MODEL), BF16)],
        ),
        out_shape=jax.ShapeDtypeStruct((t, D_MODEL), F32),
        compiler_params=_cparams(),
        name="moe_experts",
    )(cnt[:, :, 0].reshape(-1), xb, rank.reshape(N_EXPERTS, 1, t), gate.reshape(N_EXPERTS, 1, t),
      wgu, wd)
    tm = min(ROW_TILE, t)
    sgu = jnp.concatenate([sw_gate, sw_up], axis=1).astype(BF16)
    return pl.pallas_call(
        _moe_out_kernel,
        grid=(t // tm,),
        in_specs=[_row_spec(tm, D_MODEL), _row_spec(tm, D_MODEL), _row_spec(tm, D_MODEL),
                  _const_spec((D_MODEL, 2 * SHARED_DIM)), _const_spec((SHARED_DIM, D_MODEL)),
                  _const_spec((1, D_MODEL)), _const_spec((1, D_MODEL))],
        compiler_params=_cparams(),
        name="moe_out",
        **_x_outs(t, tm),
    )(x, xb, acc, sgu, sw_down.astype(BF16), ln_g[None], ln_b[None])


def kernel(x, attn_w_in, attn_w_out, ret_w_in, ret_gn_g, ret_gn_b, ret_w_out, ssm_a_re, ssm_a_im, ssm_b_re, ssm_b_im, ssm_c_re, ssm_c_im, ssm_d, ssm_log_dt, ssm_w_glu, moe_w_router, moe_router_bias, moe_w_gate, moe_w_up, moe_w_down, shared_w_gate, shared_w_up, shared_w_down, ln_g, ln_b):
    bsz, seq, d = x.shape
    depth = ln_g.shape[0]
    xf = x.reshape(bsz * seq, d)
    xb = xf.astype(BF16)
    for i in range(depth):
        j = i // N_MIXERS
        kind = i % N_MIXERS
        if kind == 0:
            xf, xb = _attention_layer(xf, xb, attn_w_in[j], attn_w_out[j], ln_g[i, 0], ln_b[i, 0],
                                      bsz, seq)
        elif kind == 1:
            xf, xb = _retention_layer(xf, xb, ret_w_in[j], ret_gn_g[j], ret_gn_b[j], ret_w_out[j],
                                      ln_g[i, 0], ln_b[i, 0], bsz, seq)
        else:
            xf, xb = _s5_layer(xf, ssm_a_re[j], ssm_a_im[j], ssm_b_re[j], ssm_b_im[j], ssm_c_re[j],
                               ssm_c_im[j], ssm_d[j], ssm_log_dt[j], ssm_w_glu[j],
                               ln_g[i, 0], ln_b[i, 0], bsz, seq)
        xf, xb = _moe_layer(xf, xb, moe_w_router[i], moe_router_bias[i], moe_w_gate, moe_w_up,
                            moe_w_down, shared_w_gate[i], shared_w_up[i], shared_w_down[i],
                            ln_g[i, 1], ln_b[i, 1], i)
    return xf.reshape(bsz, seq, d)
```

```python
import functools
import math

import jax
import jax.numpy as jnp
from jax import lax
from jax.experimental import pallas as pl
from jax.experimental.pallas import tpu as pltpu

F32 = jnp.float32
BF16 = jnp.bfloat16

D_MODEL = 1024
DEPTH = 4
N_MIXERS = 3
ATTN_GROUPS = ((128, 1), (512, 4), (2048, 16))
ATTN_HEADS = 8
ATTN_HEAD_DIM = D_MODEL // ATTN_HEADS
ATTN_BLOCK = 128
RET_HEADS = 4
RET_KEY_DIM = D_MODEL // RET_HEADS
RET_VAL_DIM = 2 * RET_KEY_DIM
RET_CHUNK = 128
SSM_GROUP = 16
SSM_GROUPS = D_MODEL // SSM_GROUP
SSM_STATE = 64
N_EXPERTS = 64
TOP_K = 8
N_EXPERT_GROUPS = 8
TOPK_GROUPS = 4
EXPERT_DIM = 256
SHARED_DIM = 256
ROUTED_SCALE = 2.5
DEEPNORM_ALPHA = (2 * DEPTH) ** 0.25
LN_EPS = 1e-5

LANES = 128
SUBLANES = 8
MASK_NEG = -1e30
VMEM_LIMIT = 56 * 1024 * 1024

ROW_TILE = 512
PROJ_TM = 1024
PROJ_TN = 1024
MOE_TILE = 1024
MOE_SLOTS = 160
MOE_GROUP = 16
MOE_UNROLL = 4
SSM_TIME = 512
SSM_COLS = 128


def _cparams():
    return pltpu.CompilerParams(vmem_limit_bytes=VMEM_LIMIT)


def _dot(a, b):
    return jnp.dot(a, b, preferred_element_type=F32)


def _dot_nt(a, b):
    return lax.dot_general(a, b, (((1,), (1,)), ((), ())), preferred_element_type=F32)


def _dot_tn(a, b):
    return lax.dot_general(a, b, (((0,), (0,)), ((), ())), preferred_element_type=F32)


def _sigmoid(x):
    return 1.0 / (1.0 + jnp.exp(-x))


def _layer_norm(y, g, b):
    mu = jnp.mean(y, axis=-1, keepdims=True)
    yc = y - mu
    var = jnp.mean(yc * yc, axis=-1, keepdims=True)
    return yc * lax.rsqrt(var + LN_EPS) * g + b


def _store_ln(y, g_ref, b_ref, o_ref, ob_ref):
    out = _layer_norm(y, g_ref[...], b_ref[...])
    o_ref[...] = out
    ob_ref[...] = out.astype(BF16)


def _proj_kernel(x_ref, w_ref, o_ref):
    o_ref[...] = _dot(x_ref[...], w_ref[...]).astype(o_ref.dtype)


def _proj(xb, w, out_dtype, name):
    m, k = xb.shape
    n = w.shape[1]
    tm = min(PROJ_TM, m)
    tn = min(PROJ_TN, n)
    return pl.pallas_call(
        _proj_kernel,
        grid=(m // tm, n // tn),
        in_specs=[pl.BlockSpec((tm, k), lambda i, j: (i, 0)),
                  pl.BlockSpec((k, tn), lambda i, j: (0, j))],
        out_specs=pl.BlockSpec((tm, tn), lambda i, j: (i, j)),
        out_shape=jax.ShapeDtypeStruct((m, n), out_dtype),
        compiler_params=_cparams(),
        name=name,
    )(xb, w)


def _proj_perm_kernel(x_ref, w_ref, o_ref, xs_ref, col_ref, *, dil):
    rows = x_ref.shape[0] // dil

    @pl.when(pl.program_id(1) == 0)
    def _():
        for c in range(x_ref.shape[1] // LANES):
            cs = slice(c * LANES, (c + 1) * LANES)
            col_ref[...] = x_ref[:, cs]
            for r in range(dil):
                xs_ref[r * rows:(r + 1) * rows, cs] = col_ref[pl.ds(r, rows, stride=dil), :].astype(BF16)

    res = _dot(xs_ref[...], w_ref[...])
    o_ref[0] = res.reshape(dil, rows, res.shape[1]).astype(o_ref.dtype)


def _proj_perm(xf, w, dil, bsz, seq, name):
    k = xf.shape[1]
    n = w.shape[1]
    tm = min(PROJ_TM, seq)
    tn = min(PROJ_TN, n)
    assert seq % tm == 0 and tm % (dil * 2 * SUBLANES) == 0
    per_b = seq // tm
    rows = tm // dil
    return pl.pallas_call(
        functools.partial(_proj_perm_kernel, dil=dil),
        grid=(bsz * per_b, n // tn),
        in_specs=[pl.BlockSpec((tm, k), lambda i, j: (i, 0)),
                  pl.BlockSpec((k, tn), lambda i, j: (0, j))],
        out_specs=pl.BlockSpec((1, dil, rows, tn), lambda i, j: (i // per_b, 0, i % per_b, j)),
        out_shape=jax.ShapeDtypeStruct((bsz, dil, seq // dil, n), BF16),
        scratch_shapes=[pltpu.VMEM((tm, k), BF16), pltpu.VMEM((tm, LANES), F32)],
        compiler_params=_cparams(),
        name=name,
    )(xf, w)


def _attn_kernel(q_ref, kc_ref, kp_ref, vc_ref, vp_ref, bp_ref, bc_ref, o_ref, lse_ref,
                 o_scr, lse_scr, mix_scr, *, dil):
    n = pl.program_id(1)
    r = pl.program_id(2)
    first = jnp.where(n == 0, MASK_NEG, 0.0).astype(F32)
    scale = ATTN_HEAD_DIM ** -0.5
    lane = lax.broadcasted_iota(jnp.int32, (ATTN_BLOCK, LANES), 1)
    lse_tile = jnp.zeros((ATTN_BLOCK, LANES), F32)
    heads = range(ATTN_HEADS)
    sls = [slice(h * ATTN_HEAD_DIM, (h + 1) * ATTN_HEAD_DIM) for h in heads]
    sps = [_dot_nt(q_ref[0, 0, :, sls[h]], kp_ref[0, 0, :, sls[h]]) * scale + (bp_ref[h] + first)
           for h in heads]
    scs = [_dot_nt(q_ref[0, 0, :, sls[h]], kc_ref[0, 0, :, sls[h]]) * scale + bc_ref[h]
           for h in heads]
    ms = [jnp.max(jnp.maximum(sps[h], scs[h]), axis=1, keepdims=True) for h in heads]
    pps = [jnp.exp(sps[h] - ms[h]) for h in heads]
    pcs = [jnp.exp(scs[h] - ms[h]) for h in heads]
    ls = [jnp.sum(pps[h] + pcs[h], axis=1, keepdims=True) for h in heads]
    for h in heads:
        o = (_dot(pps[h].astype(BF16), vp_ref[0, 0, :, sls[h]])
             + _dot(pcs[h].astype(BF16), vc_ref[0, 0, :, sls[h]]))
        o_scr[h, r] = o / ls[h]
        lse_tile = jnp.where(lane == h, ms[h] + jnp.log(ls[h]), lse_tile)
    lse_scr[r] = lse_tile

    @pl.when(r == dil - 1)
    def _():
        for rr in range(dil):
            lse_ref[0, pl.ds(rr, ATTN_BLOCK, stride=dil), :] = lse_scr[rr]
        for h in range(ATTN_HEADS):
            for rr in range(dil):
                mix_scr[pl.ds(rr, ATTN_BLOCK, stride=dil), :] = o_scr[h, rr]
            o_ref[0, :, h * ATTN_HEAD_DIM:(h + 1) * ATTN_HEAD_DIM] = mix_scr[...]


def _attn_bias(window, dil):
    steps = window // dil
    assert steps <= ATTN_BLOCK
    qi = jnp.arange(ATTN_BLOCK)[:, None]
    kj = jnp.arange(ATTN_BLOCK)[None, :]
    slopes = 2.0 ** (-8.0 * (jnp.arange(ATTN_HEADS, dtype=F32) + 1.0) / ATTN_HEADS)

    def bias(step):
        valid = (step >= 0) & (step <= steps)
        dist = (step * dil).astype(F32)
        return jnp.where(valid[None], -(slopes[:, None, None] * dist[None]), MASK_NEG).astype(F32)

    return bias(qi - kj + ATTN_BLOCK), bias(qi - kj)


def _attn_group(qkv, gi, window, dil):
    b, _, ls, width = qkv.shape
    s = ls * dil
    hd = ATTN_HEADS * ATTN_HEAD_DIM
    assert ls % ATTN_BLOCK == 0 and width == 3 * hd
    nb = ls // ATTN_BLOCK
    bias_prev, bias_cur = _attn_bias(window, dil)

    def cur(c):
        return pl.BlockSpec((1, 1, ATTN_BLOCK, hd), lambda bi, n, r: (bi, r, n, c))

    def prev(c):
        return pl.BlockSpec((1, 1, ATTN_BLOCK, hd),
                            lambda bi, n, r: (bi, r, jnp.maximum(n - 1, 0), c))

    const = pl.BlockSpec((ATTN_HEADS, ATTN_BLOCK, ATTN_BLOCK), lambda bi, n, r: (0, 0, 0))
    o, lse = pl.pallas_call(
        functools.partial(_attn_kernel, dil=dil),
        grid=(b, nb, dil),
        in_specs=[cur(0), cur(1), prev(1), cur(2), prev(2), const, const],
        out_specs=[pl.BlockSpec((1, ATTN_BLOCK * dil, hd), lambda bi, n, r: (bi, n, 0)),
                   pl.BlockSpec((1, ATTN_BLOCK * dil, LANES), lambda bi, n, r: (bi, n, 0))],
        out_shape=[jax.ShapeDtypeStruct((b, s, hd), F32),
                   jax.ShapeDtypeStruct((b, s, LANES), F32)],
        scratch_shapes=[pltpu.VMEM((ATTN_HEADS, dil, ATTN_BLOCK, ATTN_HEAD_DIM), F32),
                        pltpu.VMEM((dil, ATTN_BLOCK, LANES), F32),
                        pltpu.VMEM((dil * ATTN_BLOCK, ATTN_HEAD_DIM), F32)],
        compiler_params=_cparams(),
        name=f"attn_g{gi}",
    )(qkv, qkv, qkv, qkv, qkv, bias_prev, bias_cur)
    return o.reshape(b * s, hd), lse.reshape(b * s, LANES)


def _attn_out_kernel(o1_ref, o2_ref, o3_ref, l1_ref, l2_ref, l3_ref, x_ref, w_ref, g_ref, b_ref,
                     o_ref, ob_ref):
    l1, l2, l3 = l1_ref[...], l2_ref[...], l3_ref[...]
    mx = jnp.maximum(jnp.maximum(l1, l2), l3)
    e1, e2, e3 = jnp.exp(l1 - mx), jnp.exp(l2 - mx), jnp.exp(l3 - mx)
    den = e1 + e2 + e3
    w1, w2, w3 = e1 / den, e2 / den, e3 / den
    cols = []
    for h in range(ATTN_HEADS):
        sl = slice(h * ATTN_HEAD_DIM, (h + 1) * ATTN_HEAD_DIM)
        oh = (w1[:, h:h + 1] * o1_ref[:, sl] + w2[:, h:h + 1] * o2_ref[:, sl]
              + w3[:, h:h + 1] * o3_ref[:, sl])
        cols.append(oh.astype(BF16))
    o = jnp.concatenate(cols, axis=1)
    y = _dot(o, w_ref[...]) + DEEPNORM_ALPHA * x_ref[...]
    _store_ln(y, g_ref, b_ref, o_ref, ob_ref)


def _row_spec(tm, width):
    return pl.BlockSpec((tm, width), lambda i: (i, 0))


def _const_spec(shape):
    return pl.BlockSpec(shape, lambda i: tuple(0 for _ in shape))


def _x_outs(t, tm):
    return dict(
        out_specs=[_row_spec(tm, D_MODEL), _row_spec(tm, D_MODEL)],
        out_shape=[jax.ShapeDtypeStruct((t, D_MODEL), F32), jax.ShapeDtypeStruct((t, D_MODEL), BF16)],
    )


def _attention_layer(x, xb, w_in, w_out, ln_g, ln_b, bsz, seq):
    t = x.shape[0]
    hd3 = 3 * ATTN_HEADS * ATTN_HEAD_DIM
    wb = w_in.astype(BF16)
    outs = []
    for gi, (window, dil) in enumerate(ATTN_GROUPS):
        qkv = _proj_perm(x, wb[:, gi * hd3:(gi + 1) * hd3], dil, bsz, seq, f"attn_proj_g{gi}")
        outs.append(_attn_group(qkv, gi, window, dil))
    tm = min(ROW_TILE, t)
    hd = ATTN_HEADS * ATTN_HEAD_DIM
    return pl.pallas_call(
        _attn_out_kernel,
        grid=(t // tm,),
        in_specs=[_row_spec(tm, hd)] * 3 + [_row_spec(tm, LANES)] * 3
                 + [_row_spec(tm, D_MODEL), _const_spec((hd, D_MODEL)),
                    _const_spec((1, D_MODEL)), _const_spec((1, D_MODEL))],
        compiler_params=_cparams(),
        name="attn_out",
        **_x_outs(t, tm),
    )(outs[0][0], outs[1][0], outs[2][0], outs[0][1], outs[1][1], outs[2][1],
      x, w_out.astype(BF16), ln_g[None], ln_b[None])


def _ret_kernel(q_ref, k_ref, v_ref, g_ref, dec_ref, xi_ref, zeta_ref, cd_ref, gg_ref, gb_ref,
                y_ref, state_ref):
    n = pl.program_id(1)

    @pl.when(n == 0)
    def _():
        state_ref[...] = jnp.zeros_like(state_ref)

    kscale = RET_KEY_DIM ** -0.5
    dk, dv = RET_KEY_DIM, RET_VAL_DIM
    hs = range(RET_HEADS)
    qs = [q_ref[0, :, h * dk:(h + 1) * dk] for h in hs]
    ks = [k_ref[0, :, h * dk:(h + 1) * dk] for h in hs]
    vs = [v_ref[0, :, h * dv:(h + 1) * dv] for h in hs]
    scores = [_dot_nt(qs[h], ks[h]) * kscale * dec_ref[h] for h in hs]
    inners = [_dot(scores[h].astype(BF16), vs[h]) for h in hs]
    sts = [state_ref[h] for h in hs]
    crosses = [_dot((qs[h].astype(F32) * xi_ref[h]).astype(BF16), sts[h].astype(BF16)) for h in hs]
    kzs = [(ks[h].astype(F32) * kscale * zeta_ref[h]).astype(BF16) for h in hs]
    for h in hs:
        state_ref[h] = cd_ref[h, 0:1, 0:1] * sts[h] + _dot_tn(kzs[h], vs[h])
    for h in hs:
        sl = slice(h * dv, (h + 1) * dv)
        y = inners[h] + crosses[h]
        mu = jnp.mean(y, axis=1, keepdims=True)
        yc = y - mu
        var = jnp.mean(yc * yc, axis=1, keepdims=True)
        yn = yc * lax.rsqrt(var + LN_EPS) * gg_ref[:, sl] + gb_ref[:, sl]
        g = g_ref[0, :, sl]
        y_ref[0, :, sl] = ((g * _sigmoid(g)) * yn).astype(y_ref.dtype)


def _ret_out_kernel(y_ref, x_ref, w_ref, g_ref, b_ref, o_ref, ob_ref):
    y = _dot(y_ref[...], w_ref[...]) + DEEPNORM_ALPHA * x_ref[...]
    _store_ln(y, g_ref, b_ref, o_ref, ob_ref)


def _retention_layer(x, xb, w_in, gn_g, gn_b, w_out, ln_g, ln_b, bsz, seq):
    t = x.shape[0]
    h, dk, dv, c = RET_HEADS, RET_KEY_DIM, RET_VAL_DIM, RET_CHUNK
    nqkv = 2 * h * dk + h * dv
    wb = w_in.astype(BF16)
    qkv = _proj(xb, wb[:, :nqkv], BF16, "ret_proj_qkv").reshape(bsz, seq, nqkv)
    gate = _proj(xb, wb[:, nqkv:], F32, "ret_proj_gate").reshape(bsz, seq, h * dv)
    log_gamma = jnp.log(1.0 - 2.0 ** (-5.0 - jnp.arange(h, dtype=F32)))
    pos = jnp.arange(c, dtype=F32)
    rel = pos[:, None] - pos[None, :]
    decay = jnp.where(rel >= 0, jnp.exp(log_gamma[:, None, None] * jnp.maximum(rel, 0.0)), 0.0)
    xi = jnp.exp(log_gamma[:, None] * (pos[None, :] + 1.0))
    zeta = jnp.exp(log_gamma[:, None] * (c - 1.0 - pos[None, :]))
    xi = jnp.broadcast_to(xi[:, :, None], (h, c, dk))
    zeta = jnp.broadcast_to(zeta[:, :, None], (h, c, dk))
    cd = jnp.broadcast_to(jnp.exp(log_gamma * c)[:, None, None], (h, SUBLANES, LANES))
    assert h * dv == 2 * h * dk
    full = lambda *shape: pl.BlockSpec(shape, lambda bi, n: tuple(0 for _ in shape))
    y = pl.pallas_call(
        _ret_kernel,
        grid=(bsz, seq // c),
        in_specs=[pl.BlockSpec((1, c, h * dk), lambda bi, n: (bi, n, 0)),
                  pl.BlockSpec((1, c, h * dk), lambda bi, n: (bi, n, 1)),
                  pl.BlockSpec((1, c, h * dv), lambda bi, n: (bi, n, 1)),
                  pl.BlockSpec((1, c, h * dv), lambda bi, n: (bi, n, 0)),
                  full(h, c, c), full(h, c, dk), full(h, c, dk), full(h, SUBLANES, LANES),
                  full(1, h * dv), full(1, h * dv)],
        out_specs=pl.BlockSpec((1, c, h * dv), lambda bi, n: (bi, n, 0)),
        out_shape=jax.ShapeDtypeStruct((bsz, seq, h * dv), BF16),
        scratch_shapes=[pltpu.VMEM((h, dk, dv), F32)],
        compiler_params=_cparams(),
        name="ret_core",
    )(qkv, qkv, qkv, gate, decay, xi, zeta, cd, gn_g[None], gn_b[None])
    tm = min(ROW_TILE, t)
    return pl.pallas_call(
        _ret_out_kernel,
        grid=(t // tm,),
        in_specs=[_row_spec(tm, h * dv), _row_spec(tm, D_MODEL), _const_spec((h * dv, D_MODEL)),
                  _const_spec((1, D_MODEL)), _const_spec((1, D_MODEL))],
        compiler_params=_cparams(),
        name="ret_out",
        **_x_outs(t, tm),
    )(y.reshape(t, h * dv), x, w_out.astype(BF16), ln_g[None], ln_b[None])


def _gelu_tanh(y):
    return 0.5 * y * (1.0 + jnp.tanh(math.sqrt(2.0 / math.pi) * (y + 0.044715 * (y * y * y))))


def _s5_kernel(x_ref, wb_ref, wc_ref, pr_ref, pi_ref, d_ref, y_ref,
               cr_ref, ci_ref, xs_ref, hr_scr, hi_scr, yt_ref):
    n = pl.program_id(2)

    @pl.when(n == 0)
    def _():
        cr_ref[...] = jnp.zeros_like(cr_ref)
        ci_ref[...] = jnp.zeros_like(ci_ref)

    nch = pr_ref.shape[2]
    seg = SUBLANES
    tau_n = SSM_TIME // seg
    for s in range(seg):
        xs_ref[pl.ds(s, tau_n, stride=seg), :] = x_ref[0, s * tau_n:(s + 1) * tau_n, :]
    u = xs_ref[...]
    bu = _dot(u.astype(BF16), wb_ref[0])
    re = bu[:, :nch].reshape(tau_n, seg, nch)
    im = bu[:, nch:].reshape(tau_n, seg, nch)
    a_r = jnp.broadcast_to(pr_ref[0, 0:1, :], (seg, nch))
    a_i = jnp.broadcast_to(pi_ref[0, 0:1, :], (seg, nch))
    hr = jnp.zeros((seg, nch), F32)
    hi = jnp.zeros((seg, nch), F32)
    for t in range(tau_n):
        hr, hi = a_r * hr - a_i * hi + re[t], a_r * hi + a_i * hr + im[t]
        hr_scr[t] = hr
        hi_scr[t] = hi
    q_r = pr_ref[0, tau_n - 1:tau_n, :]
    q_i = pi_ref[0, tau_n - 1:tau_n, :]
    c_r = cr_ref[0:1, :]
    c_i = ci_ref[0:1, :]
    row = lax.broadcasted_iota(jnp.int32, (seg, nch), 0)
    in_r = jnp.zeros((seg, nch), F32)
    in_i = jnp.zeros((seg, nch), F32)
    for s in range(seg):
        in_r = jnp.where(row == s, c_r, in_r)
        in_i = jnp.where(row == s, c_i, in_i)
        c_r, c_i = (hr[s:s + 1, :] + (q_r * c_r - q_i * c_i),
                    hi[s:s + 1, :] + (q_r * c_i + q_i * c_r))
    cr_ref[...] = jnp.broadcast_to(c_r, cr_ref.shape)
    ci_ref[...] = jnp.broadcast_to(c_i, ci_ref.shape)
    hre, him = [], []
    for t in range(tau_n):
        p_r = pr_ref[0, t:t + 1, :]
        p_i = pi_ref[0, t:t + 1, :]
        hre.append(hr_scr[t] + (p_r * in_r - p_i * in_i))
        him.append(hi_scr[t] + (p_r * in_i + p_i * in_r))
    hcat = jnp.concatenate([jnp.concatenate(hre, axis=0), jnp.concatenate(him, axis=0)], axis=1)
    y = _dot(hcat.astype(BF16), wc_ref[0]) + d_ref[...] * u
    yt_ref[...] = _gelu_tanh(y)
    for s in range(seg):
        y_ref[0, s * tau_n:(s + 1) * tau_n, :] = yt_ref[pl.ds(s, tau_n, stride=seg), :].astype(y_ref.dtype)


def _s5_out_kernel(y_ref, x_ref, w_ref, g_ref, b_ref, o_ref, ob_ref):
    vg = _dot(y_ref[...], w_ref[...])
    val = vg[:, :D_MODEL]
    gate = vg[:, D_MODEL:]
    y = val * _sigmoid(gate) + DEEPNORM_ALPHA * x_ref[...]
    _store_ln(y, g_ref, b_ref, o_ref, ob_ref)


def _block_diag(m):
    nb, g, r, c = m.shape
    eye = jnp.eye(g, dtype=m.dtype)
    return (m[:, :, :, None, :] * eye[None, :, None, :, None]).reshape(nb, g * r, g * c)


def _s5_layer(x, a_re, a_im, b_re, b_im, c_re, c_im, d_skip, log_dt, w_glu, ln_g, ln_b, bsz, seq):
    t = x.shape[0]
    gpb = SSM_COLS // SSM_GROUP
    ncb = D_MODEL // SSM_COLS
    nch = gpb * SSM_STATE
    dt = jnp.exp(log_dt)[:, None]
    mag = jnp.exp(dt * a_re)
    ab_re = mag * jnp.cos(dt * a_im)
    ab_im = mag * jnp.sin(dt * a_im)
    den = a_re * a_re + a_im * a_im
    f_re = ((ab_re - 1.0) * a_re + ab_im * a_im) / den
    f_im = (ab_im * a_re - (ab_re - 1.0) * a_im) / den
    bb_re = f_re[..., None] * b_re - f_im[..., None] * b_im
    bb_im = f_re[..., None] * b_im + f_im[..., None] * b_re
    tb = lambda m: jnp.swapaxes(m, 1, 2).reshape(ncb, gpb, SSM_GROUP, SSM_STATE)
    w_b = jnp.concatenate([_block_diag(tb(bb_re)), _block_diag(tb(bb_im))], axis=2).astype(BF16)
    tc = lambda m: jnp.swapaxes(m, 1, 2).reshape(ncb, gpb, SSM_STATE, SSM_GROUP)
    w_c = jnp.concatenate([_block_diag(tc(c_re)), -_block_diag(tc(c_im))], axis=1).astype(BF16)
    tau_n = SSM_TIME // SUBLANES
    assert tau_n & (tau_n - 1) == 0
    pr, pi = ab_re[None], ab_im[None]
    while pr.shape[0] < tau_n:
        lr, li = pr[-1], pi[-1]
        pr, pi = (jnp.concatenate([pr, pr * lr - pi * li], axis=0),
                  jnp.concatenate([pi, pr * li + pi * lr], axis=0))
    pw_re = pr.reshape(tau_n, ncb, nch).transpose(1, 0, 2)
    pw_im = pi.reshape(tau_n, ncb, nch).transpose(1, 0, 2)
    x3 = x.reshape(bsz, seq, D_MODEL)
    blk = lambda shape: pl.BlockSpec((1,) + shape, lambda bi, cb, n: (cb, 0, 0))
    y = pl.pallas_call(
        _s5_kernel,
        grid=(bsz, ncb, seq // SSM_TIME),
        in_specs=[pl.BlockSpec((1, SSM_TIME, SSM_COLS), lambda bi, cb, n: (bi, n, cb)),
                  blk((SSM_COLS, 2 * nch)), blk((2 * nch, SSM_COLS)),
                  blk((tau_n, nch)), blk((tau_n, nch)),
                  pl.BlockSpec((1, SSM_COLS), lambda bi, cb, n: (0, cb))],
        out_specs=pl.BlockSpec((1, SSM_TIME, SSM_COLS), lambda bi, cb, n: (bi, n, cb)),
        out_shape=jax.ShapeDtypeStruct((bsz, seq, D_MODEL), BF16),
        scratch_shapes=[pltpu.VMEM((SUBLANES, nch), F32), pltpu.VMEM((SUBLANES, nch), F32),
                        pltpu.VMEM((SSM_TIME, SSM_COLS), F32),
                        pltpu.VMEM((tau_n, SUBLANES, nch), F32),
                        pltpu.VMEM((tau_n, SUBLANES, nch), F32),
                        pltpu.VMEM((SSM_TIME, SSM_COLS), F32)],
        compiler_params=_cparams(),
        name="s5_core",
    )(x3, w_b, w_c, pw_re, pw_im, d_skip.reshape(1, D_MODEL))
    tm = min(ROW_TILE, t)
    return pl.pallas_call(
        _s5_out_kernel,
        grid=(t // tm,),
        in_specs=[_row_spec(tm, D_MODEL), _row_spec(tm, D_MODEL), _const_spec((D_MODEL, 2 * D_MODEL)),
                  _const_spec((1, D_MODEL)), _const_spec((1, D_MODEL))],
        compiler_params=_cparams(),
        name="s5_out",
        **_x_outs(t, tm),
    )(y.reshape(t, D_MODEL), x, w_glu.astype(BF16), ln_g[None], ln_b[None])


def _cast_kernel(*refs):
    o_ref = refs[-1]
    o_ref[...] = jnp.concatenate([r[0].astype(o_ref.dtype) for r in refs[:-1]], axis=-1)


def _cast_bf16(*ws, layer, name):
    _, e, r, c = ws[0].shape
    return pl.pallas_call(
        _cast_kernel,
        grid=(e,),
        in_specs=[pl.BlockSpec((1, 1, r, c), lambda i: (layer, i, 0, 0)) for _ in ws],
        out_specs=pl.BlockSpec((1, r, c * len(ws)), lambda i: (i, 0, 0)),
        out_shape=jax.ShapeDtypeStruct((e, r, c * len(ws)), BF16),
        compiler_params=_cparams(),
        name=name,
    )(*ws)


def _first_index(eq, iota, size, axis):
    return jnp.min(jnp.where(eq, iota, size), axis=axis, keepdims=True)


def _router_kernel(x_ref, w_ref, b_ref, tri_ref, rank_ref, gate_ref, cnt_ref):
    x = x_ref[...]
    w = w_ref[...]
    x_hi = x.astype(BF16)
    x_lo = (x - x_hi.astype(F32)).astype(BF16)
    w_hi = w.astype(BF16)
    w_lo = (w - w_hi.astype(F32)).astype(BF16)
    logits = _dot_nt(w_hi, x_hi) + (_dot_nt(w_hi, x_lo) + _dot_nt(w_lo, x_hi))
    scores = _sigmoid(logits)
    sel = scores + b_ref[...]
    tt = sel.shape[1]
    gsz = N_EXPERTS // N_EXPERT_GROUPS
    grp = sel.reshape(N_EXPERT_GROUPS, gsz, tt)
    wi = lax.broadcasted_iota(jnp.int32, grp.shape, 1)
    m1 = jnp.max(grp, axis=1, keepdims=True)
    i1 = _first_index(grp == m1, wi, gsz, 1)
    m2 = jnp.max(jnp.where(wi == i1, -jnp.inf, grp), axis=1, keepdims=True)
    gs = (m1 + m2).reshape(N_EXPERT_GROUPS, tt)
    gi = lax.broadcasted_iota(jnp.int32, gs.shape, 0)
    gmask = jnp.zeros(gs.shape, jnp.bool_)
    for _ in range(TOPK_GROUPS):
        gm = jnp.max(gs, axis=0, keepdims=True)
        pick = gi == _first_index(gs == gm, gi, N_EXPERT_GROUPS, 0)
        gmask = gmask | pick
        gs = jnp.where(pick, -jnp.inf, gs)
    emask = jnp.broadcast_to(gmask[:, None, :], grp.shape).reshape(N_EXPERTS, tt)
    cand = jnp.where(emask, sel, -jnp.inf)
    ei = lax.broadcasted_iota(jnp.int32, cand.shape, 0)
    chosen = jnp.zeros(cand.shape, jnp.bool_)
    for _ in range(TOP_K):
        cm = jnp.max(cand, axis=0, keepdims=True)
        pick = ei == _first_index(cand == cm, ei, N_EXPERTS, 0)
        chosen = chosen | pick
        cand = jnp.where(pick, -jnp.inf, cand)
    top_w = jnp.where(chosen, scores, 0.0)
    denom = jnp.sum(top_w, axis=0, keepdims=True)
    gate_ref[...] = top_w / denom * ROUTED_SCALE
    chosen_f = chosen.astype(F32)
    rank = _dot(chosen_f.astype(BF16), tri_ref[...])
    rank_ref[...] = jnp.where(chosen, rank, -1.0).astype(jnp.int32)
    cnt = jnp.sum(chosen_f, axis=1, keepdims=True)
    cnt_ref[0] = jnp.broadcast_to(cnt, (N_EXPERTS, LANES)).astype(jnp.int32)


def _moe_kernel(cnt_ref, x_ref, rank_ref, gate_ref, wgu_ref, wd_ref, o_ref, oh_scr, yg_scr):
    j = pl.program_id(0)
    s = pl.program_id(1)

    @pl.when(s == 0)
    def _():
        o_ref[...] = jnp.zeros_like(o_ref)

    tt = x_ref.shape[0]
    slot = lax.broadcasted_iota(jnp.int32, (MOE_SLOTS, tt), 0)

    def expert_chunks(us, c):
        hits = [(rank_ref[u] - c * MOE_SLOTS) == slot for u in us]
        onehots = [hit.astype(F32).astype(BF16) for hit in hits]
        xcs = [_dot(oh, x_ref[...]).astype(BF16) for oh in onehots]
        hgus = [_dot(xc, wgu_ref[u]) for xc, u in zip(xcs, us)]
        hbs = [((h[:, :EXPERT_DIM] * _sigmoid(h[:, :EXPERT_DIM])) * h[:, EXPERT_DIM:]).astype(BF16)
               for h in hgus]
        ys = [_dot(hb, wd_ref[u]) for hb, u in zip(hbs, us)]
        gsels = [jnp.sum(jnp.where(hit, gate_ref[u], 0.0), axis=1, keepdims=True)
                 for hit, u in zip(hits, us)]
        return onehots, [(y * g).astype(BF16) for y, g in zip(ys, gsels)]

    steps_per_group = MOE_GROUP // MOE_UNROLL
    gs = s % steps_per_group
    us = list(range(MOE_UNROLL))
    onehots, ygs = expert_chunks(us, 0)
    for u in us:
        row0 = pl.multiple_of((gs * MOE_UNROLL + u) * MOE_SLOTS, MOE_SLOTS)
        oh_scr[pl.ds(row0, MOE_SLOTS), :] = onehots[u]
        yg_scr[pl.ds(row0, MOE_SLOTS), :] = ygs[u]

    for u in us:
        cnt = cnt_ref[j * N_EXPERTS + s * MOE_UNROLL + u]
        n_chunks = (cnt + MOE_SLOTS - 1) // MOE_SLOTS

        def overflow(c, carry, u=u):
            onehot_c, yg_c = expert_chunks([u], c)
            o_ref[...] += _dot_tn(onehot_c[0], yg_c[0])
            return carry

        lax.fori_loop(1, n_chunks, overflow, 0)

    @pl.when(gs == steps_per_group - 1)
    def _():
        o_ref[...] += _dot_tn(oh_scr[...], yg_scr[...])


def _moe_out_kernel(x_ref, xb_ref, acc_ref, wgu_ref, wd_ref, g_ref, b_ref, o_ref, ob_ref):
    hgu = _dot(xb_ref[...], wgu_ref[...])
    hg = hgu[:, :SHARED_DIM]
    hb = ((hg * _sigmoid(hg)) * hgu[:, SHARED_DIM:]).astype(BF16)
    shared = _dot(hb, wd_ref[...])
    y = DEEPNORM_ALPHA * x_ref[...] + (acc_ref[...] + shared)
    _store_ln(y, g_ref, b_ref, o_ref, ob_ref)


def _moe_layer(x, xb, w_router, router_bias, w_gate, w_up, w_down, sw_gate, sw_up, sw_down,
               ln_g, ln_b, layer):
    t = x.shape[0]
    tt = min(MOE_TILE, t)
    nj = t // tt
    tri = jnp.triu(jnp.ones((tt, tt), F32), 1).astype(BF16)
    rank, gate, cnt = pl.pallas_call(
        _router_kernel,
        grid=(nj,),
        in_specs=[_row_spec(tt, D_MODEL), _const_spec((N_EXPERTS, D_MODEL)),
                  _const_spec((N_EXPERTS, 1)), _const_spec((tt, tt))],
        out_specs=[pl.BlockSpec((N_EXPERTS, tt), lambda i: (0, i)),
                   pl.BlockSpec((N_EXPERTS, tt), lambda i: (0, i)),
                   pl.BlockSpec((1, N_EXPERTS, LANES), lambda i: (i, 0, 0))],
        out_shape=[jax.ShapeDtypeStruct((N_EXPERTS, t), jnp.int32),
                   jax.ShapeDtypeStruct((N_EXPERTS, t), F32),
                   jax.ShapeDtypeStruct((nj, N_EXPERTS, LANES), jnp.int32)],
        compiler_params=_cparams(),
        name="moe_router",
    )(x, w_router.T, router_bias[:, None], tri)
    wgu = _cast_bf16(w_gate, w_up, layer=layer, name="moe_cast_gu")
    wd = _cast_bf16(w_down, layer=layer, name="moe_cast_d")
    acc = pl.pallas_call(
        _moe_kernel,
        grid_spec=pltpu.PrefetchScalarGridSpec(
            num_scalar_prefetch=1,
            grid=(nj, N_EXPERTS // MOE_UNROLL),
            in_specs=[pl.BlockSpec((tt, D_MODEL), lambda j, e, c: (j, 0)),
                      pl.BlockSpec((MOE_UNROLL, 1, tt), lambda j, e, c: (e, 0, j)),
                      pl.BlockSpec((MOE_UNROLL, 1, tt), lambda j, e, c: (e, 0, j)),
                      pl.BlockSpec((MOE_UNROLL, D_MODEL, 2 * EXPERT_DIM), lambda j, e, c: (e, 0, 0)),
                      pl.BlockSpec((MOE_UNROLL, EXPERT_DIM, D_MODEL), lambda j, e, c: (e, 0, 0))],
            out_specs=pl.BlockSpec((tt, D_MODEL), lambda j, e, c: (j, 0)),
            scratch_shapes=[pltpu.VMEM((MOE_GROUP * MOE_SLOTS, tt), BF16),
                            pltpu.VMEM((MOE_GROUP * MOE_SLOTS, D_MODEL), BF16)],
        ),
        out_shape=jax.ShapeDtypeStruct((t, D_MODEL), F32),
        compiler_params=_cparams(),
        name="moe_experts",
    )(cnt[:, :, 0].reshape(-1), xb, rank.reshape(N_EXPERTS, 1, t), gate.reshape(N_EXPERTS, 1, t),
      wgu, wd)
    tm = min(ROW_TILE, t)
    sgu = jnp.concatenate([sw_gate, sw_up], axis=1).astype(BF16)
    return pl.pallas_call(
        _moe_out_kernel,
        grid=(t // tm,),
        in_specs=[_row_spec(tm, D_MODEL), _row_spec(tm, D_MODEL), _row_spec(tm, D_MODEL),
                  _const_spec((D_MODEL, 2 * SHARED_DIM)), _const_spec((SHARED_DIM, D_MODEL)),
                  _const_spec((1, D_MODEL)), _const_spec((1, D_MODEL))],
        compiler_params=_cparams(),
        name="moe_out",
        **_x_outs(t, tm),
    )(x, xb, acc, sgu, sw_down.astype(BF16), ln_g[None], ln_b[None])


def kernel(x, attn_w_in, attn_w_out, ret_w_in, ret_gn_g, ret_gn_b, ret_w_out, ssm_a_re, ssm_a_im, ssm_b_re, ssm_b_im, ssm_c_re, ssm_c_im, ssm_d, ssm_log_dt, ssm_w_glu, moe_w_router, moe_router_bias, moe_w_gate, moe_w_up, moe_w_down, shared_w_gate, shared_w_up, shared_w_down, ln_g, ln_b):
    bsz, seq, d = x.shape
    depth = ln_g.shape[0]
    xf = x.reshape(bsz * seq, d)
    xb = xf.astype(BF16)
    for i in range(depth):
        j = i // N_MIXERS
        kind = i % N_MIXERS
        if kind == 0:
            xf, xb = _attention_layer(xf, xb, attn_w_in[j], attn_w_out[j], ln_g[i, 0], ln_b[i, 0],
                                      bsz, seq)
        elif kind == 1:
            xf, xb = _retention_layer(xf, xb, ret_w_in[j], ret_gn_g[j], ret_gn_b[j], ret_w_out[j],
                                      ln_g[i, 0], ln_b[i, 0], bsz, seq)
        else:
            xf, xb = _s5_layer(xf, ssm_a_re[j], ssm_a_im[j], ssm_b_re[j], ssm_b_im[j], ssm_c_re[j],
                               ssm_c_im[j], ssm_d[j], ssm_log_dt[j], ssm_w_glu[j],
                               ln_g[i, 0], ln_b[i, 0], bsz, seq)
        xf, xb = _moe_layer(xf, xb, moe_w_router[i], moe_router_bias[i], moe_w_gate, moe_w_up,
                            moe_w_down, shared_w_gate[i], shared_w_up[i], shared_w_down[i],
                            ln_g[i, 1], ln_b[i, 1], i)
    return xf.reshape(bsz, seq, d)
```

```python
import functools
import math

import jax
import jax.numpy as jnp
from jax import lax
from jax.experimental import pallas as pl
from jax.experimental.pallas import tpu as pltpu

F32 = jnp.float32
BF16 = jnp.bfloat16

D_MODEL = 1024
DEPTH = 4
N_MIXERS = 3
ATTN_GROUPS = ((128, 1), (512, 4), (2048, 16))
ATTN_HEADS = 8
ATTN_HEAD_DIM = D_MODEL // ATTN_HEADS
ATTN_BLOCK = 128
RET_HEADS = 4
RET_KEY_DIM = D_MODEL // RET_HEADS
RET_VAL_DIM = 2 * RET_KEY_DIM
RET_CHUNK = 128
SSM_GROUP = 16
SSM_GROUPS = D_MODEL // SSM_GROUP
SSM_STATE = 64
N_EXPERTS = 64
TOP_K = 8
N_EXPERT_GROUPS = 8
TOPK_GROUPS = 4
EXPERT_DIM = 256
SHARED_DIM = 256
ROUTED_SCALE = 2.5
DEEPNORM_ALPHA = (2 * DEPTH) ** 0.25
LN_EPS = 1e-5

LANES = 128
SUBLANES = 8
MASK_NEG = -1e30
VMEM_LIMIT = 56 * 1024 * 1024

ROW_TILE = 512
PROJ_TM = 1024
PROJ_TN = 1024
MOE_TILE = 1024
MOE_SLOTS = 160
MOE_GROUP = 16
MOE_UNROLL = 4
SSM_TIME = 512
SSM_COLS = 128


def _cparams():
    return pltpu.CompilerParams(vmem_limit_bytes=VMEM_LIMIT)


def _dot(a, b):
    return jnp.dot(a, b, preferred_element_type=F32)


def _dot_nt(a, b):
    return lax.dot_general(a, b, (((1,), (1,)), ((), ())), preferred_element_type=F32)


def _dot_tn(a, b):
    return lax.dot_general(a, b, (((0,), (0,)), ((), ())), preferred_element_type=F32)


def _sigmoid(x):
    return 1.0 / (1.0 + jnp.exp(-x))


def _layer_norm(y, g, b):
    mu = jnp.mean(y, axis=-1, keepdims=True)
    yc = y - mu
    var = jnp.mean(yc * yc, axis=-1, keepdims=True)
    return yc * lax.rsqrt(var + LN_EPS) * g + b


def _store_ln(y, g_ref, b_ref, o_ref, ob_ref):
    out = _layer_norm(y, g_ref[...], b_ref[...])
    o_ref[...] = out
    ob_ref[...] = out.astype(BF16)


def _proj_kernel(x_ref, w_ref, o_ref):
    o_ref[...] = _dot(x_ref[...], w_ref[...]).astype(o_ref.dtype)


def _proj(xb, w, out_dtype, name):
    m, k = xb.shape
    n = w.shape[1]
    tm = min(PROJ_TM, m)
    tn = min(PROJ_TN, n)
    return pl.pallas_call(
        _proj_kernel,
        grid=(m // tm, n // tn),
        in_specs=[pl.BlockSpec((tm, k), lambda i, j: (i, 0)),
                  pl.BlockSpec((k, tn), lambda i, j: (0, j))],
        out_specs=pl.BlockSpec((tm, tn), lambda i, j: (i, j)),
        out_shape=jax.ShapeDtypeStruct((m, n), out_dtype),
        compiler_params=_cparams(),
        name=name,
    )(xb, w)


def _proj_perm_kernel(x_ref, w_ref, o_ref, xs_ref, col_ref, *, dil):
    rows = x_ref.shape[0] // dil

    @pl.when(pl.program_id(1) == 0)
    def _():
        for c in range(x_ref.shape[1] // LANES):
            cs = slice(c * LANES, (c + 1) * LANES)
            col_ref[...] = x_ref[:, cs]
            for r in range(dil):
                xs_ref[r * rows:(r + 1) * rows, cs] = col_ref[pl.ds(r, rows, stride=dil), :].astype(BF16)

    res = _dot(xs_ref[...], w_ref[...])
    o_ref[0] = res.reshape(dil, rows, res.shape[1]).astype(o_ref.dtype)


def _proj_perm(xf, w, dil, bsz, seq, name):
    k = xf.shape[1]
    n = w.shape[1]
    tm = min(PROJ_TM, seq)
    tn = min(PROJ_TN, n)
    assert seq % tm == 0 and tm % (dil * 2 * SUBLANES) == 0
    per_b = seq // tm
    rows = tm // dil
    return pl.pallas_call(
        functools.partial(_proj_perm_kernel, dil=dil),
        grid=(bsz * per_b, n // tn),
        in_specs=[pl.BlockSpec((tm, k), lambda i, j: (i, 0)),
                  pl.BlockSpec((k, tn), lambda i, j: (0, j))],
        out_specs=pl.BlockSpec((1, dil, rows, tn), lambda i, j: (i // per_b, 0, i % per_b, j)),
        out_shape=jax.ShapeDtypeStruct((bsz, dil, seq // dil, n), BF16),
        scratch_shapes=[pltpu.VMEM((tm, k), BF16), pltpu.VMEM((tm, LANES), F32)],
        compiler_params=_cparams(),
        name=name,
    )(xf, w)


def _attn_kernel(q_ref, kc_ref, vc_ref, bp_ref, bc_ref, o_ref, lse_ref,
                 o_scr, lse_scr, mix_scr, kp_scr, vp_scr, *, dil):
    n = pl.program_id(1)
    r = pl.program_id(2)

    @pl.when(n == 0)
    def _():
        kp_scr[r] = jnp.zeros(kp_scr.shape[1:], kp_scr.dtype)
        vp_scr[r] = jnp.zeros(vp_scr.shape[1:], vp_scr.dtype)

    first = jnp.where(n == 0, MASK_NEG, 0.0).astype(F32)
    scale = ATTN_HEAD_DIM ** -0.5
    lane = lax.broadcasted_iota(jnp.int32, (ATTN_BLOCK, LANES), 1)
    lse_tile = jnp.zeros((ATTN_BLOCK, LANES), F32)
    heads = range(ATTN_HEADS)
    sls = [slice(h * ATTN_HEAD_DIM, (h + 1) * ATTN_HEAD_DIM) for h in heads]
    sps = [_dot_nt(q_ref[0, 0, :, sls[h]], kp_scr[r, :, sls[h]]) * scale + (bp_ref[h] + first)
           for h in heads]
    scs = [_dot_nt(q_ref[0, 0, :, sls[h]], kc_ref[0, 0, :, sls[h]]) * scale + bc_ref[h]
           for h in heads]
    ms = [jnp.max(jnp.maximum(sps[h], scs[h]), axis=1, keepdims=True) for h in heads]
    pps = [jnp.exp(sps[h] - ms[h]) for h in heads]
    pcs = [jnp.exp(scs[h] - ms[h]) for h in heads]
    ls = [jnp.sum(pps[h] + pcs[h], axis=1, keepdims=True) for h in heads]
    for h in heads:
        o = (_dot(pps[h].astype(BF16), vp_scr[r, :, sls[h]])
             + _dot(pcs[h].astype(BF16), vc_ref[0, 0, :, sls[h]]))
        o_scr[h, r] = o / ls[h]
        lse_tile = jnp.where(lane == h, ms[h] + jnp.log(ls[h]), lse_tile)
    lse_scr[r] = lse_tile
    kp_scr[r] = kc_ref[0, 0]
    vp_scr[r] = vc_ref[0, 0]

    @pl.when(r == dil - 1)
    def _():
        for rr in range(dil):
            lse_ref[0, pl.ds(rr, ATTN_BLOCK, stride=dil), :] = lse_scr[rr]
        for h in range(ATTN_HEADS):
            for rr in range(dil):
                mix_scr[pl.ds(rr, ATTN_BLOCK, stride=dil), :] = o_scr[h, rr]
            o_ref[0, :, h * ATTN_HEAD_DIM:(h + 1) * ATTN_HEAD_DIM] = mix_scr[...]


def _attn_bias(window, dil):
    steps = window // dil
    assert steps <= ATTN_BLOCK
    qi = jnp.arange(ATTN_BLOCK)[:, None]
    kj = jnp.arange(ATTN_BLOCK)[None, :]
    slopes = 2.0 ** (-8.0 * (jnp.arange(ATTN_HEADS, dtype=F32) + 1.0) / ATTN_HEADS)

    def bias(step):
        valid = (step >= 0) & (step <= steps)
        dist = (step * dil).astype(F32)
        return jnp.where(valid[None], -(slopes[:, None, None] * dist[None]), MASK_NEG).astype(F32)

    return bias(qi - kj + ATTN_BLOCK), bias(qi - kj)


def _attn_group(qkv, gi, window, dil):
    b, _, ls, width = qkv.shape
    s = ls * dil
    hd = ATTN_HEADS * ATTN_HEAD_DIM
    assert ls % ATTN_BLOCK == 0 and width == 3 * hd
    nb = ls // ATTN_BLOCK
    bias_prev, bias_cur = _attn_bias(window, dil)

    def cur(c):
        return pl.BlockSpec((1, 1, ATTN_BLOCK, hd), lambda bi, n, r: (bi, r, n, c))

    const = pl.BlockSpec((ATTN_HEADS, ATTN_BLOCK, ATTN_BLOCK), lambda bi, n, r: (0, 0, 0))
    o, lse = pl.pallas_call(
        functools.partial(_attn_kernel, dil=dil),
        grid=(b, nb, dil),
        in_specs=[cur(0), cur(1), cur(2), const, const],
        out_specs=[pl.BlockSpec((1, ATTN_BLOCK * dil, hd), lambda bi, n, r: (bi, n, 0)),
                   pl.BlockSpec((1, ATTN_BLOCK * dil, LANES), lambda bi, n, r: (bi, n, 0))],
        out_shape=[jax.ShapeDtypeStruct((b, s, hd), F32),
                   jax.ShapeDtypeStruct((b, s, LANES), F32)],
        scratch_shapes=[pltpu.VMEM((ATTN_HEADS, dil, ATTN_BLOCK, ATTN_HEAD_DIM), F32),
                        pltpu.VMEM((dil, ATTN_BLOCK, LANES), F32),
                        pltpu.VMEM((dil * ATTN_BLOCK, ATTN_HEAD_DIM), F32),
                        pltpu.VMEM((dil, ATTN_BLOCK, hd), BF16),
                        pltpu.VMEM((dil, ATTN_BLOCK, hd), BF16)],
        compiler_params=_cparams(),
        name=f"attn_g{gi}",
    )(qkv, qkv, qkv, bias_prev, bias_cur)
    return o.reshape(b * s, hd), lse.reshape(b * s, LANES)


def _attn_out_kernel(o1_ref, o2_ref, o3_ref, l1_ref, l2_ref, l3_ref, x_ref, w_ref, g_ref, b_ref,
                     o_ref, ob_ref):
    l1, l2, l3 = l1_ref[...], l2_ref[...], l3_ref[...]
    mx = jnp.maximum(jnp.maximum(l1, l2), l3)
    e1, e2, e3 = jnp.exp(l1 - mx), jnp.exp(l2 - mx), jnp.exp(l3 - mx)
    den = e1 + e2 + e3
    w1, w2, w3 = e1 / den, e2 / den, e3 / den
    cols = []
    for h in range(ATTN_HEADS):
        sl = slice(h * ATTN_HEAD_DIM, (h + 1) * ATTN_HEAD_DIM)
        oh = (w1[:, h:h + 1] * o1_ref[:, sl] + w2[:, h:h + 1] * o2_ref[:, sl]
              + w3[:, h:h + 1] * o3_ref[:, sl])
        cols.append(oh.astype(BF16))
    o = jnp.concatenate(cols, axis=1)
    y = _dot(o, w_ref[...]) + DEEPNORM_ALPHA * x_ref[...]
    _store_ln(y, g_ref, b_ref, o_ref, ob_ref)


def _row_spec(tm, width):
    return pl.BlockSpec((tm, width), lambda i: (i, 0))


def _const_spec(shape):
    return pl.BlockSpec(shape, lambda i: tuple(0 for _ in shape))


def _x_outs(t, tm):
    return dict(
        out_specs=[_row_spec(tm, D_MODEL), _row_spec(tm, D_MODEL)],
        out_shape=[jax.ShapeDtypeStruct((t, D_MODEL), F32), jax.ShapeDtypeStruct((t, D_MODEL), BF16)],
    )


def _attention_layer(x, xb, w_in, w_out, ln_g, ln_b, bsz, seq):
    t = x.shape[0]
    hd3 = 3 * ATTN_HEADS * ATTN_HEAD_DIM
    wb = w_in.astype(BF16)
    outs = []
    for gi, (window, dil) in enumerate(ATTN_GROUPS):
        qkv = _proj_perm(x, wb[:, gi * hd3:(gi + 1) * hd3], dil, bsz, seq, f"attn_proj_g{gi}")
        outs.append(_attn_group(qkv, gi, window, dil))
    tm = min(ROW_TILE, t)
    hd = ATTN_HEADS * ATTN_HEAD_DIM
    return pl.pallas_call(
        _attn_out_kernel,
        grid=(t // tm,),
        in_specs=[_row_spec(tm, hd)] * 3 + [_row_spec(tm, LANES)] * 3
                 + [_row_spec(tm, D_MODEL), _const_spec((hd, D_MODEL)),
                    _const_spec((1, D_MODEL)), _const_spec((1, D_MODEL))],
        compiler_params=_cparams(),
        name="attn_out",
        **_x_outs(t, tm),
    )(outs[0][0], outs[1][0], outs[2][0], outs[0][1], outs[1][1], outs[2][1],
      x, w_out.astype(BF16), ln_g[None], ln_b[None])


def _ret_kernel(q_ref, k_ref, v_ref, g_ref, dec_ref, xi_ref, zeta_ref, cd_ref, gg_ref, gb_ref,
                y_ref, state_ref):
    n = pl.program_id(1)

    @pl.when(n == 0)
    def _():
        state_ref[...] = jnp.zeros_like(state_ref)

    kscale = RET_KEY_DIM ** -0.5
    dk, dv = RET_KEY_DIM, RET_VAL_DIM
    hs = range(RET_HEADS)
    qs = [q_ref[0, :, h * dk:(h + 1) * dk] for h in hs]
    ks = [k_ref[0, :, h * dk:(h + 1) * dk] for h in hs]
    vs = [v_ref[0, :, h * dv:(h + 1) * dv] for h in hs]
    scores = [_dot_nt(qs[h], ks[h]) * kscale * dec_ref[h] for h in hs]
    inners = [_dot(scores[h].astype(BF16), vs[h]) for h in hs]
    sts = [state_ref[h] for h in hs]
    crosses = [_dot((qs[h].astype(F32) * xi_ref[h]).astype(BF16), sts[h].astype(BF16)) for h in hs]
    kzs = [(ks[h].astype(F32) * kscale * zeta_ref[h]).astype(BF16) for h in hs]
    for h in hs:
        state_ref[h] = cd_ref[h, 0:1, 0:1] * sts[h] + _dot_tn(kzs[h], vs[h])
    for h in hs:
        sl = slice(h * dv, (h + 1) * dv)
        y = inners[h] + crosses[h]
        mu = jnp.mean(y, axis=1, keepdims=True)
        yc = y - mu
        var = jnp.mean(yc * yc, axis=1, keepdims=True)
        yn = yc * lax.rsqrt(var + LN_EPS) * gg_ref[:, sl] + gb_ref[:, sl]
        g = g_ref[0, :, sl]
        y_ref[0, :, sl] = ((g * _sigmoid(g)) * yn).astype(y_ref.dtype)


def _ret_out_kernel(y_ref, x_ref, w_ref, g_ref, b_ref, o_ref, ob_ref):
    y = _dot(y_ref[...], w_ref[...]) + DEEPNORM_ALPHA * x_ref[...]
    _store_ln(y, g_ref, b_ref, o_ref, ob_ref)


def _retention_layer(x, xb, w_in, gn_g, gn_b, w_out, ln_g, ln_b, bsz, seq):
    t = x.shape[0]
    h, dk, dv, c = RET_HEADS, RET_KEY_DIM, RET_VAL_DIM, RET_CHUNK
    nqkv = 2 * h * dk + h * dv
    wb = w_in.astype(BF16)
    qkv = _proj(xb, wb[:, :nqkv], BF16, "ret_proj_qkv").reshape(bsz, seq, nqkv)
    gate = _proj(xb, wb[:, nqkv:], F32, "ret_proj_gate").reshape(bsz, seq, h * dv)
    log_gamma = jnp.log(1.0 - 2.0 ** (-5.0 - jnp.arange(h, dtype=F32)))
    pos = jnp.arange(c, dtype=F32)
    rel = pos[:, None] - pos[None, :]
    decay = jnp.where(rel >= 0, jnp.exp(log_gamma[:, None, None] * jnp.maximum(rel, 0.0)), 0.0)
    xi = jnp.exp(log_gamma[:, None] * (pos[None, :] + 1.0))
    zeta = jnp.exp(log_gamma[:, None] * (c - 1.0 - pos[None, :]))
    xi = jnp.broadcast_to(xi[:, :, None], (h, c, dk))
    zeta = jnp.broadcast_to(zeta[:, :, None], (h, c, dk))
    cd = jnp.broadcast_to(jnp.exp(log_gamma * c)[:, None, None], (h, SUBLANES, LANES))
    assert h * dv == 2 * h * dk
    full = lambda *shape: pl.BlockSpec(shape, lambda bi, n: tuple(0 for _ in shape))
    y = pl.pallas_call(
        _ret_kernel,
        grid=(bsz, seq // c),
        in_specs=[pl.BlockSpec((1, c, h * dk), lambda bi, n: (bi, n, 0)),
                  pl.BlockSpec((1, c, h * dk), lambda bi, n: (bi, n, 1)),
                  pl.BlockSpec((1, c, h * dv), lambda bi, n: (bi, n, 1)),
                  pl.BlockSpec((1, c, h * dv), lambda bi, n: (bi, n, 0)),
                  full(h, c, c), full(h, c, dk), full(h, c, dk), full(h, SUBLANES, LANES),
                  full(1, h * dv), full(1, h * dv)],
        out_specs=pl.BlockSpec((1, c, h * dv), lambda bi, n: (bi, n, 0)),
        out_shape=jax.ShapeDtypeStruct((bsz, seq, h * dv), BF16),
        scratch_shapes=[pltpu.VMEM((h, dk, dv), F32)],
        compiler_params=_cparams(),
        name="ret_core",
    )(qkv, qkv, qkv, gate, decay, xi, zeta, cd, gn_g[None], gn_b[None])
    tm = min(ROW_TILE, t)
    return pl.pallas_call(
        _ret_out_kernel,
        grid=(t // tm,),
        in_specs=[_row_spec(tm, h * dv), _row_spec(tm, D_MODEL), _const_spec((h * dv, D_MODEL)),
                  _const_spec((1, D_MODEL)), _const_spec((1, D_MODEL))],
        compiler_params=_cparams(),
        name="ret_out",
        **_x_outs(t, tm),
    )(y.reshape(t, h * dv), x, w_out.astype(BF16), ln_g[None], ln_b[None])


def _gelu_tanh(y):
    return 0.5 * y * (1.0 + jnp.tanh(math.sqrt(2.0 / math.pi) * (y + 0.044715 * (y * y * y))))


def _s5_kernel(x_ref, wb_ref, wc_ref, pr_ref, pi_ref, d_ref, y_ref,
               cr_ref, ci_ref, xs_ref, hr_scr, hi_scr, yt_ref):
    n = pl.program_id(2)

    @pl.when(n == 0)
    def _():
        cr_ref[...] = jnp.zeros_like(cr_ref)
        ci_ref[...] = jnp.zeros_like(ci_ref)

    nch = pr_ref.shape[2]
    seg = SUBLANES
    tau_n = SSM_TIME // seg
    for s in range(seg):
        xs_ref[pl.ds(s, tau_n, stride=seg), :] = x_ref[0, s * tau_n:(s + 1) * tau_n, :]
    u = xs_ref[...]
    bu = _dot(u.astype(BF16), wb_ref[0])
    re = bu[:, :nch].reshape(tau_n, seg, nch)
    im = bu[:, nch:].reshape(tau_n, seg, nch)
    a_r = jnp.broadcast_to(pr_ref[0, 0:1, :], (seg, nch))
    a_i = jnp.broadcast_to(pi_ref[0, 0:1, :], (seg, nch))
    hr = jnp.zeros((seg, nch), F32)
    hi = jnp.zeros((seg, nch), F32)
    for t in range(tau_n):
        hr, hi = a_r * hr - a_i * hi + re[t], a_r * hi + a_i * hr + im[t]
        hr_scr[t] = hr
        hi_scr[t] = hi
    q_r = pr_ref[0, tau_n - 1:tau_n, :]
    q_i = pi_ref[0, tau_n - 1:tau_n, :]
    c_r = cr_ref[0:1, :]
    c_i = ci_ref[0:1, :]
    row = lax.broadcasted_iota(jnp.int32, (seg, nch), 0)
    in_r = jnp.zeros((seg, nch), F32)
    in_i = jnp.zeros((seg, nch), F32)
    for s in range(seg):
        in_r = jnp.where(row == s, c_r, in_r)
        in_i = jnp.where(row == s, c_i, in_i)
        c_r, c_i = (hr[s:s + 1, :] + (q_r * c_r - q_i * c_i),
                    hi[s:s + 1, :] + (q_r * c_i + q_i * c_r))
    cr_ref[...] = jnp.broadcast_to(c_r, cr_ref.shape)
    ci_ref[...] = jnp.broadcast_to(c_i, ci_ref.shape)
    hre, him = [], []
    for t in range(tau_n):
        p_r = pr_ref[0, t:t + 1, :]
        p_i = pi_ref[0, t:t + 1, :]
        hre.append(hr_scr[t] + (p_r * in_r - p_i * in_i))
        him.append(hi_scr[t] + (p_r * in_i + p_i * in_r))
    hcat = jnp.concatenate([jnp.concatenate(hre, axis=0), jnp.concatenate(him, axis=0)], axis=1)
    y = _dot(hcat.astype(BF16), wc_ref[0]) + d_ref[...] * u
    yt_ref[...] = _gelu_tanh(y)
    for s in range(seg):
        y_ref[0, s * tau_n:(s + 1) * tau_n, :] = yt_ref[pl.ds(s, tau_n, stride=seg), :].astype(y_ref.dtype)


def _s5_out_kernel(y_ref, x_ref, w_ref, g_ref, b_ref, o_ref, ob_ref):
    vg = _dot(y_ref[...], w_ref[...])
    val = vg[:, :D_MODEL]
    gate = vg[:, D_MODEL:]
    y = val * _sigmoid(gate) + DEEPNORM_ALPHA * x_ref[...]
    _store_ln(y, g_ref, b_ref, o_ref, ob_ref)


def _block_diag(m):
    nb, g, r, c = m.shape
    eye = jnp.eye(g, dtype=m.dtype)
    return (m[:, :, :, None, :] * eye[None, :, None, :, None]).reshape(nb, g * r, g * c)


def _s5_layer(x, a_re, a_im, b_re, b_im, c_re, c_im, d_skip, log_dt, w_glu, ln_g, ln_b, bsz, seq):
    t = x.shape[0]
    gpb = SSM_COLS // SSM_GROUP
    ncb = D_MODEL // SSM_COLS
    nch = gpb * SSM_STATE
    dt = jnp.exp(log_dt)[:, None]
    mag = jnp.exp(dt * a_re)
    ab_re = mag * jnp.cos(dt * a_im)
    ab_im = mag * jnp.sin(dt * a_im)
    den = a_re * a_re + a_im * a_im
    f_re = ((ab_re - 1.0) * a_re + ab_im * a_im) / den
    f_im = (ab_im * a_re - (ab_re - 1.0) * a_im) / den
    bb_re = f_re[..., None] * b_re - f_im[..., None] * b_im
    bb_im = f_re[..., None] * b_im + f_im[..., None] * b_re
    tb = lambda m: jnp.swapaxes(m, 1, 2).reshape(ncb, gpb, SSM_GROUP, SSM_STATE)
    w_b = jnp.concatenate([_block_diag(tb(bb_re)), _block_diag(tb(bb_im))], axis=2).astype(BF16)
    tc = lambda m: jnp.swapaxes(m, 1, 2).reshape(ncb, gpb, SSM_STATE, SSM_GROUP)
    w_c = jnp.concatenate([_block_diag(tc(c_re)), -_block_diag(tc(c_im))], axis=1).astype(BF16)
    tau_n = SSM_TIME // SUBLANES
    assert tau_n & (tau_n - 1) == 0
    pr, pi = ab_re[None], ab_im[None]
    while pr.shape[0] < tau_n:
        lr, li = pr[-1], pi[-1]
        pr, pi = (jnp.concatenate([pr, pr * lr - pi * li], axis=0),
                  jnp.concatenate([pi, pr * li + pi * lr], axis=0))
    pw_re = pr.reshape(tau_n, ncb, nch).transpose(1, 0, 2)
    pw_im = pi.reshape(tau_n, ncb, nch).transpose(1, 0, 2)
    x3 = x.reshape(bsz, seq, D_MODEL)
    blk = lambda shape: pl.BlockSpec((1,) + shape, lambda bi, cb, n: (cb, 0, 0))
    y = pl.pallas_call(
        _s5_kernel,
        grid=(bsz, ncb, seq // SSM_TIME),
        in_specs=[pl.BlockSpec((1, SSM_TIME, SSM_COLS), lambda bi, cb, n: (bi, n, cb)),
                  blk((SSM_COLS, 2 * nch)), blk((2 * nch, SSM_COLS)),
                  blk((tau_n, nch)), blk((tau_n, nch)),
                  pl.BlockSpec((1, SSM_COLS), lambda bi, cb, n: (0, cb))],
        out_specs=pl.BlockSpec((1, SSM_TIME, SSM_COLS), lambda bi, cb, n: (bi, n, cb)),
        out_shape=jax.ShapeDtypeStruct((bsz, seq, D_MODEL), BF16),
        scratch_shapes=[pltpu.VMEM((SUBLANES, nch), F32), pltpu.VMEM((SUBLANES, nch), F32),
                        pltpu.VMEM((SSM_TIME, SSM_COLS), F32),
                        pltpu.VMEM((tau_n, SUBLANES, nch), F32),
                        pltpu.VMEM((tau_n, SUBLANES, nch), F32),
                        pltpu.VMEM((SSM_TIME, SSM_COLS), F32)],
        compiler_params=_cparams(),
        name="s5_core",
    )(x3, w_b, w_c, pw_re, pw_im, d_skip.reshape(1, D_MODEL))
    tm = min(ROW_TILE, t)
    return pl.pallas_call(
        _s5_out_kernel,
        grid=(t // tm,),
        in_specs=[_row_spec(tm, D_MODEL), _row_spec(tm, D_MODEL), _const_spec((D_MODEL, 2 * D_MODEL)),
                  _const_spec((1, D_MODEL)), _const_spec((1, D_MODEL))],
        compiler_params=_cparams(),
        name="s5_out",
        **_x_outs(t, tm),
    )(y.reshape(t, D_MODEL), x, w_glu.astype(BF16), ln_g[None], ln_b[None])


def _cast_kernel(*refs):
    o_ref = refs[-1]
    o_ref[...] = jnp.concatenate([r[0].astype(o_ref.dtype) for r in refs[:-1]], axis=-1)


def _cast_bf16(*ws, layer, name):
    _, e, r, c = ws[0].shape
    return pl.pallas_call(
        _cast_kernel,
        grid=(e,),
        in_specs=[pl.BlockSpec((1, 1, r, c), lambda i: (layer, i, 0, 0)) for _ in ws],
        out_specs=pl.BlockSpec((1, r, c * len(ws)), lambda i: (i, 0, 0)),
        out_shape=jax.ShapeDtypeStruct((e, r, c * len(ws)), BF16),
        compiler_params=_cparams(),
        name=name,
    )(*ws)


def _first_index(eq, iota, size, axis):
    return jnp.min(jnp.where(eq, iota, size), axis=axis, keepdims=True)


def _router_kernel(x_ref, w_ref, b_ref, tri_ref, rank_ref, gate_ref, cnt_ref):
    x = x_ref[...]
    w = w_ref[...]
    x_hi = x.astype(BF16)
    x_lo = (x - x_hi.astype(F32)).astype(BF16)
    w_hi = w.astype(BF16)
    w_lo = (w - w_hi.astype(F32)).astype(BF16)
    logits = _dot_nt(w_hi, x_hi) + (_dot_nt(w_hi, x_lo) + _dot_nt(w_lo, x_hi))
    scores = _sigmoid(logits)
    sel = scores + b_ref[...]
    tt = sel.shape[1]
    gsz = N_EXPERTS // N_EXPERT_GROUPS
    grp = sel.reshape(N_EXPERT_GROUPS, gsz, tt)
    wi = lax.broadcasted_iota(jnp.int32, grp.shape, 1)
    m1 = jnp.max(grp, axis=1, keepdims=True)
    i1 = _first_index(grp == m1, wi, gsz, 1)
    m2 = jnp.max(jnp.where(wi == i1, -jnp.inf, grp), axis=1, keepdims=True)
    gs = (m1 + m2).reshape(N_EXPERT_GROUPS, tt)
    gi = lax.broadcasted_iota(jnp.int32, gs.shape, 0)
    gmask = jnp.zeros(gs.shape, jnp.bool_)
    for _ in range(TOPK_GROUPS):
        gm = jnp.max(gs, axis=0, keepdims=True)
        pick = gi == _first_index(gs == gm, gi, N_EXPERT_GROUPS, 0)
        gmask = gmask | pick
        gs = jnp.where(pick, -jnp.inf, gs)
    emask = jnp.broadcast_to(gmask[:, None, :], grp.shape).reshape(N_EXPERTS, tt)
    cand = jnp.where(emask, sel, -jnp.inf)
    ei = lax.broadcasted_iota(jnp.int32, cand.shape, 0)
    chosen = jnp.zeros(cand.shape, jnp.bool_)
    for _ in range(TOP_K):
        cm = jnp.max(cand, axis=0, keepdims=True)
        pick = ei == _first_index(cand == cm, ei, N_EXPERTS, 0)
        chosen = chosen | pick
        cand = jnp.where(pick, -jnp.inf, cand)
    top_w = jnp.where(chosen, scores, 0.0)
    denom = jnp.sum(top_w, axis=0, keepdims=True)
    gate_ref[...] = top_w / denom * ROUTED_SCALE
    chosen_f = chosen.astype(F32)
    rank = _dot(chosen_f.astype(BF16), tri_ref[...])
    rank_ref[...] = jnp.where(chosen, rank, -1.0).astype(jnp.int32)
    cnt = jnp.sum(chosen_f, axis=1, keepdims=True)
    cnt_ref[0] = jnp.broadcast_to(cnt, (N_EXPERTS, LANES)).astype(jnp.int32)


def _moe_kernel(cnt_ref, x_ref, rank_ref, gate_ref, wgu_ref, wd_ref, o_ref, oh_scr, yg_scr):
    j = pl.program_id(0)
    s = pl.program_id(1)

    @pl.when(s == 0)
    def _():
        o_ref[...] = jnp.zeros_like(o_ref)

    tt = x_ref.shape[0]
    slot = lax.broadcasted_iota(jnp.int32, (MOE_SLOTS, tt), 0)

    def expert_chunks(us, c):
        hits = [(rank_ref[u] - c * MOE_SLOTS) == slot for u in us]
        onehots = [hit.astype(F32).astype(BF16) for hit in hits]
        xcs = [_dot(oh, x_ref[...]).astype(BF16) for oh in onehots]
        hgus = [_dot(xc, wgu_ref[u]) for xc, u in zip(xcs, us)]
        hbs = [((h[:, :EXPERT_DIM] * _sigmoid(h[:, :EXPERT_DIM])) * h[:, EXPERT_DIM:]).astype(BF16)
               for h in hgus]
        ys = [_dot(hb, wd_ref[u]) for hb, u in zip(hbs, us)]
        gsels = [jnp.sum(jnp.where(hit, gate_ref[u], 0.0), axis=1, keepdims=True)
                 for hit, u in zip(hits, us)]
        return onehots, [(y * g).astype(BF16) for y, g in zip(ys, gsels)]

    steps_per_group = MOE_GROUP // MOE_UNROLL
    gs = s % steps_per_group
    us = list(range(MOE_UNROLL))
    onehots, ygs = expert_chunks(us, 0)
    for u in us:
        row0 = pl.multiple_of((gs * MOE_UNROLL + u) * MOE_SLOTS, MOE_SLOTS)
        oh_scr[pl.ds(row0, MOE_SLOTS), :] = onehots[u]
        yg_scr[pl.ds(row0, MOE_SLOTS), :] = ygs[u]

    for u in us:
        cnt = cnt_ref[j * N_EXPERTS + s * MOE_UNROLL + u]
        n_chunks = (cnt + MOE_SLOTS - 1) // MOE_SLOTS

        def overflow(c, carry, u=u):
            onehot_c, yg_c = expert_chunks([u], c)
            o_ref[...] += _dot_tn(onehot_c[0], yg_c[0])
            return carry

        lax.fori_loop(1, n_chunks, overflow, 0)

    @pl.when(gs == steps_per_group - 1)
    def _():
        o_ref[...] += _dot_tn(oh_scr[...], yg_scr[...])


def _moe_out_kernel(x_ref, xb_ref, acc_ref, wgu_ref, wd_ref, g_ref, b_ref, o_ref, ob_ref):
    hgu = _dot(xb_ref[...], wgu_ref[...])
    hg = hgu[:, :SHARED_DIM]
    hb = ((hg * _sigmoid(hg)) * hgu[:, SHARED_DIM:]).astype(BF16)
    shared = _dot(hb, wd_ref[...])
    y = DEEPNORM_ALPHA * x_ref[...] + (acc_ref[...] + shared)
    _store_ln(y, g_ref, b_ref, o_ref, ob_ref)


def _moe_layer(x, xb, w_router, router_bias, w_gate, w_up, w_down, sw_gate, sw_up, sw_down,
               ln_g, ln_b, layer):
    t = x.shape[0]
    tt = min(MOE_TILE, t)
    nj = t // tt
    tri = jnp.triu(jnp.ones((tt, tt), F32), 1).astype(BF16)
    rank, gate, cnt = pl.pallas_call(
        _router_kernel,
        grid=(nj,),
        in_specs=[_row_spec(tt, D_MODEL), _const_spec((N_EXPERTS, D_MODEL)),
                  _const_spec((N_EXPERTS, 1)), _const_spec((tt, tt))],
        out_specs=[pl.BlockSpec((N_EXPERTS, tt), lambda i: (0, i)),
                   pl.BlockSpec((N_EXPERTS, tt), lambda i: (0, i)),
                   pl.BlockSpec((1, N_EXPERTS, LANES), lambda i: (i, 0, 0))],
        out_shape=[jax.ShapeDtypeStruct((N_EXPERTS, t), jnp.int32),
                   jax.ShapeDtypeStruct((N_EXPERTS, t), F32),
                   jax.ShapeDtypeStruct((nj, N_EXPERTS, LANES), jnp.int32)],
        compiler_params=_cparams(),
        name="moe_router",
    )(x, w_router.T, router_bias[:, None], tri)
    wgu = _cast_bf16(w_gate, w_up, layer=layer, name="moe_cast_gu")
    wd = _cast_bf16(w_down, layer=layer, name="moe_cast_d")
    acc = pl.pallas_call(
        _moe_kernel,
        grid_spec=pltpu.PrefetchScalarGridSpec(
            num_scalar_prefetch=1,
            grid=(nj, N_EXPERTS // MOE_UNROLL),
            in_specs=[pl.BlockSpec((tt, D_MODEL), lambda j, e, c: (j, 0)),
                      pl.BlockSpec((MOE_UNROLL, 1, tt), lambda j, e, c: (e, 0, j)),
                      pl.BlockSpec((MOE_UNROLL, 1, tt), lambda j, e, c: (e, 0, j)),
                      pl.BlockSpec((MOE_UNROLL, D_MODEL, 2 * EXPERT_DIM), lambda j, e, c: (e, 0, 0)),
                      pl.BlockSpec((MOE_UNROLL, EXPERT_DIM, D_MODEL), lambda j, e, c: (e, 0, 0))],
            out_specs=pl.BlockSpec((tt, D_MODEL), lambda j, e, c: (j, 0)),
            scratch_shapes=[pltpu.VMEM((MOE_GROUP * MOE_SLOTS, tt), BF16),
                            pltpu.VMEM((MOE_GROUP * MOE_SLOTS, D_MODEL), BF16)],
        ),
        out_shape=jax.ShapeDtypeStruct((t, D_MODEL), F32),
        compiler_params=_cparams(),
        name="moe_experts",
    )(cnt[:, :, 0].reshape(-1), xb, rank.reshape(N_EXPERTS, 1, t), gate.reshape(N_EXPERTS, 1, t),
      wgu, wd)
    tm = min(ROW_TILE, t)
    sgu = jnp.concatenate([sw_gate, sw_up], axis=1).astype(BF16)
    return pl.pallas_call(
        _moe_out_kernel,
        grid=(t // tm,),
        in_specs=[_row_spec(tm, D_MODEL), _row_spec(tm, D_MODEL), _row_spec(tm, D_MODEL),
                  _const_spec((D_MODEL, 2 * SHARED_DIM)), _const_spec((SHARED_DIM, D_MODEL)),
                  _const_spec((1, D_MODEL)), _const_spec((1, D_MODEL))],
        compiler_params=_cparams(),
        name="moe_out",
        **_x_outs(t, tm),
    )(x, xb, acc, sgu, sw_down.astype(BF16), ln_g[None], ln_b[None])


def kernel(x, attn_w_in, attn_w_out, ret_w_in, ret_gn_g, ret_gn_b, ret_w_out, ssm_a_re, ssm_a_im, ssm_b_re, ssm_b_im, ssm_c_re, ssm_c_im, ssm_d, ssm_log_dt, ssm_w_glu, moe_w_router, moe_router_bias, moe_w_gate, moe_w_up, moe_w_down, shared_w_gate, shared_w_up, shared_w_down, ln_g, ln_b):
    bsz, seq, d = x.shape
    depth = ln_g.shape[0]
    xf = x.reshape(bsz * seq, d)
    xb = xf.astype(BF16)
    for i in range(depth):
        j = i // N_MIXERS
        kind = i % N_MIXERS
        if kind == 0:
            xf, xb = _attention_layer(xf, xb, attn_w_in[j], attn_w_out[j], ln_g[i, 0], ln_b[i, 0],
                                      bsz, seq)
        elif kind == 1:
            xf, xb = _retention_layer(xf, xb, ret_w_in[j], ret_gn_g[j], ret_gn_b[j], ret_w_out[j],
                                      ln_g[i, 0], ln_b[i, 0], bsz, seq)
        else:
            xf, xb = _s5_layer(xf, ssm_a_re[j], ssm_a_im[j], ssm_b_re[j], ssm_b_im[j], ssm_c_re[j],
                               ssm_c_im[j], ssm_d[j], ssm_log_dt[j], ssm_w_glu[j],
                               ln_g[i, 0], ln_b[i, 0], bsz, seq)
        xf, xb = _moe_layer(xf, xb, moe_w_router[i], moe_router_bias[i], moe_w_gate, moe_w_up,
                            moe_w_down, shared_w_gate[i], shared_w_up[i], shared_w_down[i],
                            ln_g[i, 1], ln_b[i, 1], i)
    return xf.reshape(bsz, seq, d)
```

```python
import functools
import math

import jax
import jax.numpy as jnp
from jax import lax
from jax.experimental import pallas as pl
from jax.experimental.pallas import tpu as pltpu

F32 = jnp.float32
BF16 = jnp.bfloat16

D_MODEL = 1024
DEPTH = 4
N_MIXERS = 3
ATTN_GROUPS = ((128, 1), (512, 4), (2048, 16))
ATTN_HEADS = 8
ATTN_HEAD_DIM = D_MODEL // ATTN_HEADS
ATTN_BLOCK = 128
RET_HEADS = 4
RET_KEY_DIM = D_MODEL // RET_HEADS
RET_VAL_DIM = 2 * RET_KEY_DIM
RET_CHUNK = 128
SSM_GROUP = 16
SSM_GROUPS = D_MODEL // SSM_GROUP
SSM_STATE = 64
N_EXPERTS = 64
TOP_K = 8
N_EXPERT_GROUPS = 8
TOPK_GROUPS = 4
EXPERT_DIM = 256
SHARED_DIM = 256
ROUTED_SCALE = 2.5
DEEPNORM_ALPHA = (2 * DEPTH) ** 0.25
LN_EPS = 1e-5

LANES = 128
SUBLANES = 8
MASK_NEG = -1e30
VMEM_LIMIT = 56 * 1024 * 1024

ROW_TILE = 512
PROJ_TM = 1024
PROJ_TN = 1024
MOE_TILE = 1024
MOE_SLOTS = 160
MOE_GROUP = 16
MOE_UNROLL = 4
SSM_TIME = 512
SSM_COLS = 128


def _cparams():
    return pltpu.CompilerParams(vmem_limit_bytes=VMEM_LIMIT)


def _dot(a, b):
    return jnp.dot(a, b, preferred_element_type=F32)


def _dot_nt(a, b):
    return lax.dot_general(a, b, (((1,), (1,)), ((), ())), preferred_element_type=F32)


def _dot_tn(a, b):
    return lax.dot_general(a, b, (((0,), (0,)), ((), ())), preferred_element_type=F32)


def _sigmoid(x):
    return 1.0 / (1.0 + jnp.exp(-x))


def _layer_norm(y, g, b):
    mu = jnp.mean(y, axis=-1, keepdims=True)
    yc = y - mu
    var = jnp.mean(yc * yc, axis=-1, keepdims=True)
    return yc * lax.rsqrt(var + LN_EPS) * g + b


def _store_ln(y, g_ref, b_ref, o_ref, ob_ref):
    out = _layer_norm(y, g_ref[...], b_ref[...])
    o_ref[...] = out
    ob_ref[...] = out.astype(BF16)


def _proj_kernel(x_ref, w_ref, o_ref):
    o_ref[...] = _dot(x_ref[...], w_ref[...]).astype(o_ref.dtype)


def _proj(xb, w, out_dtype, name):
    m, k = xb.shape
    n = w.shape[1]
    tm = min(PROJ_TM, m)
    tn = min(PROJ_TN, n)
    return pl.pallas_call(
        _proj_kernel,
        grid=(m // tm, n // tn),
        in_specs=[pl.BlockSpec((tm, k), lambda i, j: (i, 0)),
                  pl.BlockSpec((k, tn), lambda i, j: (0, j))],
        out_specs=pl.BlockSpec((tm, tn), lambda i, j: (i, j)),
        out_shape=jax.ShapeDtypeStruct((m, n), out_dtype),
        compiler_params=_cparams(),
        name=name,
    )(xb, w)


def _proj_perm_kernel(x_ref, w_ref, o_ref, xs_ref, col_ref, *, dil):
    rows = x_ref.shape[0] // dil

    @pl.when(pl.program_id(1) == 0)
    def _():
        for c in range(x_ref.shape[1] // LANES):
            cs = slice(c * LANES, (c + 1) * LANES)
            col_ref[...] = x_ref[:, cs]
            for r in range(dil):
                xs_ref[r * rows:(r + 1) * rows, cs] = col_ref[pl.ds(r, rows, stride=dil), :].astype(BF16)

    res = _dot(xs_ref[...], w_ref[...])
    o_ref[0] = res.reshape(dil, rows, res.shape[1]).astype(o_ref.dtype)


def _proj_perm(xf, w, dil, bsz, seq, name):
    k = xf.shape[1]
    n = w.shape[1]
    tm = min(PROJ_TM, seq)
    tn = min(PROJ_TN, n)
    assert seq % tm == 0 and tm % (dil * 2 * SUBLANES) == 0
    per_b = seq // tm
    rows = tm // dil
    return pl.pallas_call(
        functools.partial(_proj_perm_kernel, dil=dil),
        grid=(bsz * per_b, n // tn),
        in_specs=[pl.BlockSpec((tm, k), lambda i, j: (i, 0)),
                  pl.BlockSpec((k, tn), lambda i, j: (0, j))],
        out_specs=pl.BlockSpec((1, dil, rows, tn), lambda i, j: (i // per_b, 0, i % per_b, j)),
        out_shape=jax.ShapeDtypeStruct((bsz, dil, seq // dil, n), BF16),
        scratch_shapes=[pltpu.VMEM((tm, k), BF16), pltpu.VMEM((tm, LANES), F32)],
        compiler_params=_cparams(),
        name=name,
    )(xf, w)


def _attn_kernel(q_ref, kc_ref, vc_ref, bp_ref, bc_ref, o_ref, lse_ref,
                 o_scr, lse_scr, mix_scr, kp_scr, vp_scr, *, dil):
    n = pl.program_id(1)
    r = pl.program_id(2)

    @pl.when(n == 0)
    def _():
        kp_scr[r] = jnp.zeros(kp_scr.shape[1:], kp_scr.dtype)
        vp_scr[r] = jnp.zeros(vp_scr.shape[1:], vp_scr.dtype)

    first = jnp.where(n == 0, MASK_NEG, 0.0).astype(F32)
    scale = ATTN_HEAD_DIM ** -0.5
    lane = lax.broadcasted_iota(jnp.int32, (ATTN_BLOCK, LANES), 1)
    lse_tile = jnp.zeros((ATTN_BLOCK, LANES), F32)
    heads = range(ATTN_HEADS)
    sls = [slice(h * ATTN_HEAD_DIM, (h + 1) * ATTN_HEAD_DIM) for h in heads]
    sps = [_dot_nt(q_ref[0, 0, :, sls[h]], kp_scr[r, :, sls[h]]) * scale + (bp_ref[h] + first)
           for h in heads]
    scs = [_dot_nt(q_ref[0, 0, :, sls[h]], kc_ref[0, 0, :, sls[h]]) * scale + bc_ref[h]
           for h in heads]
    ms = [jnp.max(jnp.maximum(sps[h], scs[h]), axis=1, keepdims=True) for h in heads]
    pps = [jnp.exp(sps[h] - ms[h]) for h in heads]
    pcs = [jnp.exp(scs[h] - ms[h]) for h in heads]
    ls = [jnp.sum(pps[h] + pcs[h], axis=1, keepdims=True) for h in heads]
    for h in heads:
        o = (_dot(pps[h].astype(BF16), vp_scr[r, :, sls[h]])
             + _dot(pcs[h].astype(BF16), vc_ref[0, 0, :, sls[h]]))
        if dil == 1:
            o_ref[0, :, sls[h]] = o / ls[h]
        else:
            o_scr[h, r] = o / ls[h]
        lse_tile = jnp.where(lane == h, ms[h] + jnp.log(ls[h]), lse_tile)
    kp_scr[r] = kc_ref[0, 0]
    vp_scr[r] = vc_ref[0, 0]
    if dil == 1:
        lse_ref[0] = lse_tile
        return
    lse_scr[r] = lse_tile

    @pl.when(r == dil - 1)
    def _():
        for rr in range(dil):
            lse_ref[0, pl.ds(rr, ATTN_BLOCK, stride=dil), :] = lse_scr[rr]
        for h in range(ATTN_HEADS):
            for rr in range(dil):
                mix_scr[pl.ds(rr, ATTN_BLOCK, stride=dil), :] = o_scr[h, rr]
            o_ref[0, :, h * ATTN_HEAD_DIM:(h + 1) * ATTN_HEAD_DIM] = mix_scr[...]


def _attn_bias(window, dil):
    steps = window // dil
    assert steps <= ATTN_BLOCK
    qi = jnp.arange(ATTN_BLOCK)[:, None]
    kj = jnp.arange(ATTN_BLOCK)[None, :]
    slopes = 2.0 ** (-8.0 * (jnp.arange(ATTN_HEADS, dtype=F32) + 1.0) / ATTN_HEADS)

    def bias(step):
        valid = (step >= 0) & (step <= steps)
        dist = (step * dil).astype(F32)
        return jnp.where(valid[None], -(slopes[:, None, None] * dist[None]), MASK_NEG).astype(F32)

    return bias(qi - kj + ATTN_BLOCK), bias(qi - kj)


def _attn_group(qkv, gi, window, dil):
    b, _, ls, width = qkv.shape
    s = ls * dil
    hd = ATTN_HEADS * ATTN_HEAD_DIM
    assert ls % ATTN_BLOCK == 0 and width == 3 * hd
    nb = ls // ATTN_BLOCK
    bias_prev, bias_cur = _attn_bias(window, dil)

    def cur(c):
        return pl.BlockSpec((1, 1, ATTN_BLOCK, hd), lambda bi, n, r: (bi, r, n, c))

    const = pl.BlockSpec((ATTN_HEADS, ATTN_BLOCK, ATTN_BLOCK), lambda bi, n, r: (0, 0, 0))
    o, lse = pl.pallas_call(
        functools.partial(_attn_kernel, dil=dil),
        grid=(b, nb, dil),
        in_specs=[cur(0), cur(1), cur(2), const, const],
        out_specs=[pl.BlockSpec((1, ATTN_BLOCK * dil, hd), lambda bi, n, r: (bi, n, 0)),
                   pl.BlockSpec((1, ATTN_BLOCK * dil, LANES), lambda bi, n, r: (bi, n, 0))],
        out_shape=[jax.ShapeDtypeStruct((b, s, hd), F32),
                   jax.ShapeDtypeStruct((b, s, LANES), F32)],
        scratch_shapes=[pltpu.VMEM((ATTN_HEADS, dil, ATTN_BLOCK, ATTN_HEAD_DIM), F32),
                        pltpu.VMEM((dil, ATTN_BLOCK, LANES), F32),
                        pltpu.VMEM((dil * ATTN_BLOCK, ATTN_HEAD_DIM), F32),
                        pltpu.VMEM((dil, ATTN_BLOCK, hd), BF16),
                        pltpu.VMEM((dil, ATTN_BLOCK, hd), BF16)],
        compiler_params=_cparams(),
        name=f"attn_g{gi}",
    )(qkv, qkv, qkv, bias_prev, bias_cur)
    return o.reshape(b * s, hd), lse.reshape(b * s, LANES)


def _attn_out_kernel(o1_ref, o2_ref, o3_ref, l1_ref, l2_ref, l3_ref, x_ref, w_ref, g_ref, b_ref,
                     o_ref, ob_ref):
    l1, l2, l3 = l1_ref[...], l2_ref[...], l3_ref[...]
    mx = jnp.maximum(jnp.maximum(l1, l2), l3)
    e1, e2, e3 = jnp.exp(l1 - mx), jnp.exp(l2 - mx), jnp.exp(l3 - mx)
    den = e1 + e2 + e3
    w1, w2, w3 = e1 / den, e2 / den, e3 / den
    cols = []
    for h in range(ATTN_HEADS):
        sl = slice(h * ATTN_HEAD_DIM, (h + 1) * ATTN_HEAD_DIM)
        oh = (w1[:, h:h + 1] * o1_ref[:, sl] + w2[:, h:h + 1] * o2_ref[:, sl]
              + w3[:, h:h + 1] * o3_ref[:, sl])
        cols.append(oh.astype(BF16))
    o = jnp.concatenate(cols, axis=1)
    y = _dot(o, w_ref[...]) + DEEPNORM_ALPHA * x_ref[...]
    _store_ln(y, g_ref, b_ref, o_ref, ob_ref)


def _row_spec(tm, width):
    return pl.BlockSpec((tm, width), lambda i: (i, 0))


def _const_spec(shape):
    return pl.BlockSpec(shape, lambda i: tuple(0 for _ in shape))


def _x_outs(t, tm):
    return dict(
        out_specs=[_row_spec(tm, D_MODEL), _row_spec(tm, D_MODEL)],
        out_shape=[jax.ShapeDtypeStruct((t, D_MODEL), F32), jax.ShapeDtypeStruct((t, D_MODEL), BF16)],
    )


def _attention_layer(x, xb, w_in, w_out, ln_g, ln_b, bsz, seq):
    t = x.shape[0]
    hd3 = 3 * ATTN_HEADS * ATTN_HEAD_DIM
    wb = w_in.astype(BF16)
    outs = []
    for gi, (window, dil) in enumerate(ATTN_GROUPS):
        qkv = _proj_perm(x, wb[:, gi * hd3:(gi + 1) * hd3], dil, bsz, seq, f"attn_proj_g{gi}")
        outs.append(_attn_group(qkv, gi, window, dil))
    tm = min(ROW_TILE, t)
    hd = ATTN_HEADS * ATTN_HEAD_DIM
    return pl.pallas_call(
        _attn_out_kernel,
        grid=(t // tm,),
        in_specs=[_row_spec(tm, hd)] * 3 + [_row_spec(tm, LANES)] * 3
                 + [_row_spec(tm, D_MODEL), _const_spec((hd, D_MODEL)),
                    _const_spec((1, D_MODEL)), _const_spec((1, D_MODEL))],
        compiler_params=_cparams(),
        name="attn_out",
        **_x_outs(t, tm),
    )(outs[0][0], outs[1][0], outs[2][0], outs[0][1], outs[1][1], outs[2][1],
      x, w_out.astype(BF16), ln_g[None], ln_b[None])


def _ret_kernel(q_ref, k_ref, v_ref, g_ref, dec_ref, xi_ref, zeta_ref, cd_ref, gg_ref, gb_ref,
                y_ref, state_ref):
    n = pl.program_id(1)

    @pl.when(n == 0)
    def _():
        state_ref[...] = jnp.zeros_like(state_ref)

    kscale = RET_KEY_DIM ** -0.5
    dk, dv = RET_KEY_DIM, RET_VAL_DIM
    hs = range(RET_HEADS)
    qs = [q_ref[0, :, h * dk:(h + 1) * dk] for h in hs]
    ks = [k_ref[0, :, h * dk:(h + 1) * dk] for h in hs]
    vs = [v_ref[0, :, h * dv:(h + 1) * dv] for h in hs]
    scores = [_dot_nt(qs[h], ks[h]) * kscale * dec_ref[h] for h in hs]
    inners = [_dot(scores[h].astype(BF16), vs[h]) for h in hs]
    sts = [state_ref[h] for h in hs]
    crosses = [_dot((qs[h].astype(F32) * xi_ref[h]).astype(BF16), sts[h].astype(BF16)) for h in hs]
    kzs = [(ks[h].astype(F32) * kscale * zeta_ref[h]).astype(BF16) for h in hs]
    for h in hs:
        state_ref[h] = cd_ref[h, 0:1, 0:1] * sts[h] + _dot_tn(kzs[h], vs[h])
    for h in hs:
        sl = slice(h * dv, (h + 1) * dv)
        y = inners[h] + crosses[h]
        mu = jnp.mean(y, axis=1, keepdims=True)
        yc = y - mu
        var = jnp.mean(yc * yc, axis=1, keepdims=True)
        yn = yc * lax.rsqrt(var + LN_EPS) * gg_ref[:, sl] + gb_ref[:, sl]
        g = g_ref[0, :, sl]
        y_ref[0, :, sl] = ((g * _sigmoid(g)) * yn).astype(y_ref.dtype)


def _ret_out_kernel(y_ref, x_ref, w_ref, g_ref, b_ref, o_ref, ob_ref):
    y = _dot(y_ref[...], w_ref[...]) + DEEPNORM_ALPHA * x_ref[...]
    _store_ln(y, g_ref, b_ref, o_ref, ob_ref)


def _retention_layer(x, xb, w_in, gn_g, gn_b, w_out, ln_g, ln_b, bsz, seq):
    t = x.shape[0]
    h, dk, dv, c = RET_HEADS, RET_KEY_DIM, RET_VAL_DIM, RET_CHUNK
    nqkv = 2 * h * dk + h * dv
    wb = w_in.astype(BF16)
    qkv = _proj(xb, wb[:, :nqkv], BF16, "ret_proj_qkv").reshape(bsz, seq, nqkv)
    gate = _proj(xb, wb[:, nqkv:], F32, "ret_proj_gate").reshape(bsz, seq, h * dv)
    log_gamma = jnp.log(1.0 - 2.0 ** (-5.0 - jnp.arange(h, dtype=F32)))
    pos = jnp.arange(c, dtype=F32)
    rel = pos[:, None] - pos[None, :]
    decay = jnp.where(rel >= 0, jnp.exp(log_gamma[:, None, None] * jnp.maximum(rel, 0.0)), 0.0)
    xi = jnp.exp(log_gamma[:, None] * (pos[None, :] + 1.0))
    zeta = jnp.exp(log_gamma[:, None] * (c - 1.0 - pos[None, :]))
    xi = jnp.broadcast_to(xi[:, :, None], (h, c, dk))
    zeta = jnp.broadcast_to(zeta[:, :, None], (h, c, dk))
    cd = jnp.broadcast_to(jnp.exp(log_gamma * c)[:, None, None], (h, SUBLANES, LANES))
    assert h * dv == 2 * h * dk
    full = lambda *shape: pl.BlockSpec(shape, lambda bi, n: tuple(0 for _ in shape))
    y = pl.pallas_call(
        _ret_kernel,
        grid=(bsz, seq // c),
        in_specs=[pl.BlockSpec((1, c, h * dk), lambda bi, n: (bi, n, 0)),
                  pl.BlockSpec((1, c, h * dk), lambda bi, n: (bi, n, 1)),
                  pl.BlockSpec((1, c, h * dv), lambda bi, n: (bi, n, 1)),
                  pl.BlockSpec((1, c, h * dv), lambda bi, n: (bi, n, 0)),
                  full(h, c, c), full(h, c, dk), full(h, c, dk), full(h, SUBLANES, LANES),
                  full(1, h * dv), full(1, h * dv)],
        out_specs=pl.BlockSpec((1, c, h * dv), lambda bi, n: (bi, n, 0)),
        out_shape=jax.ShapeDtypeStruct((bsz, seq, h * dv), BF16),
        scratch_shapes=[pltpu.VMEM((h, dk, dv), F32)],
        compiler_params=_cparams(),
        name="ret_core",
    )(qkv, qkv, qkv, gate, decay, xi, zeta, cd, gn_g[None], gn_b[None])
    tm = min(ROW_TILE, t)
    return pl.pallas_call(
        _ret_out_kernel,
        grid=(t // tm,),
        in_specs=[_row_spec(tm, h * dv), _row_spec(tm, D_MODEL), _const_spec((h * dv, D_MODEL)),
                  _const_spec((1, D_MODEL)), _const_spec((1, D_MODEL))],
        compiler_params=_cparams(),
        name="ret_out",
        **_x_outs(t, tm),
    )(y.reshape(t, h * dv), x, w_out.astype(BF16), ln_g[None], ln_b[None])


def _gelu_tanh(y):
    return 0.5 * y * (1.0 + jnp.tanh(math.sqrt(2.0 / math.pi) * (y + 0.044715 * (y * y * y))))


def _s5_kernel(x_ref, wb_ref, wc_ref, pr_ref, pi_ref, d_ref, y_ref,
               cr_ref, ci_ref, xs_ref, hr_scr, hi_scr, yt_ref):
    n = pl.program_id(2)

    @pl.when(n == 0)
    def _():
        cr_ref[...] = jnp.zeros_like(cr_ref)
        ci_ref[...] = jnp.zeros_like(ci_ref)

    nch = pr_ref.shape[2]
    seg = SUBLANES
    tau_n = SSM_TIME // seg
    for s in range(seg):
        xs_ref[pl.ds(s, tau_n, stride=seg), :] = x_ref[0, s * tau_n:(s + 1) * tau_n, :]
    u = xs_ref[...]
    bu = _dot(u.astype(BF16), wb_ref[0])
    re = bu[:, :nch].reshape(tau_n, seg, nch)
    im = bu[:, nch:].reshape(tau_n, seg, nch)
    a_r = jnp.broadcast_to(pr_ref[0, 0:1, :], (seg, nch))
    a_i = jnp.broadcast_to(pi_ref[0, 0:1, :], (seg, nch))
    hr = jnp.zeros((seg, nch), F32)
    hi = jnp.zeros((seg, nch), F32)
    for t in range(tau_n):
        hr, hi = a_r * hr - a_i * hi + re[t], a_r * hi + a_i * hr + im[t]
        hr_scr[t] = hr
        hi_scr[t] = hi
    q_r = pr_ref[0, tau_n - 1:tau_n, :]
    q_i = pi_ref[0, tau_n - 1:tau_n, :]
    c_r = cr_ref[0:1, :]
    c_i = ci_ref[0:1, :]
    row = lax.broadcasted_iota(jnp.int32, (seg, nch), 0)
    in_r = jnp.zeros((seg, nch), F32)
    in_i = jnp.zeros((seg, nch), F32)
    for s in range(seg):
        in_r = jnp.where(row == s, c_r, in_r)
        in_i = jnp.where(row == s, c_i, in_i)
        c_r, c_i = (hr[s:s + 1, :] + (q_r * c_r - q_i * c_i),
                    hi[s:s + 1, :] + (q_r * c_i + q_i * c_r))
    cr_ref[...] = jnp.broadcast_to(c_r, cr_ref.shape)
    ci_ref[...] = jnp.broadcast_to(c_i, ci_ref.shape)
    hre, him = [], []
    for t in range(tau_n):
        p_r = pr_ref[0, t:t + 1, :]
        p_i = pi_ref[0, t:t + 1, :]
        hre.append(hr_scr[t] + (p_r * in_r - p_i * in_i))
        him.append(hi_scr[t] + (p_r * in_i + p_i * in_r))
    hcat = jnp.concatenate([jnp.concatenate(hre, axis=0), jnp.concatenate(him, axis=0)], axis=1)
    y = _dot(hcat.astype(BF16), wc_ref[0]) + d_ref[...] * u
    yt_ref[...] = _gelu_tanh(y)
    for s in range(seg):
        y_ref[0, s * tau_n:(s + 1) * tau_n, :] = yt_ref[pl.ds(s, tau_n, stride=seg), :].astype(y_ref.dtype)


def _s5_out_kernel(y_ref, x_ref, w_ref, g_ref, b_ref, o_ref, ob_ref):
    vg = _dot(y_ref[...], w_ref[...])
    val = vg[:, :D_MODEL]
    gate = vg[:, D_MODEL:]
    y = val * _sigmoid(gate) + DEEPNORM_ALPHA * x_ref[...]
    _store_ln(y, g_ref, b_ref, o_ref, ob_ref)


def _block_diag(m):
    nb, g, r, c = m.shape
    eye = jnp.eye(g, dtype=m.dtype)
    return (m[:, :, :, None, :] * eye[None, :, None, :, None]).reshape(nb, g * r, g * c)


def _s5_layer(x, a_re, a_im, b_re, b_im, c_re, c_im, d_skip, log_dt, w_glu, ln_g, ln_b, bsz, seq):
    t = x.shape[0]
    gpb = SSM_COLS // SSM_GROUP
    ncb = D_MODEL // SSM_COLS
    nch = gpb * SSM_STATE
    dt = jnp.exp(log_dt)[:, None]
    mag = jnp.exp(dt * a_re)
    ab_re = mag * jnp.cos(dt * a_im)
    ab_im = mag * jnp.sin(dt * a_im)
    den = a_re * a_re + a_im * a_im
    f_re = ((ab_re - 1.0) * a_re + ab_im * a_im) / den
    f_im = (ab_im * a_re - (ab_re - 1.0) * a_im) / den
    bb_re = f_re[..., None] * b_re - f_im[..., None] * b_im
    bb_im = f_re[..., None] * b_im + f_im[..., None] * b_re
    tb = lambda m: jnp.swapaxes(m, 1, 2).reshape(ncb, gpb, SSM_GROUP, SSM_STATE)
    w_b = jnp.concatenate([_block_diag(tb(bb_re)), _block_diag(tb(bb_im))], axis=2).astype(BF16)
    tc = lambda m: jnp.swapaxes(m, 1, 2).reshape(ncb, gpb, SSM_STATE, SSM_GROUP)
    w_c = jnp.concatenate([_block_diag(tc(c_re)), -_block_diag(tc(c_im))], axis=1).astype(BF16)
    tau_n = SSM_TIME // SUBLANES
    assert tau_n & (tau_n - 1) == 0
    pr, pi = ab_re[None], ab_im[None]
    while pr.shape[0] < tau_n:
        lr, li = pr[-1], pi[-1]
        pr, pi = (jnp.concatenate([pr, pr * lr - pi * li], axis=0),
                  jnp.concatenate([pi, pr * li + pi * lr], axis=0))
    pw_re = pr.reshape(tau_n, ncb, nch).transpose(1, 0, 2)
    pw_im = pi.reshape(tau_n, ncb, nch).transpose(1, 0, 2)
    x3 = x.reshape(bsz, seq, D_MODEL)
    blk = lambda shape: pl.BlockSpec((1,) + shape, lambda bi, cb, n: (cb, 0, 0))
    y = pl.pallas_call(
        _s5_kernel,
        grid=(bsz, ncb, seq // SSM_TIME),
        in_specs=[pl.BlockSpec((1, SSM_TIME, SSM_COLS), lambda bi, cb, n: (bi, n, cb)),
                  blk((SSM_COLS, 2 * nch)), blk((2 * nch, SSM_COLS)),
                  blk((tau_n, nch)), blk((tau_n, nch)),
                  pl.BlockSpec((1, SSM_COLS), lambda bi, cb, n: (0, cb))],
        out_specs=pl.BlockSpec((1, SSM_TIME, SSM_COLS), lambda bi, cb, n: (bi, n, cb)),
        out_shape=jax.ShapeDtypeStruct((bsz, seq, D_MODEL), BF16),
        scratch_shapes=[pltpu.VMEM((SUBLANES, nch), F32), pltpu.VMEM((SUBLANES, nch), F32),
                        pltpu.VMEM((SSM_TIME, SSM_COLS), F32),
                        pltpu.VMEM((tau_n, SUBLANES, nch), F32),
                        pltpu.VMEM((tau_n, SUBLANES, nch), F32),
                        pltpu.VMEM((SSM_TIME, SSM_COLS), F32)],
        compiler_params=_cparams(),
        name="s5_core",
    )(x3, w_b, w_c, pw_re, pw_im, d_skip.reshape(1, D_MODEL))
    tm = min(ROW_TILE, t)
    return pl.pallas_call(
        _s5_out_kernel,
        grid=(t // tm,),
        in_specs=[_row_spec(tm, D_MODEL), _row_spec(tm, D_MODEL), _const_spec((D_MODEL, 2 * D_MODEL)),
                  _const_spec((1, D_MODEL)), _const_spec((1, D_MODEL))],
        compiler_params=_cparams(),
        name="s5_out",
        **_x_outs(t, tm),
    )(y.reshape(t, D_MODEL), x, w_glu.astype(BF16), ln_g[None], ln_b[None])


def _cast_kernel(*refs):
    o_ref = refs[-1]
    o_ref[...] = jnp.concatenate([r[0].astype(o_ref.dtype) for r in refs[:-1]], axis=-1)


def _cast_bf16(*ws, layer, name):
    _, e, r, c = ws[0].shape
    return pl.pallas_call(
        _cast_kernel,
        grid=(e,),
        in_specs=[pl.BlockSpec((1, 1, r, c), lambda i: (layer, i, 0, 0)) for _ in ws],
        out_specs=pl.BlockSpec((1, r, c * len(ws)), lambda i: (i, 0, 0)),
        out_shape=jax.ShapeDtypeStruct((e, r, c * len(ws)), BF16),
        compiler_params=_cparams(),
        name=name,
    )(*ws)


def _first_index(eq, iota, size, axis):
    return jnp.min(jnp.where(eq, iota, size), axis=axis, keepdims=True)


def _router_kernel(x_ref, w_ref, b_ref, tri_ref, rank_ref, gate_ref, cnt_ref):
    x = x_ref[...]
    w = w_ref[...]
    x_hi = x.astype(BF16)
    x_lo = (x - x_hi.astype(F32)).astype(BF16)
    w_hi = w.astype(BF16)
    w_lo = (w - w_hi.astype(F32)).astype(BF16)
    logits = _dot_nt(w_hi, x_hi) + (_dot_nt(w_hi, x_lo) + _dot_nt(w_lo, x_hi))
    scores = _sigmoid(logits)
    sel = scores + b_ref[...]
    tt = sel.shape[1]
    gsz = N_EXPERTS // N_EXPERT_GROUPS
    grp = sel.reshape(N_EXPERT_GROUPS, gsz, tt)
    wi = lax.broadcasted_iota(jnp.int32, grp.shape, 1)
    m1 = jnp.max(grp, axis=1, keepdims=True)
    i1 = _first_index(grp == m1, wi, gsz, 1)
    m2 = jnp.max(jnp.where(wi == i1, -jnp.inf, grp), axis=1, keepdims=True)
    gs = (m1 + m2).reshape(N_EXPERT_GROUPS, tt)
    gi = lax.broadcasted_iota(jnp.int32, gs.shape, 0)
    gmask = jnp.zeros(gs.shape, jnp.bool_)
    for _ in range(TOPK_GROUPS):
        gm = jnp.max(gs, axis=0, keepdims=True)
        pick = gi == _first_index(gs == gm, gi, N_EXPERT_GROUPS, 0)
        gmask = gmask | pick
        gs = jnp.where(pick, -jnp.inf, gs)
    emask = jnp.broadcast_to(gmask[:, None, :], grp.shape).reshape(N_EXPERTS, tt)
    cand = jnp.where(emask, sel, -jnp.inf)
    ei = lax.broadcasted_iota(jnp.int32, cand.shape, 0)
    chosen = jnp.zeros(cand.shape, jnp.bool_)
    for _ in range(TOP_K):
        cm = jnp.max(cand, axis=0, keepdims=True)
        pick = ei == _first_index(cand == cm, ei, N_EXPERTS, 0)
        chosen = chosen | pick
        cand = jnp.where(pick, -jnp.inf, cand)
    top_w = jnp.where(chosen, scores, 0.0)
    denom = jnp.sum(top_w, axis=0, keepdims=True)
    gate_ref[...] = top_w / denom * ROUTED_SCALE
    chosen_f = chosen.astype(F32)
    rank = _dot(chosen_f.astype(BF16), tri_ref[...])
    rank_ref[...] = jnp.where(chosen, rank, -1.0).astype(jnp.int32)
    cnt = jnp.sum(chosen_f, axis=1, keepdims=True)
    cnt_ref[0] = jnp.broadcast_to(cnt, (N_EXPERTS, LANES)).astype(jnp.int32)


def _moe_kernel(cnt_ref, x_ref, rank_ref, gate_ref, wgu_ref, wd_ref, o_ref, oh_scr, yg_scr):
    j = pl.program_id(0)
    s = pl.program_id(1)

    @pl.when(s == 0)
    def _():
        o_ref[...] = jnp.zeros_like(o_ref)

    tt = x_ref.shape[0]
    slot = lax.broadcasted_iota(jnp.int32, (MOE_SLOTS, tt), 0)

    def expert_chunks(us, c):
        hits = [(rank_ref[u] - c * MOE_SLOTS) == slot for u in us]
        onehots = [hit.astype(F32).astype(BF16) for hit in hits]
        xcs = [_dot(oh, x_ref[...]).astype(BF16) for oh in onehots]
        hgus = [_dot(xc, wgu_ref[u]) for xc, u in zip(xcs, us)]
        hbs = [((h[:, :EXPERT_DIM] * _sigmoid(h[:, :EXPERT_DIM])) * h[:, EXPERT_DIM:]).astype(BF16)
               for h in hgus]
        ys = [_dot(hb, wd_ref[u]) for hb, u in zip(hbs, us)]
        gsels = [jnp.sum(jnp.where(hit, gate_ref[u], 0.0), axis=1, keepdims=True)
                 for hit, u in zip(hits, us)]
        return onehots, [(y * g).astype(BF16) for y, g in zip(ys, gsels)]

    steps_per_group = MOE_GROUP // MOE_UNROLL
    gs = s % steps_per_group
    us = list(range(MOE_UNROLL))
    onehots, ygs = expert_chunks(us, 0)
    for u in us:
        row0 = pl.multiple_of((gs * MOE_UNROLL + u) * MOE_SLOTS, MOE_SLOTS)
        oh_scr[pl.ds(row0, MOE_SLOTS), :] = onehots[u]
        yg_scr[pl.ds(row0, MOE_SLOTS), :] = ygs[u]

    for u in us:
        cnt = cnt_ref[j * N_EXPERTS + s * MOE_UNROLL + u]
        n_chunks = (cnt + MOE_SLOTS - 1) // MOE_SLOTS

        def overflow(c, carry, u=u):
            onehot_c, yg_c = expert_chunks([u], c)
            o_ref[...] += _dot_tn(onehot_c[0], yg_c[0])
            return carry

        lax.fori_loop(1, n_chunks, overflow, 0)

    @pl.when(gs == steps_per_group - 1)
    def _():
        o_ref[...] += _dot_tn(oh_scr[...], yg_scr[...])


def _moe_out_kernel(x_ref, xb_ref, acc_ref, wgu_ref, wd_ref, g_ref, b_ref, o_ref, ob_ref):
    hgu = _dot(xb_ref[...], wgu_ref[...])
    hg = hgu[:, :SHARED_DIM]
    hb = ((hg * _sigmoid(hg)) * hgu[:, SHARED_DIM:]).astype(BF16)
    shared = _dot(hb, wd_ref[...])
    y = DEEPNORM_ALPHA * x_ref[...] + (acc_ref[...] + shared)
    _store_ln(y, g_ref, b_ref, o_ref, ob_ref)


def _moe_layer(x, xb, w_router, router_bias, w_gate, w_up, w_down, sw_gate, sw_up, sw_down,
               ln_g, ln_b, layer):
    t = x.shape[0]
    tt = min(MOE_TILE, t)
    nj = t // tt
    tri = jnp.triu(jnp.ones((tt, tt), F32), 1).astype(BF16)
    rank, gate, cnt = pl.pallas_call(
        _router_kernel,
        grid=(nj,),
        in_specs=[_row_spec(tt, D_MODEL), _const_spec((N_EXPERTS, D_MODEL)),
                  _const_spec((N_EXPERTS, 1)), _const_spec((tt, tt))],
        out_specs=[pl.BlockSpec((N_EXPERTS, tt), lambda i: (0, i)),
                   pl.BlockSpec((N_EXPERTS, tt), lambda i: (0, i)),
                   pl.BlockSpec((1, N_EXPERTS, LANES), lambda i: (i, 0, 0))],
        out_shape=[jax.ShapeDtypeStruct((N_EXPERTS, t), jnp.int32),
                   jax.ShapeDtypeStruct((N_EXPERTS, t), F32),
                   jax.ShapeDtypeStruct((nj, N_EXPERTS, LANES), jnp.int32)],
        compiler_params=_cparams(),
        name="moe_router",
    )(x, w_router.T, router_bias[:, None], tri)
    wgu = _cast_bf16(w_gate, w_up, layer=layer, name="moe_cast_gu")
    wd = _cast_bf16(w_down, layer=layer, name="moe_cast_d")
    acc = pl.pallas_call(
        _moe_kernel,
        grid_spec=pltpu.PrefetchScalarGridSpec(
            num_scalar_prefetch=1,
            grid=(nj, N_EXPERTS // MOE_UNROLL),
            in_specs=[pl.BlockSpec((tt, D_MODEL), lambda j, e, c: (j, 0)),
                      pl.BlockSpec((MOE_UNROLL, 1, tt), lambda j, e, c: (e, 0, j)),
                      pl.BlockSpec((MOE_UNROLL, 1, tt), lambda j, e, c: (e, 0, j)),
                      pl.BlockSpec((MOE_UNROLL, D_MODEL, 2 * EXPERT_DIM), lambda j, e, c: (e, 0, 0)),
                      pl.BlockSpec((MOE_UNROLL, EXPERT_DIM, D_MODEL), lambda j, e, c: (e, 0, 0))],
            out_specs=pl.BlockSpec((tt, D_MODEL), lambda j, e, c: (j, 0)),
            scratch_shapes=[pltpu.VMEM((MOE_GROUP * MOE_SLOTS, tt), BF16),
                            pltpu.VMEM((MOE_GROUP * MOE_SLOTS, D_MODEL), BF16)],
        ),
        out_shape=jax.ShapeDtypeStruct((t, D_MODEL), F32),
        compiler_params=_cparams(),
        name="moe_experts",
    )(cnt[:, :, 0].reshape(-1), xb, rank.reshape(N_EXPERTS, 1, t), gate.reshape(N_EXPERTS, 1, t),
      wgu, wd)
    tm = min(ROW_TILE, t)
    sgu = jnp.concatenate([sw_gate, sw_up], axis=1).astype(BF16)
    return pl.pallas_call(
        _moe_out_kernel,
        grid=(t // tm,),
        in_specs=[_row_spec(tm, D_MODEL), _row_spec(tm, D_MODEL), _row_spec(tm, D_MODEL),
                  _const_spec((D_MODEL, 2 * SHARED_DIM)), _const_spec((SHARED_DIM, D_MODEL)),
                  _const_spec((1, D_MODEL)), _const_spec((1, D_MODEL))],
        compiler_params=_cparams(),
        name="moe_out",
        **_x_outs(t, tm),
    )(x, xb, acc, sgu, sw_down.astype(BF16), ln_g[None], ln_b[None])


def kernel(x, attn_w_in, attn_w_out, ret_w_in, ret_gn_g, ret_gn_b, ret_w_out, ssm_a_re, ssm_a_im, ssm_b_re, ssm_b_im, ssm_c_re, ssm_c_im, ssm_d, ssm_log_dt, ssm_w_glu, moe_w_router, moe_router_bias, moe_w_gate, moe_w_up, moe_w_down, shared_w_gate, shared_w_up, shared_w_down, ln_g, ln_b):
    bsz, seq, d = x.shape
    depth = ln_g.shape[0]
    xf = x.reshape(bsz * seq, d)
    xb = xf.astype(BF16)
    for i in range(depth):
        j = i // N_MIXERS
        kind = i % N_MIXERS
        if kind == 0:
            xf, xb = _attention_layer(xf, xb, attn_w_in[j], attn_w_out[j], ln_g[i, 0], ln_b[i, 0],
                                      bsz, seq)
        elif kind == 1:
            xf, xb = _retention_layer(xf, xb, ret_w_in[j], ret_gn_g[j], ret_gn_b[j], ret_w_out[j],
                                      ln_g[i, 0], ln_b[i, 0], bsz, seq)
        else:
            xf, xb = _s5_layer(xf, ssm_a_re[j], ssm_a_im[j], ssm_b_re[j], ssm_b_im[j], ssm_c_re[j],
                               ssm_c_im[j], ssm_d[j], ssm_log_dt[j], ssm_w_glu[j],
                               ln_g[i, 0], ln_b[i, 0], bsz, seq)
        xf, xb = _moe_layer(xf, xb, moe_w_router[i], moe_router_bias[i], moe_w_gate, moe_w_up,
                            moe_w_down, shared_w_gate[i], shared_w_up[i], shared_w_down[i],
                            ln_g[i, 1], ln_b[i, 1], i)
    return xf.reshape(bsz, seq, d)
```

```python
import functools
import math

import jax
import jax.numpy as jnp
from jax import lax
from jax.experimental import pallas as pl
from jax.experimental.pallas import tpu as pltpu

F32 = jnp.float32
BF16 = jnp.bfloat16

D_MODEL = 1024
DEPTH = 4
N_MIXERS = 3
ATTN_GROUPS = ((128, 1), (512, 4), (2048, 16))
ATTN_HEADS = 8
ATTN_HEAD_DIM = D_MODEL // ATTN_HEADS
ATTN_BLOCK = 128
RET_HEADS = 4
RET_KEY_DIM = D_MODEL // RET_HEADS
RET_VAL_DIM = 2 * RET_KEY_DIM
RET_CHUNK = 128
SSM_GROUP = 16
SSM_GROUPS = D_MODEL // SSM_GROUP
SSM_STATE = 64
N_EXPERTS = 64
TOP_K = 8
N_EXPERT_GROUPS = 8
TOPK_GROUPS = 4
EXPERT_DIM = 256
SHARED_DIM = 256
ROUTED_SCALE = 2.5
DEEPNORM_ALPHA = (2 * DEPTH) ** 0.25
LN_EPS = 1e-5

LANES = 128
SUBLANES = 8
MASK_NEG = -1e30
VMEM_LIMIT = 56 * 1024 * 1024

ROW_TILE = 512
PROJ_TM = 2048
PROJ_TN = 1024
MOE_TILE = 1024
MOE_SLOTS = 160
MOE_GROUP = 16
MOE_UNROLL = 4
SSM_TIME = 512
SSM_COLS = 128


def _cparams():
    return pltpu.CompilerParams(vmem_limit_bytes=VMEM_LIMIT)


def _dot(a, b):
    return jnp.dot(a, b, preferred_element_type=F32)


def _dot_nt(a, b):
    return lax.dot_general(a, b, (((1,), (1,)), ((), ())), preferred_element_type=F32)


def _dot_tn(a, b):
    return lax.dot_general(a, b, (((0,), (0,)), ((), ())), preferred_element_type=F32)


def _sigmoid(x):
    return 1.0 / (1.0 + jnp.exp(-x))


def _layer_norm(y, g, b):
    mu = jnp.mean(y, axis=-1, keepdims=True)
    yc = y - mu
    var = jnp.mean(yc * yc, axis=-1, keepdims=True)
    return yc * lax.rsqrt(var + LN_EPS) * g + b


def _store_ln(y, g_ref, b_ref, o_ref, ob_ref):
    out = _layer_norm(y, g_ref[...], b_ref[...])
    o_ref[...] = out
    ob_ref[...] = out.astype(BF16)


def _proj_kernel(x_ref, w_ref, o_ref):
    o_ref[...] = _dot(x_ref[...], w_ref[...]).astype(o_ref.dtype)


def _proj(xb, w, out_dtype, name):
    m, k = xb.shape
    n = w.shape[1]
    tm = min(PROJ_TM, m)
    tn = min(PROJ_TN, n)
    return pl.pallas_call(
        _proj_kernel,
        grid=(m // tm, n // tn),
        in_specs=[pl.BlockSpec((tm, k), lambda i, j: (i, 0)),
                  pl.BlockSpec((k, tn), lambda i, j: (0, j))],
        out_specs=pl.BlockSpec((tm, tn), lambda i, j: (i, j)),
        out_shape=jax.ShapeDtypeStruct((m, n), out_dtype),
        compiler_params=_cparams(),
        name=name,
    )(xb, w)


def _proj_perm_kernel(x_ref, w_ref, o_ref, xs_ref, col_ref, *, dil):
    rows = x_ref.shape[0] // dil

    @pl.when(pl.program_id(1) == 0)
    def _():
        for c in range(x_ref.shape[1] // LANES):
            cs = slice(c * LANES, (c + 1) * LANES)
            col_ref[...] = x_ref[:, cs]
            for r in range(dil):
                xs_ref[r * rows:(r + 1) * rows, cs] = col_ref[pl.ds(r, rows, stride=dil), :].astype(BF16)

    res = _dot(xs_ref[...], w_ref[...])
    o_ref[0] = res.reshape(dil, rows, res.shape[1]).astype(o_ref.dtype)


def _proj_perm(xf, w, dil, bsz, seq, name):
    k = xf.shape[1]
    n = w.shape[1]
    tm = min(PROJ_TM, seq)
    tn = min(PROJ_TN, n)
    assert seq % tm == 0 and tm % (dil * 2 * SUBLANES) == 0
    per_b = seq // tm
    rows = tm // dil
    return pl.pallas_call(
        functools.partial(_proj_perm_kernel, dil=dil),
        grid=(bsz * per_b, n // tn),
        in_specs=[pl.BlockSpec((tm, k), lambda i, j: (i, 0)),
                  pl.BlockSpec((k, tn), lambda i, j: (0, j))],
        out_specs=pl.BlockSpec((1, dil, rows, tn), lambda i, j: (i // per_b, 0, i % per_b, j)),
        out_shape=jax.ShapeDtypeStruct((bsz, dil, seq // dil, n), BF16),
        scratch_shapes=[pltpu.VMEM((tm, k), BF16), pltpu.VMEM((tm, LANES), F32)],
        compiler_params=_cparams(),
        name=name,
    )(xf, w)


def _attn_kernel(q_ref, kc_ref, vc_ref, bp_ref, bc_ref, o_ref, lse_ref,
                 o_scr, lse_scr, mix_scr, kp_scr, vp_scr, *, dil):
    n = pl.program_id(1)
    r = pl.program_id(2)

    @pl.when(n == 0)
    def _():
        kp_scr[r] = jnp.zeros(kp_scr.shape[1:], kp_scr.dtype)
        vp_scr[r] = jnp.zeros(vp_scr.shape[1:], vp_scr.dtype)

    first = jnp.where(n == 0, MASK_NEG, 0.0).astype(F32)
    scale = ATTN_HEAD_DIM ** -0.5
    lane = lax.broadcasted_iota(jnp.int32, (ATTN_BLOCK, LANES), 1)
    lse_tile = jnp.zeros((ATTN_BLOCK, LANES), F32)
    heads = range(ATTN_HEADS)
    sls = [slice(h * ATTN_HEAD_DIM, (h + 1) * ATTN_HEAD_DIM) for h in heads]
    sps = [_dot_nt(q_ref[0, 0, :, sls[h]], kp_scr[r, :, sls[h]]) * scale + (bp_ref[h] + first)
           for h in heads]
    scs = [_dot_nt(q_ref[0, 0, :, sls[h]], kc_ref[0, 0, :, sls[h]]) * scale + bc_ref[h]
           for h in heads]
    ms = [jnp.max(jnp.maximum(sps[h], scs[h]), axis=1, keepdims=True) for h in heads]
    pps = [jnp.exp(sps[h] - ms[h]) for h in heads]
    pcs = [jnp.exp(scs[h] - ms[h]) for h in heads]
    ls = [jnp.sum(pps[h] + pcs[h], axis=1, keepdims=True) for h in heads]
    for h in heads:
        o = (_dot(pps[h].astype(BF16), vp_scr[r, :, sls[h]])
             + _dot(pcs[h].astype(BF16), vc_ref[0, 0, :, sls[h]]))
        o_scr[h, r] = o / ls[h]
        lse_tile = jnp.where(lane == h, ms[h] + jnp.log(ls[h]), lse_tile)
    lse_scr[r] = lse_tile
    kp_scr[r] = kc_ref[0, 0]
    vp_scr[r] = vc_ref[0, 0]

    @pl.when(r == dil - 1)
    def _():
        for rr in range(dil):
            lse_ref[0, pl.ds(rr, ATTN_BLOCK, stride=dil), :] = lse_scr[rr]
        for h in range(ATTN_HEADS):
            for rr in range(dil):
                mix_scr[pl.ds(rr, ATTN_BLOCK, stride=dil), :] = o_scr[h, rr]
            o_ref[0, :, h * ATTN_HEAD_DIM:(h + 1) * ATTN_HEAD_DIM] = mix_scr[...]


def _attn_bias(window, dil):
    steps = window // dil
    assert steps <= ATTN_BLOCK
    qi = jnp.arange(ATTN_BLOCK)[:, None]
    kj = jnp.arange(ATTN_BLOCK)[None, :]
    slopes = 2.0 ** (-8.0 * (jnp.arange(ATTN_HEADS, dtype=F32) + 1.0) / ATTN_HEADS)

    def bias(step):
        valid = (step >= 0) & (step <= steps)
        dist = (step * dil).astype(F32)
        return jnp.where(valid[None], -(slopes[:, None, None] * dist[None]), MASK_NEG).astype(F32)

    return bias(qi - kj + ATTN_BLOCK), bias(qi - kj)


def _attn_group(qkv, gi, window, dil):
    b, _, ls, width = qkv.shape
    s = ls * dil
    hd = ATTN_HEADS * ATTN_HEAD_DIM
    assert ls % ATTN_BLOCK == 0 and width == 3 * hd
    nb = ls // ATTN_BLOCK
    bias_prev, bias_cur = _attn_bias(window, dil)

    def cur(c):
        return pl.BlockSpec((1, 1, ATTN_BLOCK, hd), lambda bi, n, r: (bi, r, n, c))

    const = pl.BlockSpec((ATTN_HEADS, ATTN_BLOCK, ATTN_BLOCK), lambda bi, n, r: (0, 0, 0))
    o, lse = pl.pallas_call(
        functools.partial(_attn_kernel, dil=dil),
        grid=(b, nb, dil),
        in_specs=[cur(0), cur(1), cur(2), const, const],
        out_specs=[pl.BlockSpec((1, ATTN_BLOCK * dil, hd), lambda bi, n, r: (bi, n, 0)),
                   pl.BlockSpec((1, ATTN_BLOCK * dil, LANES), lambda bi, n, r: (bi, n, 0))],
        out_shape=[jax.ShapeDtypeStruct((b, s, hd), F32),
                   jax.ShapeDtypeStruct((b, s, LANES), F32)],
        scratch_shapes=[pltpu.VMEM((ATTN_HEADS, dil, ATTN_BLOCK, ATTN_HEAD_DIM), F32),
                        pltpu.VMEM((dil, ATTN_BLOCK, LANES), F32),
                        pltpu.VMEM((dil * ATTN_BLOCK, ATTN_HEAD_DIM), F32),
                        pltpu.VMEM((dil, ATTN_BLOCK, hd), BF16),
                        pltpu.VMEM((dil, ATTN_BLOCK, hd), BF16)],
        compiler_params=_cparams(),
        name=f"attn_g{gi}",
    )(qkv, qkv, qkv, bias_prev, bias_cur)
    return o.reshape(b * s, hd), lse.reshape(b * s, LANES)


def _attn_out_kernel(o1_ref, o2_ref, o3_ref, l1_ref, l2_ref, l3_ref, x_ref, w_ref, g_ref, b_ref,
                     o_ref, ob_ref):
    l1, l2, l3 = l1_ref[...], l2_ref[...], l3_ref[...]
    mx = jnp.maximum(jnp.maximum(l1, l2), l3)
    e1, e2, e3 = jnp.exp(l1 - mx), jnp.exp(l2 - mx), jnp.exp(l3 - mx)
    den = e1 + e2 + e3
    w1, w2, w3 = e1 / den, e2 / den, e3 / den
    cols = []
    for h in range(ATTN_HEADS):
        sl = slice(h * ATTN_HEAD_DIM, (h + 1) * ATTN_HEAD_DIM)
        oh = (w1[:, h:h + 1] * o1_ref[:, sl] + w2[:, h:h + 1] * o2_ref[:, sl]
              + w3[:, h:h + 1] * o3_ref[:, sl])
        cols.append(oh.astype(BF16))
    o = jnp.concatenate(cols, axis=1)
    y = _dot(o, w_ref[...]) + DEEPNORM_ALPHA * x_ref[...]
    _store_ln(y, g_ref, b_ref, o_ref, ob_ref)


def _row_spec(tm, width):
    return pl.BlockSpec((tm, width), lambda i: (i, 0))


def _const_spec(shape):
    return pl.BlockSpec(shape, lambda i: tuple(0 for _ in shape))


def _x_outs(t, tm):
    return dict(
        out_specs=[_row_spec(tm, D_MODEL), _row_spec(tm, D_MODEL)],
        out_shape=[jax.ShapeDtypeStruct((t, D_MODEL), F32), jax.ShapeDtypeStruct((t, D_MODEL), BF16)],
    )


def _attention_layer(x, xb, w_in, w_out, ln_g, ln_b, bsz, seq):
    t = x.shape[0]
    hd3 = 3 * ATTN_HEADS * ATTN_HEAD_DIM
    wb = w_in.astype(BF16)
    outs = []
    for gi, (window, dil) in enumerate(ATTN_GROUPS):
        qkv = _proj_perm(x, wb[:, gi * hd3:(gi + 1) * hd3], dil, bsz, seq, f"attn_proj_g{gi}")
        outs.append(_attn_group(qkv, gi, window, dil))
    tm = min(ROW_TILE, t)
    hd = ATTN_HEADS * ATTN_HEAD_DIM
    return pl.pallas_call(
        _attn_out_kernel,
        grid=(t // tm,),
        in_specs=[_row_spec(tm, hd)] * 3 + [_row_spec(tm, LANES)] * 3
                 + [_row_spec(tm, D_MODEL), _const_spec((hd, D_MODEL)),
                    _const_spec((1, D_MODEL)), _const_spec((1, D_MODEL))],
        compiler_params=_cparams(),
        name="attn_out",
        **_x_outs(t, tm),
    )(outs[0][0], outs[1][0], outs[2][0], outs[0][1], outs[1][1], outs[2][1],
      x, w_out.astype(BF16), ln_g[None], ln_b[None])


def _ret_kernel(q_ref, k_ref, v_ref, g_ref, dec_ref, xi_ref, zeta_ref, cd_ref, gg_ref, gb_ref,
                y_ref, state_ref):
    n = pl.program_id(1)

    @pl.when(n == 0)
    def _():
        state_ref[...] = jnp.zeros_like(state_ref)

    kscale = RET_KEY_DIM ** -0.5
    dk, dv = RET_KEY_DIM, RET_VAL_DIM
    hs = range(RET_HEADS)
    qs = [q_ref[0, :, h * dk:(h + 1) * dk] for h in hs]
    ks = [k_ref[0, :, h * dk:(h + 1) * dk] for h in hs]
    vs = [v_ref[0, :, h * dv:(h + 1) * dv] for h in hs]
    scores = [_dot_nt(qs[h], ks[h]) * kscale * dec_ref[h] for h in hs]
    inners = [_dot(scores[h].astype(BF16), vs[h]) for h in hs]
    sts = [state_ref[h] for h in hs]
    crosses = [_dot((qs[h].astype(F32) * xi_ref[h]).astype(BF16), sts[h].astype(BF16)) for h in hs]
    kzs = [(ks[h].astype(F32) * kscale * zeta_ref[h]).astype(BF16) for h in hs]
    for h in hs:
        state_ref[h] = cd_ref[h, 0:1, 0:1] * sts[h] + _dot_tn(kzs[h], vs[h])
    for h in hs:
        sl = slice(h * dv, (h + 1) * dv)
        y = inners[h] + crosses[h]
        mu = jnp.mean(y, axis=1, keepdims=True)
        yc = y - mu
        var = jnp.mean(yc * yc, axis=1, keepdims=True)
        yn = yc * lax.rsqrt(var + LN_EPS) * gg_ref[:, sl] + gb_ref[:, sl]
        g = g_ref[0, :, sl]
        y_ref[0, :, sl] = ((g * _sigmoid(g)) * yn).astype(y_ref.dtype)


def _ret_out_kernel(y_ref, x_ref, w_ref, g_ref, b_ref, o_ref, ob_ref):
    y = _dot(y_ref[...], w_ref[...]) + DEEPNORM_ALPHA * x_ref[...]
    _store_ln(y, g_ref, b_ref, o_ref, ob_ref)


def _retention_layer(x, xb, w_in, gn_g, gn_b, w_out, ln_g, ln_b, bsz, seq):
    t = x.shape[0]
    h, dk, dv, c = RET_HEADS, RET_KEY_DIM, RET_VAL_DIM, RET_CHUNK
    nqkv = 2 * h * dk + h * dv
    wb = w_in.astype(BF16)
    qkv = _proj(xb, wb[:, :nqkv], BF16, "ret_proj_qkv").reshape(bsz, seq, nqkv)
    gate = _proj(xb, wb[:, nqkv:], F32, "ret_proj_gate").reshape(bsz, seq, h * dv)
    log_gamma = jnp.log(1.0 - 2.0 ** (-5.0 - jnp.arange(h, dtype=F32)))
    pos = jnp.arange(c, dtype=F32)
    rel = pos[:, None] - pos[None, :]
    decay = jnp.where(rel >= 0, jnp.exp(log_gamma[:, None, None] * jnp.maximum(rel, 0.0)), 0.0)
    xi = jnp.exp(log_gamma[:, None] * (pos[None, :] + 1.0))
    zeta = jnp.exp(log_gamma[:, None] * (c - 1.0 - pos[None, :]))
    xi = jnp.broadcast_to(xi[:, :, None], (h, c, dk))
    zeta = jnp.broadcast_to(zeta[:, :, None], (h, c, dk))
    cd = jnp.broadcast_to(jnp.exp(log_gamma * c)[:, None, None], (h, SUBLANES, LANES))
    assert h * dv == 2 * h * dk
    full = lambda *shape: pl.BlockSpec(shape, lambda bi, n: tuple(0 for _ in shape))
    y = pl.pallas_call(
        _ret_kernel,
        grid=(bsz, seq // c),
        in_specs=[pl.BlockSpec((1, c, h * dk), lambda bi, n: (bi, n, 0)),
                  pl.BlockSpec((1, c, h * dk), lambda bi, n: (bi, n, 1)),
                  pl.BlockSpec((1, c, h * dv), lambda bi, n: (bi, n, 1)),
                  pl.BlockSpec((1, c, h * dv), lambda bi, n: (bi, n, 0)),
                  full(h, c, c), full(h, c, dk), full(h, c, dk), full(h, SUBLANES, LANES),
                  full(1, h * dv), full(1, h * dv)],
        out_specs=pl.BlockSpec((1, c, h * dv), lambda bi, n: (bi, n, 0)),
        out_shape=jax.ShapeDtypeStruct((bsz, seq, h * dv), BF16),
        scratch_shapes=[pltpu.VMEM((h, dk, dv), F32)],
        compiler_params=_cparams(),
        name="ret_core",
    )(qkv, qkv, qkv, gate, decay, xi, zeta, cd, gn_g[None], gn_b[None])
    tm = min(ROW_TILE, t)
    return pl.pallas_call(
        _ret_out_kernel,
        grid=(t // tm,),
        in_specs=[_row_spec(tm, h * dv), _row_spec(tm, D_MODEL), _const_spec((h * dv, D_MODEL)),
                  _const_spec((1, D_MODEL)), _const_spec((1, D_MODEL))],
        compiler_params=_cparams(),
        name="ret_out",
        **_x_outs(t, tm),
    )(y.reshape(t, h * dv), x, w_out.astype(BF16), ln_g[None], ln_b[None])


def _gelu_tanh(y):
    return 0.5 * y * (1.0 + jnp.tanh(math.sqrt(2.0 / math.pi) * (y + 0.044715 * (y * y * y))))


def _s5_kernel(x_ref, wb_ref, wc_ref, pr_ref, pi_ref, d_ref, y_ref,
               cr_ref, ci_ref, xs_ref, hr_scr, hi_scr, yt_ref):
    n = pl.program_id(2)

    @pl.when(n == 0)
    def _():
        cr_ref[...] = jnp.zeros_like(cr_ref)
        ci_ref[...] = jnp.zeros_like(ci_ref)

    nch = pr_ref.shape[2]
    seg = SUBLANES
    tau_n = SSM_TIME // seg
    for s in range(seg):
        xs_ref[pl.ds(s, tau_n, stride=seg), :] = x_ref[0, s * tau_n:(s + 1) * tau_n, :]
    u = xs_ref[...]
    bu = _dot(u.astype(BF16), wb_ref[0])
    re = bu[:, :nch].reshape(tau_n, seg, nch)
    im = bu[:, nch:].reshape(tau_n, seg, nch)
    a_r = jnp.broadcast_to(pr_ref[0, 0:1, :], (seg, nch))
    a_i = jnp.broadcast_to(pi_ref[0, 0:1, :], (seg, nch))
    hr = jnp.zeros((seg, nch), F32)
    hi = jnp.zeros((seg, nch), F32)
    for t in range(tau_n):
        hr, hi = a_r * hr - a_i * hi + re[t], a_r * hi + a_i * hr + im[t]
        hr_scr[t] = hr
        hi_scr[t] = hi
    q_r = pr_ref[0, tau_n - 1:tau_n, :]
    q_i = pi_ref[0, tau_n - 1:tau_n, :]
    c_r = cr_ref[0:1, :]
    c_i = ci_ref[0:1, :]
    row = lax.broadcasted_iota(jnp.int32, (seg, nch), 0)
    in_r = jnp.zeros((seg, nch), F32)
    in_i = jnp.zeros((seg, nch), F32)
    for s in range(seg):
        in_r = jnp.where(row == s, c_r, in_r)
        in_i = jnp.where(row == s, c_i, in_i)
        c_r, c_i = (hr[s:s + 1, :] + (q_r * c_r - q_i * c_i),
                    hi[s:s + 1, :] + (q_r * c_i + q_i * c_r))
    cr_ref[...] = jnp.broadcast_to(c_r, cr_ref.shape)
    ci_ref[...] = jnp.broadcast_to(c_i, ci_ref.shape)
    hre, him = [], []
    for t in range(tau_n):
        p_r = pr_ref[0, t:t + 1, :]
        p_i = pi_ref[0, t:t + 1, :]
        hre.append(hr_scr[t] + (p_r * in_r - p_i * in_i))
        him.append(hi_scr[t] + (p_r * in_i + p_i * in_r))
    hcat = jnp.concatenate([jnp.concatenate(hre, axis=0), jnp.concatenate(him, axis=0)], axis=1)
    y = _dot(hcat.astype(BF16), wc_ref[0]) + d_ref[...] * u
    yt_ref[...] = _gelu_tanh(y)
    for s in range(seg):
        y_ref[0, s * tau_n:(s + 1) * tau_n, :] = yt_ref[pl.ds(s, tau_n, stride=seg), :].astype(y_ref.dtype)


def _s5_out_kernel(y_ref, x_ref, w_ref, g_ref, b_ref, o_ref, ob_ref):
    vg = _dot(y_ref[...], w_ref[...])
    val = vg[:, :D_MODEL]
    gate = vg[:, D_MODEL:]
    y = val * _sigmoid(gate) + DEEPNORM_ALPHA * x_ref[...]
    _store_ln(y, g_ref, b_ref, o_ref, ob_ref)


def _block_diag(m):
    nb, g, r, c = m.shape
    eye = jnp.eye(g, dtype=m.dtype)
    return (m[:, :, :, None, :] * eye[None, :, None, :, None]).reshape(nb, g * r, g * c)


def _s5_layer(x, a_re, a_im, b_re, b_im, c_re, c_im, d_skip, log_dt, w_glu, ln_g, ln_b, bsz, seq):
    t = x.shape[0]
    gpb = SSM_COLS // SSM_GROUP
    ncb = D_MODEL // SSM_COLS
    nch = gpb * SSM_STATE
    dt = jnp.exp(log_dt)[:, None]
    mag = jnp.exp(dt * a_re)
    ab_re = mag * jnp.cos(dt * a_im)
    ab_im = mag * jnp.sin(dt * a_im)
    den = a_re * a_re + a_im * a_im
    f_re = ((ab_re - 1.0) * a_re + ab_im * a_im) / den
    f_im = (ab_im * a_re - (ab_re - 1.0) * a_im) / den
    bb_re = f_re[..., None] * b_re - f_im[..., None] * b_im
    bb_im = f_re[..., None] * b_im + f_im[..., None] * b_re
    tb = lambda m: jnp.swapaxes(m, 1, 2).reshape(ncb, gpb, SSM_GROUP, SSM_STATE)
    w_b = jnp.concatenate([_block_diag(tb(bb_re)), _block_diag(tb(bb_im))], axis=2).astype(BF16)
    tc = lambda m: jnp.swapaxes(m, 1, 2).reshape(ncb, gpb, SSM_STATE, SSM_GROUP)
    w_c = jnp.concatenate([_block_diag(tc(c_re)), -_block_diag(tc(c_im))], axis=1).astype(BF16)
    tau_n = SSM_TIME // SUBLANES
    assert tau_n & (tau_n - 1) == 0
    pr, pi = ab_re[None], ab_im[None]
    while pr.shape[0] < tau_n:
        lr, li = pr[-1], pi[-1]
        pr, pi = (jnp.concatenate([pr, pr * lr - pi * li], axis=0),
                  jnp.concatenate([pi, pr * li + pi * lr], axis=0))
    pw_re = pr.reshape(tau_n, ncb, nch).transpose(1, 0, 2)
    pw_im = pi.reshape(tau_n, ncb, nch).transpose(1, 0, 2)
    x3 = x.reshape(bsz, seq, D_MODEL)
    blk = lambda shape: pl.BlockSpec((1,) + shape, lambda bi, cb, n: (cb, 0, 0))
    y = pl.pallas_call(
        _s5_kernel,
        grid=(bsz, ncb, seq // SSM_TIME),
        in_specs=[pl.BlockSpec((1, SSM_TIME, SSM_COLS), lambda bi, cb, n: (bi, n, cb)),
                  blk((SSM_COLS, 2 * nch)), blk((2 * nch, SSM_COLS)),
                  blk((tau_n, nch)), blk((tau_n, nch)),
                  pl.BlockSpec((1, SSM_COLS), lambda bi, cb, n: (0, cb))],
        out_specs=pl.BlockSpec((1, SSM_TIME, SSM_COLS), lambda bi, cb, n: (bi, n, cb)),
        out_shape=jax.ShapeDtypeStruct((bsz, seq, D_MODEL), BF16),
        scratch_shapes=[pltpu.VMEM((SUBLANES, nch), F32), pltpu.VMEM((SUBLANES, nch), F32),
                        pltpu.VMEM((SSM_TIME, SSM_COLS), F32),
                        pltpu.VMEM((tau_n, SUBLANES, nch), F32),
                        pltpu.VMEM((tau_n, SUBLANES, nch), F32),
                        pltpu.VMEM((SSM_TIME, SSM_COLS), F32)],
        compiler_params=_cparams(),
        name="s5_core",
    )(x3, w_b, w_c, pw_re, pw_im, d_skip.reshape(1, D_MODEL))
    tm = min(ROW_TILE, t)
    return pl.pallas_call(
        _s5_out_kernel,
        grid=(t // tm,),
        in_specs=[_row_spec(tm, D_MODEL), _row_spec(tm, D_MODEL), _const_spec((D_MODEL, 2 * D_MODEL)),
                  _const_spec((1, D_MODEL)), _const_spec((1, D_MODEL))],
        compiler_params=_cparams(),
        name="s5_out",
        **_x_outs(t, tm),
    )(y.reshape(t, D_MODEL), x, w_glu.astype(BF16), ln_g[None], ln_b[None])


def _cast_kernel(*refs):
    o_ref = refs[-1]
    o_ref[...] = jnp.concatenate([r[0].astype(o_ref.dtype) for r in refs[:-1]], axis=-1)


def _cast_bf16(*ws, layer, name):
    _, e, r, c = ws[0].shape
    return pl.pallas_call(
        _cast_kernel,
        grid=(e,),
        in_specs=[pl.BlockSpec((1, 1, r, c), lambda i: (layer, i, 0, 0)) for _ in ws],
        out_specs=pl.BlockSpec((1, r, c * len(ws)), lambda i: (i, 0, 0)),
        out_shape=jax.ShapeDtypeStruct((e, r, c * len(ws)), BF16),
        compiler_params=_cparams(),
        name=name,
    )(*ws)


def _first_index(eq, iota, size, axis):
    return jnp.min(jnp.where(eq, iota, size), axis=axis, keepdims=True)


def _router_kernel(x_ref, w_ref, b_ref, tri_ref, rank_ref, gate_ref, cnt_ref):
    x = x_ref[...]
    w = w_ref[...]
    x_hi = x.astype(BF16)
    x_lo = (x - x_hi.astype(F32)).astype(BF16)
    w_hi = w.astype(BF16)
    w_lo = (w - w_hi.astype(F32)).astype(BF16)
    logits = _dot_nt(w_hi, x_hi) + (_dot_nt(w_hi, x_lo) + _dot_nt(w_lo, x_hi))
    scores = _sigmoid(logits)
    sel = scores + b_ref[...]
    tt = sel.shape[1]
    gsz = N_EXPERTS // N_EXPERT_GROUPS
    grp = sel.reshape(N_EXPERT_GROUPS, gsz, tt)
    wi = lax.broadcasted_iota(jnp.int32, grp.shape, 1)
    m1 = jnp.max(grp, axis=1, keepdims=True)
    i1 = _first_index(grp == m1, wi, gsz, 1)
    m2 = jnp.max(jnp.where(wi == i1, -jnp.inf, grp), axis=1, keepdims=True)
    gs = (m1 + m2).reshape(N_EXPERT_GROUPS, tt)
    gi = lax.broadcasted_iota(jnp.int32, gs.shape, 0)
    gmask = jnp.zeros(gs.shape, jnp.bool_)
    for _ in range(TOPK_GROUPS):
        gm = jnp.max(gs, axis=0, keepdims=True)
        pick = gi == _first_index(gs == gm, gi, N_EXPERT_GROUPS, 0)
        gmask = gmask | pick
        gs = jnp.where(pick, -jnp.inf, gs)
    emask = jnp.broadcast_to(gmask[:, None, :], grp.shape).reshape(N_EXPERTS, tt)
    cand = jnp.where(emask, sel, -jnp.inf)
    ei = lax.broadcasted_iota(jnp.int32, cand.shape, 0)
    chosen = jnp.zeros(cand.shape, jnp.bool_)
    for _ in range(TOP_K):
        cm = jnp.max(cand, axis=0, keepdims=True)
        pick = ei == _first_index(cand == cm, ei, N_EXPERTS, 0)
        chosen = chosen | pick
        cand = jnp.where(pick, -jnp.inf, cand)
    top_w = jnp.where(chosen, scores, 0.0)
    denom = jnp.sum(top_w, axis=0, keepdims=True)
    gate_ref[...] = top_w / denom * ROUTED_SCALE
    chosen_f = chosen.astype(F32)
    rank = _dot(chosen_f.astype(BF16), tri_ref[...])
    rank_ref[...] = jnp.where(chosen, rank, -1.0).astype(jnp.int32)
    cnt = jnp.sum(chosen_f, axis=1, keepdims=True)
    cnt_ref[0] = jnp.broadcast_to(cnt, (N_EXPERTS, LANES)).astype(jnp.int32)


def _moe_kernel(cnt_ref, x_ref, rank_ref, gate_ref, wgu_ref, wd_ref, o_ref, oh_scr, yg_scr):
    j = pl.program_id(0)
    s = pl.program_id(1)

    @pl.when(s == 0)
    def _():
        o_ref[...] = jnp.zeros_like(o_ref)

    tt = x_ref.shape[0]
    slot = lax.broadcasted_iota(jnp.int32, (MOE_SLOTS, tt), 0)

    def expert_chunks(us, c):
        hits = [(rank_ref[u] - c * MOE_SLOTS) == slot for u in us]
        onehots = [hit.astype(F32).astype(BF16) for hit in hits]
        xcs = [_dot(oh, x_ref[...]).astype(BF16) for oh in onehots]
        hgus = [_dot(xc, wgu_ref[u]) for xc, u in zip(xcs, us)]
        hbs = [((h[:, :EXPERT_DIM] * _sigmoid(h[:, :EXPERT_DIM])) * h[:, EXPERT_DIM:]).astype(BF16)
               for h in hgus]
        ys = [_dot(hb, wd_ref[u]) for hb, u in zip(hbs, us)]
        gsels = [jnp.sum(jnp.where(hit, gate_ref[u], 0.0), axis=1, keepdims=True)
                 for hit, u in zip(hits, us)]
        return onehots, [(y * g).astype(BF16) for y, g in zip(ys, gsels)]

    steps_per_group = MOE_GROUP // MOE_UNROLL
    gs = s % steps_per_group
    us = list(range(MOE_UNROLL))
    onehots, ygs = expert_chunks(us, 0)
    for u in us:
        row0 = pl.multiple_of((gs * MOE_UNROLL + u) * MOE_SLOTS, MOE_SLOTS)
        oh_scr[pl.ds(row0, MOE_SLOTS), :] = onehots[u]
        yg_scr[pl.ds(row0, MOE_SLOTS), :] = ygs[u]

    for u in us:
        cnt = cnt_ref[j * N_EXPERTS + s * MOE_UNROLL + u]
        n_chunks = (cnt + MOE_SLOTS - 1) // MOE_SLOTS

        def overflow(c, carry, u=u):
            onehot_c, yg_c = expert_chunks([u], c)
            o_ref[...] += _dot_tn(onehot_c[0], yg_c[0])
            return carry

        lax.fori_loop(1, n_chunks, overflow, 0)

    @pl.when(gs == steps_per_group - 1)
    def _():
        o_ref[...] += _dot_tn(oh_scr[...], yg_scr[...])


def _moe_out_kernel(x_ref, xb_ref, acc_ref, wgu_ref, wd_ref, g_ref, b_ref, o_ref, ob_ref):
    hgu = _dot(xb_ref[...], wgu_ref[...])
    hg = hgu[:, :SHARED_DIM]
    hb = ((hg * _sigmoid(hg)) * hgu[:, SHARED_DIM:]).astype(BF16)
    shared = _dot(hb, wd_ref[...])
    y = DEEPNORM_ALPHA * x_ref[...] + (acc_ref[...] + shared)
    _store_ln(y, g_ref, b_ref, o_ref, ob_ref)


def _moe_layer(x, xb, w_router, router_bias, w_gate, w_up, w_down, sw_gate, sw_up, sw_down,
               ln_g, ln_b, layer):
    t = x.shape[0]
    tt = min(MOE_TILE, t)
    nj = t // tt
    tri = jnp.triu(jnp.ones((tt, tt), F32), 1).astype(BF16)
    rank, gate, cnt = pl.pallas_call(
        _router_kernel,
        grid=(nj,),
        in_specs=[_row_spec(tt, D_MODEL), _const_spec((N_EXPERTS, D_MODEL)),
                  _const_spec((N_EXPERTS, 1)), _const_spec((tt, tt))],
        out_specs=[pl.BlockSpec((N_EXPERTS, tt), lambda i: (0, i)),
                   pl.BlockSpec((N_EXPERTS, tt), lambda i: (0, i)),
                   pl.BlockSpec((1, N_EXPERTS, LANES), lambda i: (i, 0, 0))],
        out_shape=[jax.ShapeDtypeStruct((N_EXPERTS, t), jnp.int32),
                   jax.ShapeDtypeStruct((N_EXPERTS, t), F32),
                   jax.ShapeDtypeStruct((nj, N_EXPERTS, LANES), jnp.int32)],
        compiler_params=_cparams(),
        name="moe_router",
    )(x, w_router.T, router_bias[:, None], tri)
    wgu = _cast_bf16(w_gate, w_up, layer=layer, name="moe_cast_gu")
    wd = _cast_bf16(w_down, layer=layer, name="moe_cast_d")
    acc = pl.pallas_call(
        _moe_kernel,
        grid_spec=pltpu.PrefetchScalarGridSpec(
            num_scalar_prefetch=1,
            grid=(nj, N_EXPERTS // MOE_UNROLL),
            in_specs=[pl.BlockSpec((tt, D_MODEL), lambda j, e, c: (j, 0)),
                      pl.BlockSpec((MOE_UNROLL, 1, tt), lambda j, e, c: (e, 0, j)),
                      pl.BlockSpec((MOE_UNROLL, 1, tt), lambda j, e, c: (e, 0, j)),
                      pl.BlockSpec((MOE_UNROLL, D_MODEL, 2 * EXPERT_DIM), lambda j, e, c: (e, 0, 0)),
                      pl.BlockSpec((MOE_UNROLL, EXPERT_DIM, D_MODEL), lambda j, e, c: (e, 0, 0))],
            out_specs=pl.BlockSpec((tt, D_MODEL), lambda j, e, c: (j, 0)),
            scratch_shapes=[pltpu.VMEM((MOE_GROUP * MOE_SLOTS, tt), BF16),
                            pltpu.VMEM((MOE_GROUP * MOE_SLOTS, D_MODEL), BF16)],
        ),
        out_shape=jax.ShapeDtypeStruct((t, D_MODEL), F32),
        compiler_params=_cparams(),
        name="moe_experts",
    )(cnt[:, :, 0].reshape(-1), xb, rank.reshape(N_EXPERTS, 1, t), gate.reshape(N_EXPERTS, 1, t),
      wgu, wd)
    tm = min(ROW_TILE, t)
    sgu = jnp.concatenate([sw_gate, sw_up], axis=1).astype(BF16)
    return pl.pallas_call(
        _moe_out_kernel,
        grid=(t // tm,),
        in_specs=[_row_spec(tm, D_MODEL), _row_spec(tm, D_MODEL), _row_spec(tm, D_MODEL),
                  _const_spec((D_MODEL, 2 * SHARED_DIM)), _const_spec((SHARED_DIM, D_MODEL)),
                  _const_spec((1, D_MODEL)), _const_spec((1, D_MODEL))],
        compiler_params=_cparams(),
        name="moe_out",
        **_x_outs(t, tm),
    )(x, xb, acc, sgu, sw_down.astype(BF16), ln_g[None], ln_b[None])


def kernel(x, attn_w_in, attn_w_out, ret_w_in, ret_gn_g, ret_gn_b, ret_w_out, ssm_a_re, ssm_a_im, ssm_b_re, ssm_b_im, ssm_c_re, ssm_c_im, ssm_d, ssm_log_dt, ssm_w_glu, moe_w_router, moe_router_bias, moe_w_gate, moe_w_up, moe_w_down, shared_w_gate, shared_w_up, shared_w_down, ln_g, ln_b):
    bsz, seq, d = x.shape
    depth = ln_g.shape[0]
    xf = x.reshape(bsz * seq, d)
    xb = xf.astype(BF16)
    for i in range(depth):
        j = i // N_MIXERS
        kind = i % N_MIXERS
        if kind == 0:
            xf, xb = _attention_layer(xf, xb, attn_w_in[j], attn_w_out[j], ln_g[i, 0], ln_b[i, 0],
                                      bsz, seq)
        elif kind == 1:
            xf, xb = _retention_layer(xf, xb, ret_w_in[j], ret_gn_g[j], ret_gn_b[j], ret_w_out[j],
                                      ln_g[i, 0], ln_b[i, 0], bsz, seq)
        else:
            xf, xb = _s5_layer(xf, ssm_a_re[j], ssm_a_im[j], ssm_b_re[j], ssm_b_im[j], ssm_c_re[j],
                               ssm_c_im[j], ssm_d[j], ssm_log_dt[j], ssm_w_glu[j],
                               ln_g[i, 0], ln_b[i, 0], bsz, seq)
        xf, xb = _moe_layer(xf, xb, moe_w_router[i], moe_router_bias[i], moe_w_gate, moe_w_up,
                            moe_w_down, shared_w_gate[i], shared_w_up[i], shared_w_down[i],
                            ln_g[i, 1], ln_b[i, 1], i)
    return xf.reshape(bsz, seq, d)
```

```python
import functools
import math

import jax
import jax.numpy as jnp
from jax import lax
from jax.experimental import pallas as pl
from jax.experimental.pallas import tpu as pltpu

F32 = jnp.float32
BF16 = jnp.bfloat16

D_MODEL = 1024
DEPTH = 4
N_MIXERS = 3
ATTN_GROUPS = ((128, 1), (512, 4), (2048, 16))
ATTN_HEADS = 8
ATTN_HEAD_DIM = D_MODEL // ATTN_HEADS
ATTN_BLOCK = 128
RET_HEADS = 4
RET_KEY_DIM = D_MODEL // RET_HEADS
RET_VAL_DIM = 2 * RET_KEY_DIM
RET_CHUNK = 128
SSM_GROUP = 16
SSM_GROUPS = D_MODEL // SSM_GROUP
SSM_STATE = 64
N_EXPERTS = 64
TOP_K = 8
N_EXPERT_GROUPS = 8
TOPK_GROUPS = 4
EXPERT_DIM = 256
SHARED_DIM = 256
ROUTED_SCALE = 2.5
DEEPNORM_ALPHA = (2 * DEPTH) ** 0.25
LN_EPS = 1e-5

LANES = 128
SUBLANES = 8
MASK_NEG = -1e30
VMEM_LIMIT = 56 * 1024 * 1024

ROW_TILE = 512
PROJ_TM = 2048
PROJ_TN = 1024
MOE_TILE = 1024
MOE_SLOTS = 160
MOE_GROUP = 16
MOE_UNROLL = 4
SSM_TIME = 1024
SSM_COLS = 128


def _cparams():
    return pltpu.CompilerParams(vmem_limit_bytes=VMEM_LIMIT)


def _dot(a, b):
    return jnp.dot(a, b, preferred_element_type=F32)


def _dot_nt(a, b):
    return lax.dot_general(a, b, (((1,), (1,)), ((), ())), preferred_element_type=F32)


def _dot_tn(a, b):
    return lax.dot_general(a, b, (((0,), (0,)), ((), ())), preferred_element_type=F32)


def _sigmoid(x):
    return 1.0 / (1.0 + jnp.exp(-x))


def _layer_norm(y, g, b):
    mu = jnp.mean(y, axis=-1, keepdims=True)
    yc = y - mu
    var = jnp.mean(yc * yc, axis=-1, keepdims=True)
    return yc * lax.rsqrt(var + LN_EPS) * g + b


def _store_ln(y, g_ref, b_ref, o_ref, ob_ref):
    out = _layer_norm(y, g_ref[...], b_ref[...])
    o_ref[...] = out
    ob_ref[...] = out.astype(BF16)


def _proj_kernel(x_ref, w_ref, o_ref):
    o_ref[...] = _dot(x_ref[...], w_ref[...]).astype(o_ref.dtype)


def _proj(xb, w, out_dtype, name):
    m, k = xb.shape
    n = w.shape[1]
    tm = min(PROJ_TM, m)
    tn = min(PROJ_TN, n)
    return pl.pallas_call(
        _proj_kernel,
        grid=(m // tm, n // tn),
        in_specs=[pl.BlockSpec((tm, k), lambda i, j: (i, 0)),
                  pl.BlockSpec((k, tn), lambda i, j: (0, j))],
        out_specs=pl.BlockSpec((tm, tn), lambda i, j: (i, j)),
        out_shape=jax.ShapeDtypeStruct((m, n), out_dtype),
        compiler_params=_cparams(),
        name=name,
    )(xb, w)


def _proj_perm_kernel(x_ref, w_ref, o_ref, xs_ref, col_ref, *, dil):
    rows = x_ref.shape[0] // dil

    @pl.when(pl.program_id(1) == 0)
    def _():
        for c in range(x_ref.shape[1] // LANES):
            cs = slice(c * LANES, (c + 1) * LANES)
            col_ref[...] = x_ref[:, cs]
            for r in range(dil):
                xs_ref[r * rows:(r + 1) * rows, cs] = col_ref[pl.ds(r, rows, stride=dil), :].astype(BF16)

    res = _dot(xs_ref[...], w_ref[...])
    o_ref[0] = res.reshape(dil, rows, res.shape[1]).astype(o_ref.dtype)


def _proj_perm(xf, w, dil, bsz, seq, name):
    k = xf.shape[1]
    n = w.shape[1]
    tm = min(PROJ_TM, seq)
    tn = min(PROJ_TN, n)
    assert seq % tm == 0 and tm % (dil * 2 * SUBLANES) == 0
    per_b = seq // tm
    rows = tm // dil
    return pl.pallas_call(
        functools.partial(_proj_perm_kernel, dil=dil),
        grid=(bsz * per_b, n // tn),
        in_specs=[pl.BlockSpec((tm, k), lambda i, j: (i, 0)),
                  pl.BlockSpec((k, tn), lambda i, j: (0, j))],
        out_specs=pl.BlockSpec((1, dil, rows, tn), lambda i, j: (i // per_b, 0, i % per_b, j)),
        out_shape=jax.ShapeDtypeStruct((bsz, dil, seq // dil, n), BF16),
        scratch_shapes=[pltpu.VMEM((tm, k), BF16), pltpu.VMEM((tm, LANES), F32)],
        compiler_params=_cparams(),
        name=name,
    )(xf, w)


def _attn_kernel(q_ref, kc_ref, vc_ref, bp_ref, bc_ref, o_ref, lse_ref,
                 o_scr, lse_scr, mix_scr, kp_scr, vp_scr, *, dil):
    n = pl.program_id(1)
    r = pl.program_id(2)

    @pl.when(n == 0)
    def _():
        kp_scr[r] = jnp.zeros(kp_scr.shape[1:], kp_scr.dtype)
        vp_scr[r] = jnp.zeros(vp_scr.shape[1:], vp_scr.dtype)

    first = jnp.where(n == 0, MASK_NEG, 0.0).astype(F32)
    scale = ATTN_HEAD_DIM ** -0.5
    lane = lax.broadcasted_iota(jnp.int32, (ATTN_BLOCK, LANES), 1)
    lse_tile = jnp.zeros((ATTN_BLOCK, LANES), F32)
    heads = range(ATTN_HEADS)
    sls = [slice(h * ATTN_HEAD_DIM, (h + 1) * ATTN_HEAD_DIM) for h in heads]
    sps = [_dot_nt(q_ref[0, 0, :, sls[h]], kp_scr[r, :, sls[h]]) * scale + (bp_ref[h] + first)
           for h in heads]
    scs = [_dot_nt(q_ref[0, 0, :, sls[h]], kc_ref[0, 0, :, sls[h]]) * scale + bc_ref[h]
           for h in heads]
    ms = [jnp.max(jnp.maximum(sps[h], scs[h]), axis=1, keepdims=True) for h in heads]
    pps = [jnp.exp(sps[h] - ms[h]) for h in heads]
    pcs = [jnp.exp(scs[h] - ms[h]) for h in heads]
    ls = [jnp.sum(pps[h] + pcs[h], axis=1, keepdims=True) for h in heads]
    for h in heads:
        o = (_dot(pps[h].astype(BF16), vp_scr[r, :, sls[h]])
             + _dot(pcs[h].astype(BF16), vc_ref[0, 0, :, sls[h]]))
        o_scr[h, r] = o / ls[h]
        lse_tile = jnp.where(lane == h, ms[h] + jnp.log(ls[h]), lse_tile)
    lse_scr[r] = lse_tile
    kp_scr[r] = kc_ref[0, 0]
    vp_scr[r] = vc_ref[0, 0]

    @pl.when(r == dil - 1)
    def _():
        for rr in range(dil):
            lse_ref[0, pl.ds(rr, ATTN_BLOCK, stride=dil), :] = lse_scr[rr]
        for h in range(ATTN_HEADS):
            for rr in range(dil):
                mix_scr[pl.ds(rr, ATTN_BLOCK, stride=dil), :] = o_scr[h, rr]
            o_ref[0, :, h * ATTN_HEAD_DIM:(h + 1) * ATTN_HEAD_DIM] = mix_scr[...]


def _attn_bias(window, dil):
    steps = window // dil
    assert steps <= ATTN_BLOCK
    qi = jnp.arange(ATTN_BLOCK)[:, None]
    kj = jnp.arange(ATTN_BLOCK)[None, :]
    slopes = 2.0 ** (-8.0 * (jnp.arange(ATTN_HEADS, dtype=F32) + 1.0) / ATTN_HEADS)

    def bias(step):
        valid = (step >= 0) & (step <= steps)
        dist = (step * dil).astype(F32)
        return jnp.where(valid[None], -(slopes[:, None, None] * dist[None]), MASK_NEG).astype(F32)

    return bias(qi - kj + ATTN_BLOCK), bias(qi - kj)


def _attn_group(qkv, gi, window, dil):
    b, _, ls, width = qkv.shape
    s = ls * dil
    hd = ATTN_HEADS * ATTN_HEAD_DIM
    assert ls % ATTN_BLOCK == 0 and width == 3 * hd
    nb = ls // ATTN_BLOCK
    bias_prev, bias_cur = _attn_bias(window, dil)

    def cur(c):
        return pl.BlockSpec((1, 1, ATTN_BLOCK, hd), lambda bi, n, r: (bi, r, n, c))

    const = pl.BlockSpec((ATTN_HEADS, ATTN_BLOCK, ATTN_BLOCK), lambda bi, n, r: (0, 0, 0))
    o, lse = pl.pallas_call(
        functools.partial(_attn_kernel, dil=dil),
        grid=(b, nb, dil),
        in_specs=[cur(0), cur(1), cur(2), const, const],
        out_specs=[pl.BlockSpec((1, ATTN_BLOCK * dil, hd), lambda bi, n, r: (bi, n, 0)),
                   pl.BlockSpec((1, ATTN_BLOCK * dil, LANES), lambda bi, n, r: (bi, n, 0))],
        out_shape=[jax.ShapeDtypeStruct((b, s, hd), F32),
                   jax.ShapeDtypeStruct((b, s, LANES), F32)],
        scratch_shapes=[pltpu.VMEM((ATTN_HEADS, dil, ATTN_BLOCK, ATTN_HEAD_DIM), F32),
                        pltpu.VMEM((dil, ATTN_BLOCK, LANES), F32),
                        pltpu.VMEM((dil * ATTN_BLOCK, ATTN_HEAD_DIM), F32),
                        pltpu.VMEM((dil, ATTN_BLOCK, hd), BF16),
                        pltpu.VMEM((dil, ATTN_BLOCK, hd), BF16)],
        compiler_params=_cparams(),
        name=f"attn_g{gi}",
    )(qkv, qkv, qkv, bias_prev, bias_cur)
    return o.reshape(b * s, hd), lse.reshape(b * s, LANES)


def _attn_out_kernel(o1_ref, o2_ref, o3_ref, l1_ref, l2_ref, l3_ref, x_ref, w_ref, g_ref, b_ref,
                     o_ref, ob_ref):
    l1, l2, l3 = l1_ref[...], l2_ref[...], l3_ref[...]
    mx = jnp.maximum(jnp.maximum(l1, l2), l3)
    e1, e2, e3 = jnp.exp(l1 - mx), jnp.exp(l2 - mx), jnp.exp(l3 - mx)
    den = e1 + e2 + e3
    w1, w2, w3 = e1 / den, e2 / den, e3 / den
    cols = []
    for h in range(ATTN_HEADS):
        sl = slice(h * ATTN_HEAD_DIM, (h + 1) * ATTN_HEAD_DIM)
        oh = (w1[:, h:h + 1] * o1_ref[:, sl] + w2[:, h:h + 1] * o2_ref[:, sl]
              + w3[:, h:h + 1] * o3_ref[:, sl])
        cols.append(oh.astype(BF16))
    o = jnp.concatenate(cols, axis=1)
    y = _dot(o, w_ref[...]) + DEEPNORM_ALPHA * x_ref[...]
    _store_ln(y, g_ref, b_ref, o_ref, ob_ref)


def _row_spec(tm, width):
    return pl.BlockSpec((tm, width), lambda i: (i, 0))


def _const_spec(shape):
    return pl.BlockSpec(shape, lambda i: tuple(0 for _ in shape))


def _x_outs(t, tm):
    return dict(
        out_specs=[_row_spec(tm, D_MODEL), _row_spec(tm, D_MODEL)],
        out_shape=[jax.ShapeDtypeStruct((t, D_MODEL), F32), jax.ShapeDtypeStruct((t, D_MODEL), BF16)],
    )


def _attention_layer(x, xb, w_in, w_out, ln_g, ln_b, bsz, seq):
    t = x.shape[0]
    hd3 = 3 * ATTN_HEADS * ATTN_HEAD_DIM
    wb = w_in.astype(BF16)
    outs = []
    for gi, (window, dil) in enumerate(ATTN_GROUPS):
        qkv = _proj_perm(x, wb[:, gi * hd3:(gi + 1) * hd3], dil, bsz, seq, f"attn_proj_g{gi}")
        outs.append(_attn_group(qkv, gi, window, dil))
    tm = min(ROW_TILE, t)
    hd = ATTN_HEADS * ATTN_HEAD_DIM
    return pl.pallas_call(
        _attn_out_kernel,
        grid=(t // tm,),
        in_specs=[_row_spec(tm, hd)] * 3 + [_row_spec(tm, LANES)] * 3
                 + [_row_spec(tm, D_MODEL), _const_spec((hd, D_MODEL)),
                    _const_spec((1, D_MODEL)), _const_spec((1, D_MODEL))],
        compiler_params=_cparams(),
        name="attn_out",
        **_x_outs(t, tm),
    )(outs[0][0], outs[1][0], outs[2][0], outs[0][1], outs[1][1], outs[2][1],
      x, w_out.astype(BF16), ln_g[None], ln_b[None])


def _ret_kernel(q_ref, k_ref, v_ref, g_ref, dec_ref, xi_ref, zeta_ref, cd_ref, gg_ref, gb_ref,
                y_ref, state_ref):
    n = pl.program_id(1)

    @pl.when(n == 0)
    def _():
        state_ref[...] = jnp.zeros_like(state_ref)

    kscale = RET_KEY_DIM ** -0.5
    dk, dv = RET_KEY_DIM, RET_VAL_DIM
    hs = range(RET_HEADS)
    qs = [q_ref[0, :, h * dk:(h + 1) * dk] for h in hs]
    ks = [k_ref[0, :, h * dk:(h + 1) * dk] for h in hs]
    vs = [v_ref[0, :, h * dv:(h + 1) * dv] for h in hs]
    scores = [_dot_nt(qs[h], ks[h]) * kscale * dec_ref[h] for h in hs]
    inners = [_dot(scores[h].astype(BF16), vs[h]) for h in hs]
    sts = [state_ref[h] for h in hs]
    crosses = [_dot((qs[h].astype(F32) * xi_ref[h]).astype(BF16), sts[h].astype(BF16)) for h in hs]
    kzs = [(ks[h].astype(F32) * kscale * zeta_ref[h]).astype(BF16) for h in hs]
    for h in hs:
        state_ref[h] = cd_ref[h, 0:1, 0:1] * sts[h] + _dot_tn(kzs[h], vs[h])
    for h in hs:
        sl = slice(h * dv, (h + 1) * dv)
        y = inners[h] + crosses[h]
        mu = jnp.mean(y, axis=1, keepdims=True)
        yc = y - mu
        var = jnp.mean(yc * yc, axis=1, keepdims=True)
        yn = yc * lax.rsqrt(var + LN_EPS) * gg_ref[:, sl] + gb_ref[:, sl]
        g = g_ref[0, :, sl]
        y_ref[0, :, sl] = ((g * _sigmoid(g)) * yn).astype(y_ref.dtype)


def _ret_out_kernel(y_ref, x_ref, w_ref, g_ref, b_ref, o_ref, ob_ref):
    y = _dot(y_ref[...], w_ref[...]) + DEEPNORM_ALPHA * x_ref[...]
    _store_ln(y, g_ref, b_ref, o_ref, ob_ref)


def _retention_layer(x, xb, w_in, gn_g, gn_b, w_out, ln_g, ln_b, bsz, seq):
    t = x.shape[0]
    h, dk, dv, c = RET_HEADS, RET_KEY_DIM, RET_VAL_DIM, RET_CHUNK
    nqkv = 2 * h * dk + h * dv
    wb = w_in.astype(BF16)
    qkv = _proj(xb, wb[:, :nqkv], BF16, "ret_proj_qkv").reshape(bsz, seq, nqkv)
    gate = _proj(xb, wb[:, nqkv:], F32, "ret_proj_gate").reshape(bsz, seq, h * dv)
    log_gamma = jnp.log(1.0 - 2.0 ** (-5.0 - jnp.arange(h, dtype=F32)))
    pos = jnp.arange(c, dtype=F32)
    rel = pos[:, None] - pos[None, :]
    decay = jnp.where(rel >= 0, jnp.exp(log_gamma[:, None, None] * jnp.maximum(rel, 0.0)), 0.0)
    xi = jnp.exp(log_gamma[:, None] * (pos[None, :] + 1.0))
    zeta = jnp.exp(log_gamma[:, None] * (c - 1.0 - pos[None, :]))
    xi = jnp.broadcast_to(xi[:, :, None], (h, c, dk))
    zeta = jnp.broadcast_to(zeta[:, :, None], (h, c, dk))
    cd = jnp.broadcast_to(jnp.exp(log_gamma * c)[:, None, None], (h, SUBLANES, LANES))
    assert h * dv == 2 * h * dk
    full = lambda *shape: pl.BlockSpec(shape, lambda bi, n: tuple(0 for _ in shape))
    y = pl.pallas_call(
        _ret_kernel,
        grid=(bsz, seq // c),
        in_specs=[pl.BlockSpec((1, c, h * dk), lambda bi, n: (bi, n, 0)),
                  pl.BlockSpec((1, c, h * dk), lambda bi, n: (bi, n, 1)),
                  pl.BlockSpec((1, c, h * dv), lambda bi, n: (bi, n, 1)),
                  pl.BlockSpec((1, c, h * dv), lambda bi, n: (bi, n, 0)),
                  full(h, c, c), full(h, c, dk), full(h, c, dk), full(h, SUBLANES, LANES),
                  full(1, h * dv), full(1, h * dv)],
        out_specs=pl.BlockSpec((1, c, h * dv), lambda bi, n: (bi, n, 0)),
        out_shape=jax.ShapeDtypeStruct((bsz, seq, h * dv), BF16),
        scratch_shapes=[pltpu.VMEM((h, dk, dv), F32)],
        compiler_params=_cparams(),
        name="ret_core",
    )(qkv, qkv, qkv, gate, decay, xi, zeta, cd, gn_g[None], gn_b[None])
    tm = min(ROW_TILE, t)
    return pl.pallas_call(
        _ret_out_kernel,
        grid=(t // tm,),
        in_specs=[_row_spec(tm, h * dv), _row_spec(tm, D_MODEL), _const_spec((h * dv, D_MODEL)),
                  _const_spec((1, D_MODEL)), _const_spec((1, D_MODEL))],
        compiler_params=_cparams(),
        name="ret_out",
        **_x_outs(t, tm),
    )(y.reshape(t, h * dv), x, w_out.astype(BF16), ln_g[None], ln_b[None])


def _gelu_tanh(y):
    return 0.5 * y * (1.0 + jnp.tanh(math.sqrt(2.0 / math.pi) * (y + 0.044715 * (y * y * y))))


def _s5_kernel(x_ref, wb_ref, wc_ref, pr_ref, pi_ref, d_ref, y_ref,
               cr_ref, ci_ref, xs_ref, hr_scr, hi_scr, yt_ref):
    n = pl.program_id(2)

    @pl.when(n == 0)
    def _():
        cr_ref[...] = jnp.zeros_like(cr_ref)
        ci_ref[...] = jnp.zeros_like(ci_ref)

    nch = pr_ref.shape[2]
    seg = SUBLANES
    tau_n = SSM_TIME // seg
    for s in range(seg):
        xs_ref[pl.ds(s, tau_n, stride=seg), :] = x_ref[0, s * tau_n:(s + 1) * tau_n, :]
    u = xs_ref[...]
    bu = _dot(u.astype(BF16), wb_ref[0])
    re = bu[:, :nch].reshape(tau_n, seg, nch)
    im = bu[:, nch:].reshape(tau_n, seg, nch)
    a_r = jnp.broadcast_to(pr_ref[0, 0:1, :], (seg, nch))
    a_i = jnp.broadcast_to(pi_ref[0, 0:1, :], (seg, nch))
    hr = jnp.zeros((seg, nch), F32)
    hi = jnp.zeros((seg, nch), F32)
    for t in range(tau_n):
        hr, hi = a_r * hr - a_i * hi + re[t], a_r * hi + a_i * hr + im[t]
        hr_scr[t] = hr
        hi_scr[t] = hi
    q_r = pr_ref[0, tau_n - 1:tau_n, :]
    q_i = pi_ref[0, tau_n - 1:tau_n, :]
    c_r = cr_ref[0:1, :]
    c_i = ci_ref[0:1, :]
    row = lax.broadcasted_iota(jnp.int32, (seg, nch), 0)
    in_r = jnp.zeros((seg, nch), F32)
    in_i = jnp.zeros((seg, nch), F32)
    for s in range(seg):
        in_r = jnp.where(row == s, c_r, in_r)
        in_i = jnp.where(row == s, c_i, in_i)
        c_r, c_i = (hr[s:s + 1, :] + (q_r * c_r - q_i * c_i),
                    hi[s:s + 1, :] + (q_r * c_i + q_i * c_r))
    cr_ref[...] = jnp.broadcast_to(c_r, cr_ref.shape)
    ci_ref[...] = jnp.broadcast_to(c_i, ci_ref.shape)
    hre, him = [], []
    for t in range(tau_n):
        p_r = pr_ref[0, t:t + 1, :]
        p_i = pi_ref[0, t:t + 1, :]
        hre.append(hr_scr[t] + (p_r * in_r - p_i * in_i))
        him.append(hi_scr[t] + (p_r * in_i + p_i * in_r))
    hcat = jnp.concatenate([jnp.concatenate(hre, axis=0), jnp.concatenate(him, axis=0)], axis=1)
    y = _dot(hcat.astype(BF16), wc_ref[0]) + d_ref[...] * u
    yt_ref[...] = _gelu_tanh(y)
    for s in range(seg):
        y_ref[0, s * tau_n:(s + 1) * tau_n, :] = yt_ref[pl.ds(s, tau_n, stride=seg), :].astype(y_ref.dtype)


def _s5_out_kernel(y_ref, x_ref, w_ref, g_ref, b_ref, o_ref, ob_ref):
    vg = _dot(y_ref[...], w_ref[...])
    val = vg[:, :D_MODEL]
    gate = vg[:, D_MODEL:]
    y = val * _sigmoid(gate) + DEEPNORM_ALPHA * x_ref[...]
    _store_ln(y, g_ref, b_ref, o_ref, ob_ref)


def _block_diag(m):
    nb, g, r, c = m.shape
    eye = jnp.eye(g, dtype=m.dtype)
    return (m[:, :, :, None, :] * eye[None, :, None, :, None]).reshape(nb, g * r, g * c)


def _s5_layer(x, a_re, a_im, b_re, b_im, c_re, c_im, d_skip, log_dt, w_glu, ln_g, ln_b, bsz, seq):
    t = x.shape[0]
    gpb = SSM_COLS // SSM_GROUP
    ncb = D_MODEL // SSM_COLS
    nch = gpb * SSM_STATE
    dt = jnp.exp(log_dt)[:, None]
    mag = jnp.exp(dt * a_re)
    ab_re = mag * jnp.cos(dt * a_im)
    ab_im = mag * jnp.sin(dt * a_im)
    den = a_re * a_re + a_im * a_im
    f_re = ((ab_re - 1.0) * a_re + ab_im * a_im) / den
    f_im = (ab_im * a_re - (ab_re - 1.0) * a_im) / den
    bb_re = f_re[..., None] * b_re - f_im[..., None] * b_im
    bb_im = f_re[..., None] * b_im + f_im[..., None] * b_re
    tb = lambda m: jnp.swapaxes(m, 1, 2).reshape(ncb, gpb, SSM_GROUP, SSM_STATE)
    w_b = jnp.concatenate([_block_diag(tb(bb_re)), _block_diag(tb(bb_im))], axis=2).astype(BF16)
    tc = lambda m: jnp.swapaxes(m, 1, 2).reshape(ncb, gpb, SSM_STATE, SSM_GROUP)
    w_c = jnp.concatenate([_block_diag(tc(c_re)), -_block_diag(tc(c_im))], axis=1).astype(BF16)
    tau_n = SSM_TIME // SUBLANES
    assert tau_n & (tau_n - 1) == 0
    pr, pi = ab_re[None], ab_im[None]
    while pr.shape[0] < tau_n:
        lr, li = pr[-1], pi[-1]
        pr, pi = (jnp.concatenate([pr, pr * lr - pi * li], axis=0),
                  jnp.concatenate([pi, pr * li + pi * lr], axis=0))
    pw_re = pr.reshape(tau_n, ncb, nch).transpose(1, 0, 2)
    pw_im = pi.reshape(tau_n, ncb, nch).transpose(1, 0, 2)
    x3 = x.reshape(bsz, seq, D_MODEL)
    blk = lambda shape: pl.BlockSpec((1,) + shape, lambda bi, cb, n: (cb, 0, 0))
    y = pl.pallas_call(
        _s5_kernel,
        grid=(bsz, ncb, seq // SSM_TIME),
        in_specs=[pl.BlockSpec((1, SSM_TIME, SSM_COLS), lambda bi, cb, n: (bi, n, cb)),
                  blk((SSM_COLS, 2 * nch)), blk((2 * nch, SSM_COLS)),
                  blk((tau_n, nch)), blk((tau_n, nch)),
                  pl.BlockSpec((1, SSM_COLS), lambda bi, cb, n: (0, cb))],
        out_specs=pl.BlockSpec((1, SSM_TIME, SSM_COLS), lambda bi, cb, n: (bi, n, cb)),
        out_shape=jax.ShapeDtypeStruct((bsz, seq, D_MODEL), BF16),
        scratch_shapes=[pltpu.VMEM((SUBLANES, nch), F32), pltpu.VMEM((SUBLANES, nch), F32),
                        pltpu.VMEM((SSM_TIME, SSM_COLS), F32),
                        pltpu.VMEM((tau_n, SUBLANES, nch), F32),
                        pltpu.VMEM((tau_n, SUBLANES, nch), F32),
                        pltpu.VMEM((SSM_TIME, SSM_COLS), F32)],
        compiler_params=_cparams(),
        name="s5_core",
    )(x3, w_b, w_c, pw_re, pw_im, d_skip.reshape(1, D_MODEL))
    tm = min(ROW_TILE, t)
    return pl.pallas_call(
        _s5_out_kernel,
        grid=(t // tm,),
        in_specs=[_row_spec(tm, D_MODEL), _row_spec(tm, D_MODEL), _const_spec((D_MODEL, 2 * D_MODEL)),
                  _const_spec((1, D_MODEL)), _const_spec((1, D_MODEL))],
        compiler_params=_cparams(),
        name="s5_out",
        **_x_outs(t, tm),
    )(y.reshape(t, D_MODEL), x, w_glu.astype(BF16), ln_g[None], ln_b[None])


def _cast_kernel(*refs):
    o_ref = refs[-1]
    o_ref[...] = jnp.concatenate([r[0].astype(o_ref.dtype) for r in refs[:-1]], axis=-1)


def _cast_bf16(*ws, layer, name):
    _, e, r, c = ws[0].shape
    return pl.pallas_call(
        _cast_kernel,
        grid=(e,),
        in_specs=[pl.BlockSpec((1, 1, r, c), lambda i: (layer, i, 0, 0)) for _ in ws],
        out_specs=pl.BlockSpec((1, r, c * len(ws)), lambda i: (i, 0, 0)),
        out_shape=jax.ShapeDtypeStruct((e, r, c * len(ws)), BF16),
        compiler_params=_cparams(),
        name=name,
    )(*ws)


def _first_index(eq, iota, size, axis):
    return jnp.min(jnp.where(eq, iota, size), axis=axis, keepdims=True)


def _router_kernel(x_ref, w_ref, b_ref, tri_ref, rank_ref, gate_ref, cnt_ref):
    x = x_ref[...]
    w = w_ref[...]
    x_hi = x.astype(BF16)
    x_lo = (x - x_hi.astype(F32)).astype(BF16)
    w_hi = w.astype(BF16)
    w_lo = (w - w_hi.astype(F32)).astype(BF16)
    logits = _dot_nt(w_hi, x_hi) + (_dot_nt(w_hi, x_lo) + _dot_nt(w_lo, x_hi))
    scores = _sigmoid(logits)
    sel = scores + b_ref[...]
    tt = sel.shape[1]
    gsz = N_EXPERTS // N_EXPERT_GROUPS
    grp = sel.reshape(N_EXPERT_GROUPS, gsz, tt)
    wi = lax.broadcasted_iota(jnp.int32, grp.shape, 1)
    m1 = jnp.max(grp, axis=1, keepdims=True)
    i1 = _first_index(grp == m1, wi, gsz, 1)
    m2 = jnp.max(jnp.where(wi == i1, -jnp.inf, grp), axis=1, keepdims=True)
    gs = (m1 + m2).reshape(N_EXPERT_GROUPS, tt)
    gi = lax.broadcasted_iota(jnp.int32, gs.shape, 0)
    gmask = jnp.zeros(gs.shape, jnp.bool_)
    for _ in range(TOPK_GROUPS):
        gm = jnp.max(gs, axis=0, keepdims=True)
        pick = gi == _first_index(gs == gm, gi, N_EXPERT_GROUPS, 0)
        gmask = gmask | pick
        gs = jnp.where(pick, -jnp.inf, gs)
    emask = jnp.broadcast_to(gmask[:, None, :], grp.shape).reshape(N_EXPERTS, tt)
    cand = jnp.where(emask, sel, -jnp.inf)
    ei = lax.broadcasted_iota(jnp.int32, cand.shape, 0)
    chosen = jnp.zeros(cand.shape, jnp.bool_)
    for _ in range(TOP_K):
        cm = jnp.max(cand, axis=0, keepdims=True)
        pick = ei == _first_index(cand == cm, ei, N_EXPERTS, 0)
        chosen = chosen | pick
        cand = jnp.where(pick, -jnp.inf, cand)
    top_w = jnp.where(chosen, scores, 0.0)
    denom = jnp.sum(top_w, axis=0, keepdims=True)
    gate_ref[...] = top_w / denom * ROUTED_SCALE
    chosen_f = chosen.astype(F32)
    rank = _dot(chosen_f.astype(BF16), tri_ref[...])
    rank_ref[...] = jnp.where(chosen, rank, -1.0).astype(jnp.int32)
    cnt = jnp.sum(chosen_f, axis=1, keepdims=True)
    cnt_ref[0] = jnp.broadcast_to(cnt, (N_EXPERTS, LANES)).astype(jnp.int32)


def _moe_kernel(cnt_ref, x_ref, rank_ref, gate_ref, wgu_ref, wd_ref, o_ref, oh_scr, yg_scr):
    j = pl.program_id(0)
    s = pl.program_id(1)

    @pl.when(s == 0)
    def _():
        o_ref[...] = jnp.zeros_like(o_ref)

    tt = x_ref.shape[0]
    slot = lax.broadcasted_iota(jnp.int32, (MOE_SLOTS, tt), 0)

    def expert_chunks(us, c):
        hits = [(rank_ref[u] - c * MOE_SLOTS) == slot for u in us]
        onehots = [hit.astype(F32).astype(BF16) for hit in hits]
        xcs = [_dot(oh, x_ref[...]).astype(BF16) for oh in onehots]
        hgus = [_dot(xc, wgu_ref[u]) for xc, u in zip(xcs, us)]
        hbs = [((h[:, :EXPERT_DIM] * _sigmoid(h[:, :EXPERT_DIM])) * h[:, EXPERT_DIM:]).astype(BF16)
               for h in hgus]
        ys = [_dot(hb, wd_ref[u]) for hb, u in zip(hbs, us)]
        gsels = [jnp.sum(jnp.where(hit, gate_ref[u], 0.0), axis=1, keepdims=True)
                 for hit, u in zip(hits, us)]
        return onehots, [(y * g).astype(BF16) for y, g in zip(ys, gsels)]

    steps_per_group = MOE_GROUP // MOE_UNROLL
    gs = s % steps_per_group
    us = list(range(MOE_UNROLL))
    onehots, ygs = expert_chunks(us, 0)
    for u in us:
        row0 = pl.multiple_of((gs * MOE_UNROLL + u) * MOE_SLOTS, MOE_SLOTS)
        oh_scr[pl.ds(row0, MOE_SLOTS), :] = onehots[u]
        yg_scr[pl.ds(row0, MOE_SLOTS), :] = ygs[u]

    for u in us:
        cnt = cnt_ref[j * N_EXPERTS + s * MOE_UNROLL + u]
        n_chunks = (cnt + MOE_SLOTS - 1) // MOE_SLOTS

        def overflow(c, carry, u=u):
            onehot_c, yg_c = expert_chunks([u], c)
            o_ref[...] += _dot_tn(onehot_c[0], yg_c[0])
            return carry

        lax.fori_loop(1, n_chunks, overflow, 0)

    @pl.when(gs == steps_per_group - 1)
    def _():
        o_ref[...] += _dot_tn(oh_scr[...], yg_scr[...])


def _moe_out_kernel(x_ref, xb_ref, acc_ref, wgu_ref, wd_ref, g_ref, b_ref, o_ref, ob_ref):
    hgu = _dot(xb_ref[...], wgu_ref[...])
    hg = hgu[:, :SHARED_DIM]
    hb = ((hg * _sigmoid(hg)) * hgu[:, SHARED_DIM:]).astype(BF16)
    shared = _dot(hb, wd_ref[...])
    y = DEEPNORM_ALPHA * x_ref[...] + (acc_ref[...] + shared)
    _store_ln(y, g_ref, b_ref, o_ref, ob_ref)


def _moe_layer(x, xb, w_router, router_bias, w_gate, w_up, w_down, sw_gate, sw_up, sw_down,
               ln_g, ln_b, layer):
    t = x.shape[0]
    tt = min(MOE_TILE, t)
    nj = t // tt
    tri = jnp.triu(jnp.ones((tt, tt), F32), 1).astype(BF16)
    rank, gate, cnt = pl.pallas_call(
        _router_kernel,
        grid=(nj,),
        in_specs=[_row_spec(tt, D_MODEL), _const_spec((N_EXPERTS, D_MODEL)),
                  _const_spec((N_EXPERTS, 1)), _const_spec((tt, tt))],
        out_specs=[pl.BlockSpec((N_EXPERTS, tt), lambda i: (0, i)),
                   pl.BlockSpec((N_EXPERTS, tt), lambda i: (0, i)),
                   pl.BlockSpec((1, N_EXPERTS, LANES), lambda i: (i, 0, 0))],
        out_shape=[jax.ShapeDtypeStruct((N_EXPERTS, t), jnp.int32),
                   jax.ShapeDtypeStruct((N_EXPERTS, t), F32),
                   jax.ShapeDtypeStruct((nj, N_EXPERTS, LANES), jnp.int32)],
        compiler_params=_cparams(),
        name="moe_router",
    )(x, w_router.T, router_bias[:, None], tri)
    wgu = _cast_bf16(w_gate, w_up, layer=layer, name="moe_cast_gu")
    wd = _cast_bf16(w_down, layer=layer, name="moe_cast_d")
    acc = pl.pallas_call(
        _moe_kernel,
        grid_spec=pltpu.PrefetchScalarGridSpec(
            num_scalar_prefetch=1,
            grid=(nj, N_EXPERTS // MOE_UNROLL),
            in_specs=[pl.BlockSpec((tt, D_MODEL), lambda j, e, c: (j, 0)),
                      pl.BlockSpec((MOE_UNROLL, 1, tt), lambda j, e, c: (e, 0, j)),
                      pl.BlockSpec((MOE_UNROLL, 1, tt), lambda j, e, c: (e, 0, j)),
                      pl.BlockSpec((MOE_UNROLL, D_MODEL, 2 * EXPERT_DIM), lambda j, e, c: (e, 0, 0)),
                      pl.BlockSpec((MOE_UNROLL, EXPERT_DIM, D_MODEL), lambda j, e, c: (e, 0, 0))],
            out_specs=pl.BlockSpec((tt, D_MODEL), lambda j, e, c: (j, 0)),
            scratch_shapes=[pltpu.VMEM((MOE_GROUP * MOE_SLOTS, tt), BF16),
                            pltpu.VMEM((MOE_GROUP * MOE_SLOTS, D_MODEL), BF16)],
        ),
        out_shape=jax.ShapeDtypeStruct((t, D_MODEL), F32),
        compiler_params=_cparams(),
        name="moe_experts",
    )(cnt[:, :, 0].reshape(-1), xb, rank.reshape(N_EXPERTS, 1, t), gate.reshape(N_EXPERTS, 1, t),
      wgu, wd)
    tm = min(ROW_TILE, t)
    sgu = jnp.concatenate([sw_gate, sw_up], axis=1).astype(BF16)
    return pl.pallas_call(
        _moe_out_kernel,
        grid=(t // tm,),
        in_specs=[_row_spec(tm, D_MODEL), _row_spec(tm, D_MODEL), _row_spec(tm, D_MODEL),
                  _const_spec((D_MODEL, 2 * SHARED_DIM)), _const_spec((SHARED_DIM, D_MODEL)),
                  _const_spec((1, D_MODEL)), _const_spec((1, D_MODEL))],
        compiler_params=_cparams(),
        name="moe_out",
        **_x_outs(t, tm),
    )(x, xb, acc, sgu, sw_down.astype(BF16), ln_g[None], ln_b[None])


def kernel(x, attn_w_in, attn_w_out, ret_w_in, ret_gn_g, ret_gn_b, ret_w_out, ssm_a_re, ssm_a_im, ssm_b_re, ssm_b_im, ssm_c_re, ssm_c_im, ssm_d, ssm_log_dt, ssm_w_glu, moe_w_router, moe_router_bias, moe_w_gate, moe_w_up, moe_w_down, shared_w_gate, shared_w_up, shared_w_down, ln_g, ln_b):
    bsz, seq, d = x.shape
    depth = ln_g.shape[0]
    xf = x.reshape(bsz * seq, d)
    xb = xf.astype(BF16)
    for i in range(depth):
        j = i // N_MIXERS
        kind = i % N_MIXERS
        if kind == 0:
            xf, xb = _attention_layer(xf, xb, attn_w_in[j], attn_w_out[j], ln_g[i, 0], ln_b[i, 0],
                                      bsz, seq)
        elif kind == 1:
            xf, xb = _retention_layer(xf, xb, ret_w_in[j], ret_gn_g[j], ret_gn_b[j], ret_w_out[j],
                                      ln_g[i, 0], ln_b[i, 0], bsz, seq)
        else:
            xf, xb = _s5_layer(xf, ssm_a_re[j], ssm_a_im[j], ssm_b_re[j], ssm_b_im[j], ssm_c_re[j],
                               ssm_c_im[j], ssm_d[j], ssm_log_dt[j], ssm_w_glu[j],
                               ln_g[i, 0], ln_b[i, 0], bsz, seq)
        xf, xb = _moe_layer(xf, xb, moe_w_router[i], moe_router_bias[i], moe_w_gate, moe_w_up,
                            moe_w_down, shared_w_gate[i], shared_w_up[i], shared_w_down[i],
                            ln_g[i, 1], ln_b[i, 1], i)
    return xf.reshape(bsz, seq, d)
```

```python
import functools
import math

import jax
import jax.numpy as jnp
from jax import lax
from jax.experimental import pallas as pl
from jax.experimental.pallas import tpu as pltpu

F32 = jnp.float32
BF16 = jnp.bfloat16

D_MODEL = 1024
DEPTH = 4
N_MIXERS = 3
ATTN_GROUPS = ((128, 1), (512, 4), (2048, 16))
ATTN_HEADS = 8
ATTN_HEAD_DIM = D_MODEL // ATTN_HEADS
ATTN_BLOCK = 128
RET_HEADS = 4
RET_KEY_DIM = D_MODEL // RET_HEADS
RET_VAL_DIM = 2 * RET_KEY_DIM
RET_CHUNK = 128
SSM_GROUP = 16
SSM_GROUPS = D_MODEL // SSM_GROUP
SSM_STATE = 64
N_EXPERTS = 64
TOP_K = 8
N_EXPERT_GROUPS = 8
TOPK_GROUPS = 4
EXPERT_DIM = 256
SHARED_DIM = 256
ROUTED_SCALE = 2.5
DEEPNORM_ALPHA = (2 * DEPTH) ** 0.25
LN_EPS = 1e-5

LANES = 128
SUBLANES = 8
MASK_NEG = -1e30
VMEM_LIMIT = 56 * 1024 * 1024

ROW_TILE = 512
PROJ_TM = 2048
PROJ_TN = 1024
MOE_TILE = 1024
MOE_SLOTS = 160
MOE_GROUP = 16
MOE_UNROLL = 4
SSM_TIME = 512
SSM_COLS = 128


def _cparams():
    return pltpu.CompilerParams(vmem_limit_bytes=VMEM_LIMIT)


def _dot(a, b):
    return jnp.dot(a, b, preferred_element_type=F32)


def _dot_nt(a, b):
    return lax.dot_general(a, b, (((1,), (1,)), ((), ())), preferred_element_type=F32)


def _dot_tn(a, b):
    return lax.dot_general(a, b, (((0,), (0,)), ((), ())), preferred_element_type=F32)


def _sigmoid(x):
    return 1.0 / (1.0 + jnp.exp(-x))


def _layer_norm(y, g, b):
    mu = jnp.mean(y, axis=-1, keepdims=True)
    yc = y - mu
    var = jnp.mean(yc * yc, axis=-1, keepdims=True)
    return yc * lax.rsqrt(var + LN_EPS) * g + b


def _store_ln(y, g_ref, b_ref, o_ref, ob_ref):
    out = _layer_norm(y, g_ref[...], b_ref[...])
    o_ref[...] = out
    ob_ref[...] = out.astype(BF16)


def _proj_kernel(x_ref, w_ref, o_ref):
    o_ref[...] = _dot(x_ref[...], w_ref[...]).astype(o_ref.dtype)


def _proj(xb, w, out_dtype, name):
    m, k = xb.shape
    n = w.shape[1]
    tm = min(PROJ_TM, m)
    tn = min(PROJ_TN, n)
    return pl.pallas_call(
        _proj_kernel,
        grid=(m // tm, n // tn),
        in_specs=[pl.BlockSpec((tm, k), lambda i, j: (i, 0)),
                  pl.BlockSpec((k, tn), lambda i, j: (0, j))],
        out_specs=pl.BlockSpec((tm, tn), lambda i, j: (i, j)),
        out_shape=jax.ShapeDtypeStruct((m, n), out_dtype),
        compiler_params=_cparams(),
        name=name,
    )(xb, w)


def _proj_perm_kernel(x_ref, w_ref, o_ref, xs_ref, col_ref, *, dil):
    rows = x_ref.shape[0] // dil

    @pl.when(pl.program_id(1) == 0)
    def _():
        for c in range(x_ref.shape[1] // LANES):
            cs = slice(c * LANES, (c + 1) * LANES)
            col_ref[...] = x_ref[:, cs]
            for r in range(dil):
                xs_ref[r * rows:(r + 1) * rows, cs] = col_ref[pl.ds(r, rows, stride=dil), :].astype(BF16)

    res = _dot(xs_ref[...], w_ref[...])
    o_ref[0] = res.reshape(dil, rows, res.shape[1]).astype(o_ref.dtype)


def _proj_perm(xf, w, dil, bsz, seq, name):
    k = xf.shape[1]
    n = w.shape[1]
    tm = min(PROJ_TM, seq)
    tn = min(PROJ_TN, n)
    assert seq % tm == 0 and tm % (dil * 2 * SUBLANES) == 0
    per_b = seq // tm
    rows = tm // dil
    return pl.pallas_call(
        functools.partial(_proj_perm_kernel, dil=dil),
        grid=(bsz * per_b, n // tn),
        in_specs=[pl.BlockSpec((tm, k), lambda i, j: (i, 0)),
                  pl.BlockSpec((k, tn), lambda i, j: (0, j))],
        out_specs=pl.BlockSpec((1, dil, rows, tn), lambda i, j: (i // per_b, 0, i % per_b, j)),
        out_shape=jax.ShapeDtypeStruct((bsz, dil, seq // dil, n), BF16),
        scratch_shapes=[pltpu.VMEM((tm, k), BF16), pltpu.VMEM((tm, LANES), F32)],
        compiler_params=_cparams(),
        name=name,
    )(xf, w)


def _attn_kernel(q_ref, kc_ref, vc_ref, bp_ref, bc_ref, o_ref, lse_ref,
                 o_scr, lse_scr, mix_scr, kp_scr, vp_scr, *, dil):
    n = pl.program_id(1)
    r = pl.program_id(2)

    blk = ATTN_BLOCK

    @pl.when(n == 0)
    def _():
        kp_scr[r, 0:blk] = jnp.zeros((blk, kp_scr.shape[2]), kp_scr.dtype)
        vp_scr[r, 0:blk] = jnp.zeros((blk, vp_scr.shape[2]), vp_scr.dtype)

    kp_scr[r, blk:2 * blk] = kc_ref[0, 0]
    vp_scr[r, blk:2 * blk] = vc_ref[0, 0]
    first = jnp.where(n == 0, MASK_NEG, 0.0).astype(F32)
    col = lax.broadcasted_iota(jnp.int32, (1, 2 * blk), 1)
    first_row = jnp.where(col < blk, first, 0.0)
    scale = ATTN_HEAD_DIM ** -0.5
    lane = lax.broadcasted_iota(jnp.int32, (ATTN_BLOCK, LANES), 1)
    lse_tile = jnp.zeros((ATTN_BLOCK, LANES), F32)
    heads = range(ATTN_HEADS)
    sls = [slice(h * ATTN_HEAD_DIM, (h + 1) * ATTN_HEAD_DIM) for h in heads]
    ss = [_dot_nt(q_ref[0, 0, :, sls[h]], kp_scr[r, :, sls[h]]) * scale
          + (jnp.concatenate([bp_ref[h], bc_ref[h]], axis=1) + first_row) for h in heads]
    ms = [jnp.max(ss[h], axis=1, keepdims=True) for h in heads]
    ps = [jnp.exp(ss[h] - ms[h]) for h in heads]
    ls = [jnp.sum(ps[h], axis=1, keepdims=True) for h in heads]
    for h in heads:
        o = _dot(ps[h].astype(BF16), vp_scr[r, :, sls[h]])
        o_scr[h, r] = o / ls[h]
        lse_tile = jnp.where(lane == h, ms[h] + jnp.log(ls[h]), lse_tile)
    lse_scr[r] = lse_tile
    kp_scr[r, 0:blk] = kc_ref[0, 0]
    vp_scr[r, 0:blk] = vc_ref[0, 0]

    @pl.when(r == dil - 1)
    def _():
        for rr in range(dil):
            lse_ref[0, pl.ds(rr, ATTN_BLOCK, stride=dil), :] = lse_scr[rr]
        for h in range(ATTN_HEADS):
            for rr in range(dil):
                mix_scr[pl.ds(rr, ATTN_BLOCK, stride=dil), :] = o_scr[h, rr]
            o_ref[0, :, h * ATTN_HEAD_DIM:(h + 1) * ATTN_HEAD_DIM] = mix_scr[...]


def _attn_bias(window, dil):
    steps = window // dil
    assert steps <= ATTN_BLOCK
    qi = jnp.arange(ATTN_BLOCK)[:, None]
    kj = jnp.arange(ATTN_BLOCK)[None, :]
    slopes = 2.0 ** (-8.0 * (jnp.arange(ATTN_HEADS, dtype=F32) + 1.0) / ATTN_HEADS)

    def bias(step):
        valid = (step >= 0) & (step <= steps)
        dist = (step * dil).astype(F32)
        return jnp.where(valid[None], -(slopes[:, None, None] * dist[None]), MASK_NEG).astype(F32)

    return bias(qi - kj + ATTN_BLOCK), bias(qi - kj)


def _attn_group(qkv, gi, window, dil):
    b, _, ls, width = qkv.shape
    s = ls * dil
    hd = ATTN_HEADS * ATTN_HEAD_DIM
    assert ls % ATTN_BLOCK == 0 and width == 3 * hd
    nb = ls // ATTN_BLOCK
    bias_prev, bias_cur = _attn_bias(window, dil)

    def cur(c):
        return pl.BlockSpec((1, 1, ATTN_BLOCK, hd), lambda bi, n, r: (bi, r, n, c))

    const = pl.BlockSpec((ATTN_HEADS, ATTN_BLOCK, ATTN_BLOCK), lambda bi, n, r: (0, 0, 0))
    o, lse = pl.pallas_call(
        functools.partial(_attn_kernel, dil=dil),
        grid=(b, nb, dil),
        in_specs=[cur(0), cur(1), cur(2), const, const],
        out_specs=[pl.BlockSpec((1, ATTN_BLOCK * dil, hd), lambda bi, n, r: (bi, n, 0)),
                   pl.BlockSpec((1, ATTN_BLOCK * dil, LANES), lambda bi, n, r: (bi, n, 0))],
        out_shape=[jax.ShapeDtypeStruct((b, s, hd), F32),
                   jax.ShapeDtypeStruct((b, s, LANES), F32)],
        scratch_shapes=[pltpu.VMEM((ATTN_HEADS, dil, ATTN_BLOCK, ATTN_HEAD_DIM), F32),
                        pltpu.VMEM((dil, ATTN_BLOCK, LANES), F32),
                        pltpu.VMEM((dil * ATTN_BLOCK, ATTN_HEAD_DIM), F32),
                        pltpu.VMEM((dil, 2 * ATTN_BLOCK, hd), BF16),
                        pltpu.VMEM((dil, 2 * ATTN_BLOCK, hd), BF16)],
        compiler_params=_cparams(),
        name=f"attn_g{gi}",
    )(qkv, qkv, qkv, bias_prev, bias_cur)
    return o.reshape(b * s, hd), lse.reshape(b * s, LANES)


def _attn_out_kernel(o1_ref, o2_ref, o3_ref, l1_ref, l2_ref, l3_ref, x_ref, w_ref, g_ref, b_ref,
                     o_ref, ob_ref):
    l1, l2, l3 = l1_ref[...], l2_ref[...], l3_ref[...]
    mx = jnp.maximum(jnp.maximum(l1, l2), l3)
    e1, e2, e3 = jnp.exp(l1 - mx), jnp.exp(l2 - mx), jnp.exp(l3 - mx)
    den = e1 + e2 + e3
    w1, w2, w3 = e1 / den, e2 / den, e3 / den
    cols = []
    for h in range(ATTN_HEADS):
        sl = slice(h * ATTN_HEAD_DIM, (h + 1) * ATTN_HEAD_DIM)
        oh = (w1[:, h:h + 1] * o1_ref[:, sl] + w2[:, h:h + 1] * o2_ref[:, sl]
              + w3[:, h:h + 1] * o3_ref[:, sl])
        cols.append(oh.astype(BF16))
    o = jnp.concatenate(cols, axis=1)
    y = _dot(o, w_ref[...]) + DEEPNORM_ALPHA * x_ref[...]
    _store_ln(y, g_ref, b_ref, o_ref, ob_ref)


def _row_spec(tm, width):
    return pl.BlockSpec((tm, width), lambda i: (i, 0))


def _const_spec(shape):
    return pl.BlockSpec(shape, lambda i: tuple(0 for _ in shape))


def _x_outs(t, tm):
    return dict(
        out_specs=[_row_spec(tm, D_MODEL), _row_spec(tm, D_MODEL)],
        out_shape=[jax.ShapeDtypeStruct((t, D_MODEL), F32), jax.ShapeDtypeStruct((t, D_MODEL), BF16)],
    )


def _attention_layer(x, xb, w_in, w_out, ln_g, ln_b, bsz, seq):
    t = x.shape[0]
    hd3 = 3 * ATTN_HEADS * ATTN_HEAD_DIM
    wb = w_in.astype(BF16)
    outs = []
    for gi, (window, dil) in enumerate(ATTN_GROUPS):
        qkv = _proj_perm(x, wb[:, gi * hd3:(gi + 1) * hd3], dil, bsz, seq, f"attn_proj_g{gi}")
        outs.append(_attn_group(qkv, gi, window, dil))
    tm = min(ROW_TILE, t)
    hd = ATTN_HEADS * ATTN_HEAD_DIM
    return pl.pallas_call(
        _attn_out_kernel,
        grid=(t // tm,),
        in_specs=[_row_spec(tm, hd)] * 3 + [_row_spec(tm, LANES)] * 3
                 + [_row_spec(tm, D_MODEL), _const_spec((hd, D_MODEL)),
                    _const_spec((1, D_MODEL)), _const_spec((1, D_MODEL))],
        compiler_params=_cparams(),
        name="attn_out",
        **_x_outs(t, tm),
    )(outs[0][0], outs[1][0], outs[2][0], outs[0][1], outs[1][1], outs[2][1],
      x, w_out.astype(BF16), ln_g[None], ln_b[None])


def _ret_kernel(q_ref, k_ref, v_ref, g_ref, dec_ref, xi_ref, zeta_ref, cd_ref, gg_ref, gb_ref,
                y_ref, state_ref):
    n = pl.program_id(1)

    @pl.when(n == 0)
    def _():
        state_ref[...] = jnp.zeros_like(state_ref)

    kscale = RET_KEY_DIM ** -0.5
    dk, dv = RET_KEY_DIM, RET_VAL_DIM
    hs = range(RET_HEADS)
    qs = [q_ref[0, :, h * dk:(h + 1) * dk] for h in hs]
    ks = [k_ref[0, :, h * dk:(h + 1) * dk] for h in hs]
    vs = [v_ref[0, :, h * dv:(h + 1) * dv] for h in hs]
    scores = [_dot_nt(qs[h], ks[h]) * kscale * dec_ref[h] for h in hs]
    inners = [_dot(scores[h].astype(BF16), vs[h]) for h in hs]
    sts = [state_ref[h] for h in hs]
    crosses = [_dot((qs[h].astype(F32) * xi_ref[h]).astype(BF16), sts[h].astype(BF16)) for h in hs]
    kzs = [(ks[h].astype(F32) * kscale * zeta_ref[h]).astype(BF16) for h in hs]
    for h in hs:
        state_ref[h] = cd_ref[h, 0:1, 0:1] * sts[h] + _dot_tn(kzs[h], vs[h])
    for h in hs:
        sl = slice(h * dv, (h + 1) * dv)
        y = inners[h] + crosses[h]
        mu = jnp.mean(y, axis=1, keepdims=True)
        yc = y - mu
        var = jnp.mean(yc * yc, axis=1, keepdims=True)
        yn = yc * lax.rsqrt(var + LN_EPS) * gg_ref[:, sl] + gb_ref[:, sl]
        g = g_ref[0, :, sl]
        y_ref[0, :, sl] = ((g * _sigmoid(g)) * yn).astype(y_ref.dtype)


def _ret_out_kernel(y_ref, x_ref, w_ref, g_ref, b_ref, o_ref, ob_ref):
    y = _dot(y_ref[...], w_ref[...]) + DEEPNORM_ALPHA * x_ref[...]
    _store_ln(y, g_ref, b_ref, o_ref, ob_ref)


def _retention_layer(x, xb, w_in, gn_g, gn_b, w_out, ln_g, ln_b, bsz, seq):
    t = x.shape[0]
    h, dk, dv, c = RET_HEADS, RET_KEY_DIM, RET_VAL_DIM, RET_CHUNK
    nqkv = 2 * h * dk + h * dv
    wb = w_in.astype(BF16)
    qkv = _proj(xb, wb[:, :nqkv], BF16, "ret_proj_qkv").reshape(bsz, seq, nqkv)
    gate = _proj(xb, wb[:, nqkv:], F32, "ret_proj_gate").reshape(bsz, seq, h * dv)
    log_gamma = jnp.log(1.0 - 2.0 ** (-5.0 - jnp.arange(h, dtype=F32)))
    pos = jnp.arange(c, dtype=F32)
    rel = pos[:, None] - pos[None, :]
    decay = jnp.where(rel >= 0, jnp.exp(log_gamma[:, None, None] * jnp.maximum(rel, 0.0)), 0.0)
    xi = jnp.exp(log_gamma[:, None] * (pos[None, :] + 1.0))
    zeta = jnp.exp(log_gamma[:, None] * (c - 1.0 - pos[None, :]))
    xi = jnp.broadcast_to(xi[:, :, None], (h, c, dk))
    zeta = jnp.broadcast_to(zeta[:, :, None], (h, c, dk))
    cd = jnp.broadcast_to(jnp.exp(log_gamma * c)[:, None, None], (h, SUBLANES, LANES))
    assert h * dv == 2 * h * dk
    full = lambda *shape: pl.BlockSpec(shape, lambda bi, n: tuple(0 for _ in shape))
    y = pl.pallas_call(
        _ret_kernel,
        grid=(bsz, seq // c),
        in_specs=[pl.BlockSpec((1, c, h * dk), lambda bi, n: (bi, n, 0)),
                  pl.BlockSpec((1, c, h * dk), lambda bi, n: (bi, n, 1)),
                  pl.BlockSpec((1, c, h * dv), lambda bi, n: (bi, n, 1)),
                  pl.BlockSpec((1, c, h * dv), lambda bi, n: (bi, n, 0)),
                  full(h, c, c), full(h, c, dk), full(h, c, dk), full(h, SUBLANES, LANES),
                  full(1, h * dv), full(1, h * dv)],
        out_specs=pl.BlockSpec((1, c, h * dv), lambda bi, n: (bi, n, 0)),
        out_shape=jax.ShapeDtypeStruct((bsz, seq, h * dv), BF16),
        scratch_shapes=[pltpu.VMEM((h, dk, dv), F32)],
        compiler_params=_cparams(),
        name="ret_core",
    )(qkv, qkv, qkv, gate, decay, xi, zeta, cd, gn_g[None], gn_b[None])
    tm = min(ROW_TILE, t)
    return pl.pallas_call(
        _ret_out_kernel,
        grid=(t // tm,),
        in_specs=[_row_spec(tm, h * dv), _row_spec(tm, D_MODEL), _const_spec((h * dv, D_MODEL)),
                  _const_spec((1, D_MODEL)), _const_spec((1, D_MODEL))],
        compiler_params=_cparams(),
        name="ret_out",
        **_x_outs(t, tm),
    )(y.reshape(t, h * dv), x, w_out.astype(BF16), ln_g[None], ln_b[None])


def _gelu_tanh(y):
    return 0.5 * y * (1.0 + jnp.tanh(math.sqrt(2.0 / math.pi) * (y + 0.044715 * (y * y * y))))


def _s5_kernel(x_ref, wb_ref, wc_ref, pr_ref, pi_ref, d_ref, y_ref,
               cr_ref, ci_ref, xs_ref, hr_scr, hi_scr, yt_ref):
    n = pl.program_id(2)

    @pl.when(n == 0)
    def _():
        cr_ref[...] = jnp.zeros_like(cr_ref)
        ci_ref[...] = jnp.zeros_like(ci_ref)

    nch = pr_ref.shape[2]
    seg = SUBLANES
    tau_n = SSM_TIME // seg
    for s in range(seg):
        xs_ref[pl.ds(s, tau_n, stride=seg), :] = x_ref[0, s * tau_n:(s + 1) * tau_n, :]
    u = xs_ref[...]
    bu = _dot(u.astype(BF16), wb_ref[0])
    re = bu[:, :nch].reshape(tau_n, seg, nch)
    im = bu[:, nch:].reshape(tau_n, seg, nch)
    a_r = jnp.broadcast_to(pr_ref[0, 0:1, :], (seg, nch))
    a_i = jnp.broadcast_to(pi_ref[0, 0:1, :], (seg, nch))
    hr = jnp.zeros((seg, nch), F32)
    hi = jnp.zeros((seg, nch), F32)
    for t in range(tau_n):
        hr, hi = a_r * hr - a_i * hi + re[t], a_r * hi + a_i * hr + im[t]
        hr_scr[t] = hr
        hi_scr[t] = hi
    q_r = pr_ref[0, tau_n - 1:tau_n, :]
    q_i = pi_ref[0, tau_n - 1:tau_n, :]
    c_r = cr_ref[0:1, :]
    c_i = ci_ref[0:1, :]
    row = lax.broadcasted_iota(jnp.int32, (seg, nch), 0)
    in_r = jnp.zeros((seg, nch), F32)
    in_i = jnp.zeros((seg, nch), F32)
    for s in range(seg):
        in_r = jnp.where(row == s, c_r, in_r)
        in_i = jnp.where(row == s, c_i, in_i)
        c_r, c_i = (hr[s:s + 1, :] + (q_r * c_r - q_i * c_i),
                    hi[s:s + 1, :] + (q_r * c_i + q_i * c_r))
    cr_ref[...] = jnp.broadcast_to(c_r, cr_ref.shape)
    ci_ref[...] = jnp.broadcast_to(c_i, ci_ref.shape)
    hre, him = [], []
    for t in range(tau_n):
        p_r = pr_ref[0, t:t + 1, :]
        p_i = pi_ref[0, t:t + 1, :]
        hre.append(hr_scr[t] + (p_r * in_r - p_i * in_i))
        him.append(hi_scr[t] + (p_r * in_i + p_i * in_r))
    hcat = jnp.concatenate([jnp.concatenate(hre, axis=0), jnp.concatenate(him, axis=0)], axis=1)
    y = _dot(hcat.astype(BF16), wc_ref[0]) + d_ref[...] * u
    yt_ref[...] = _gelu_tanh(y)
    for s in range(seg):
        y_ref[0, s * tau_n:(s + 1) * tau_n, :] = yt_ref[pl.ds(s, tau_n, stride=seg), :].astype(y_ref.dtype)


def _s5_out_kernel(y_ref, x_ref, w_ref, g_ref, b_ref, o_ref, ob_ref):
    vg = _dot(y_ref[...], w_ref[...])
    val = vg[:, :D_MODEL]
    gate = vg[:, D_MODEL:]
    y = val * _sigmoid(gate) + DEEPNORM_ALPHA * x_ref[...]
    _store_ln(y, g_ref, b_ref, o_ref, ob_ref)


def _block_diag(m):
    nb, g, r, c = m.shape
    eye = jnp.eye(g, dtype=m.dtype)
    return (m[:, :, :, None, :] * eye[None, :, None, :, None]).reshape(nb, g * r, g * c)


def _s5_layer(x, a_re, a_im, b_re, b_im, c_re, c_im, d_skip, log_dt, w_glu, ln_g, ln_b, bsz, seq):
    t = x.shape[0]
    gpb = SSM_COLS // SSM_GROUP
    ncb = D_MODEL // SSM_COLS
    nch = gpb * SSM_STATE
    dt = jnp.exp(log_dt)[:, None]
    mag = jnp.exp(dt * a_re)
    ab_re = mag * jnp.cos(dt * a_im)
    ab_im = mag * jnp.sin(dt * a_im)
    den = a_re * a_re + a_im * a_im
    f_re = ((ab_re - 1.0) * a_re + ab_im * a_im) / den
    f_im = (ab_im * a_re - (ab_re - 1.0) * a_im) / den
    bb_re = f_re[..., None] * b_re - f_im[..., None] * b_im
    bb_im = f_re[..., None] * b_im + f_im[..., None] * b_re
    tb = lambda m: jnp.swapaxes(m, 1, 2).reshape(ncb, gpb, SSM_GROUP, SSM_STATE)
    w_b = jnp.concatenate([_block_diag(tb(bb_re)), _block_diag(tb(bb_im))], axis=2).astype(BF16)
    tc = lambda m: jnp.swapaxes(m, 1, 2).reshape(ncb, gpb, SSM_STATE, SSM_GROUP)
    w_c = jnp.concatenate([_block_diag(tc(c_re)), -_block_diag(tc(c_im))], axis=1).astype(BF16)
    tau_n = SSM_TIME // SUBLANES
    assert tau_n & (tau_n - 1) == 0
    pr, pi = ab_re[None], ab_im[None]
    while pr.shape[0] < tau_n:
        lr, li = pr[-1], pi[-1]
        pr, pi = (jnp.concatenate([pr, pr * lr - pi * li], axis=0),
                  jnp.concatenate([pi, pr * li + pi * lr], axis=0))
    pw_re = pr.reshape(tau_n, ncb, nch).transpose(1, 0, 2)
    pw_im = pi.reshape(tau_n, ncb, nch).transpose(1, 0, 2)
    x3 = x.reshape(bsz, seq, D_MODEL)
    blk = lambda shape: pl.BlockSpec((1,) + shape, lambda bi, cb, n: (cb, 0, 0))
    y = pl.pallas_call(
        _s5_kernel,
        grid=(bsz, ncb, seq // SSM_TIME),
        in_specs=[pl.BlockSpec((1, SSM_TIME, SSM_COLS), lambda bi, cb, n: (bi, n, cb)),
                  blk((SSM_COLS, 2 * nch)), blk((2 * nch, SSM_COLS)),
                  blk((tau_n, nch)), blk((tau_n, nch)),
                  pl.BlockSpec((1, SSM_COLS), lambda bi, cb, n: (0, cb))],
        out_specs=pl.BlockSpec((1, SSM_TIME, SSM_COLS), lambda bi, cb, n: (bi, n, cb)),
        out_shape=jax.ShapeDtypeStruct((bsz, seq, D_MODEL), BF16),
        scratch_shapes=[pltpu.VMEM((SUBLANES, nch), F32), pltpu.VMEM((SUBLANES, nch), F32),
                        pltpu.VMEM((SSM_TIME, SSM_COLS), F32),
                        pltpu.VMEM((tau_n, SUBLANES, nch), F32),
                        pltpu.VMEM((tau_n, SUBLANES, nch), F32),
                        pltpu.VMEM((SSM_TIME, SSM_COLS), F32)],
        compiler_params=_cparams(),
        name="s5_core",
    )(x3, w_b, w_c, pw_re, pw_im, d_skip.reshape(1, D_MODEL))
    tm = min(ROW_TILE, t)
    return pl.pallas_call(
        _s5_out_kernel,
        grid=(t // tm,),
        in_specs=[_row_spec(tm, D_MODEL), _row_spec(tm, D_MODEL), _const_spec((D_MODEL, 2 * D_MODEL)),
                  _const_spec((1, D_MODEL)), _const_spec((1, D_MODEL))],
        compiler_params=_cparams(),
        name="s5_out",
        **_x_outs(t, tm),
    )(y.reshape(t, D_MODEL), x, w_glu.astype(BF16), ln_g[None], ln_b[None])


def _cast_kernel(*refs):
    o_ref = refs[-1]
    o_ref[...] = jnp.concatenate([r[0].astype(o_ref.dtype) for r in refs[:-1]], axis=-1)


def _cast_bf16(*ws, layer, name):
    _, e, r, c = ws[0].shape
    return pl.pallas_call(
        _cast_kernel,
        grid=(e,),
        in_specs=[pl.BlockSpec((1, 1, r, c), lambda i: (layer, i, 0, 0)) for _ in ws],
        out_specs=pl.BlockSpec((1, r, c * len(ws)), lambda i: (i, 0, 0)),
        out_shape=jax.ShapeDtypeStruct((e, r, c * len(ws)), BF16),
        compiler_params=_cparams(),
        name=name,
    )(*ws)


def _first_index(eq, iota, size, axis):
    return jnp.min(jnp.where(eq, iota, size), axis=axis, keepdims=True)


def _router_kernel(x_ref, w_ref, b_ref, tri_ref, rank_ref, gate_ref, cnt_ref):
    x = x_ref[...]
    w = w_ref[...]
    x_hi = x.astype(BF16)
    x_lo = (x - x_hi.astype(F32)).astype(BF16)
    w_hi = w.astype(BF16)
    w_lo = (w - w_hi.astype(F32)).astype(BF16)
    logits = _dot_nt(w_hi, x_hi) + (_dot_nt(w_hi, x_lo) + _dot_nt(w_lo, x_hi))
    scores = _sigmoid(logits)
    sel = scores + b_ref[...]
    tt = sel.shape[1]
    gsz = N_EXPERTS // N_EXPERT_GROUPS
    grp = sel.reshape(N_EXPERT_GROUPS, gsz, tt)
    wi = lax.broadcasted_iota(jnp.int32, grp.shape, 1)
    m1 = jnp.max(grp, axis=1, keepdims=True)
    i1 = _first_index(grp == m1, wi, gsz, 1)
    m2 = jnp.max(jnp.where(wi == i1, -jnp.inf, grp), axis=1, keepdims=True)
    gs = (m1 + m2).reshape(N_EXPERT_GROUPS, tt)
    gi = lax.broadcasted_iota(jnp.int32, gs.shape, 0)
    gmask = jnp.zeros(gs.shape, jnp.bool_)
    for _ in range(TOPK_GROUPS):
        gm = jnp.max(gs, axis=0, keepdims=True)
        pick = gi == _first_index(gs == gm, gi, N_EXPERT_GROUPS, 0)
        gmask = gmask | pick
        gs = jnp.where(pick, -jnp.inf, gs)
    emask = jnp.broadcast_to(gmask[:, None, :], grp.shape).reshape(N_EXPERTS, tt)
    cand = jnp.where(emask, sel, -jnp.inf)
    ei = lax.broadcasted_iota(jnp.int32, cand.shape, 0)
    chosen = jnp.zeros(cand.shape, jnp.bool_)
    for _ in range(TOP_K):
        cm = jnp.max(cand, axis=0, keepdims=True)
        pick = ei == _first_index(cand == cm, ei, N_EXPERTS, 0)
        chosen = chosen | pick
        cand = jnp.where(pick, -jnp.inf, cand)
    top_w = jnp.where(chosen, scores, 0.0)
    denom = jnp.sum(top_w, axis=0, keepdims=True)
    gate_ref[...] = top_w / denom * ROUTED_SCALE
    chosen_f = chosen.astype(F32)
    rank = _dot(chosen_f.astype(BF16), tri_ref[...])
    rank_ref[...] = jnp.where(chosen, rank, -1.0).astype(jnp.int32)
    cnt = jnp.sum(chosen_f, axis=1, keepdims=True)
    cnt_ref[0] = jnp.broadcast_to(cnt, (N_EXPERTS, LANES)).astype(jnp.int32)


def _moe_kernel(cnt_ref, x_ref, rank_ref, gate_ref, wgu_ref, wd_ref, o_ref, oh_scr, yg_scr):
    j = pl.program_id(0)
    s = pl.program_id(1)

    @pl.when(s == 0)
    def _():
        o_ref[...] = jnp.zeros_like(o_ref)

    tt = x_ref.shape[0]
    slot = lax.broadcasted_iota(jnp.int32, (MOE_SLOTS, tt), 0)

    def expert_chunks(us, c):
        hits = [(rank_ref[u] - c * MOE_SLOTS) == slot for u in us]
        onehots = [hit.astype(F32).astype(BF16) for hit in hits]
        xcs = [_dot(oh, x_ref[...]).astype(BF16) for oh in onehots]
        hgus = [_dot(xc, wgu_ref[u]) for xc, u in zip(xcs, us)]
        hbs = [((h[:, :EXPERT_DIM] * _sigmoid(h[:, :EXPERT_DIM])) * h[:, EXPERT_DIM:]).astype(BF16)
               for h in hgus]
        ys = [_dot(hb, wd_ref[u]) for hb, u in zip(hbs, us)]
        gsels = [jnp.sum(jnp.where(hit, gate_ref[u], 0.0), axis=1, keepdims=True)
                 for hit, u in zip(hits, us)]
        return onehots, [(y * g).astype(BF16) for y, g in zip(ys, gsels)]

    steps_per_group = MOE_GROUP // MOE_UNROLL
    gs = s % steps_per_group
    us = list(range(MOE_UNROLL))
    onehots, ygs = expert_chunks(us, 0)
    for u in us:
        row0 = pl.multiple_of((gs * MOE_UNROLL + u) * MOE_SLOTS, MOE_SLOTS)
        oh_scr[pl.ds(row0, MOE_SLOTS), :] = onehots[u]
        yg_scr[pl.ds(row0, MOE_SLOTS), :] = ygs[u]

    for u in us:
        cnt = cnt_ref[j * N_EXPERTS + s * MOE_UNROLL + u]
        n_chunks = (cnt + MOE_SLOTS - 1) // MOE_SLOTS

        def overflow(c, carry, u=u):
            onehot_c, yg_c = expert_chunks([u], c)
            o_ref[...] += _dot_tn(onehot_c[0], yg_c[0])
            return carry

        lax.fori_loop(1, n_chunks, overflow, 0)

    @pl.when(gs == steps_per_group - 1)
    def _():
        o_ref[...] += _dot_tn(oh_scr[...], yg_scr[...])


def _moe_out_kernel(x_ref, xb_ref, acc_ref, wgu_ref, wd_ref, g_ref, b_ref, o_ref, ob_ref):
    hgu = _dot(xb_ref[...], wgu_ref[...])
    hg = hgu[:, :SHARED_DIM]
    hb = ((hg * _sigmoid(hg)) * hgu[:, SHARED_DIM:]).astype(BF16)
    shared = _dot(hb, wd_ref[...])
    y = DEEPNORM_ALPHA * x_ref[...] + (acc_ref[...] + shared)
    _store_ln(y, g_ref, b_ref, o_ref, ob_ref)


def _moe_layer(x, xb, w_router, router_bias, w_gate, w_up, w_down, sw_gate, sw_up, sw_down,
               ln_g, ln_b, layer):
    t = x.shape[0]
    tt = min(MOE_TILE, t)
    nj = t // tt
    tri = jnp.triu(jnp.ones((tt, tt), F32), 1).astype(BF16)
    rank, gate, cnt = pl.pallas_call(
        _router_kernel,
        grid=(nj,),
        in_specs=[_row_spec(tt, D_MODEL), _const_spec((N_EXPERTS, D_MODEL)),
                  _const_spec((N_EXPERTS, 1)), _const_spec((tt, tt))],
        out_specs=[pl.BlockSpec((N_EXPERTS, tt), lambda i: (0, i)),
                   pl.BlockSpec((N_EXPERTS, tt), lambda i: (0, i)),
                   pl.BlockSpec((1, N_EXPERTS, LANES), lambda i: (i, 0, 0))],
        out_shape=[jax.ShapeDtypeStruct((N_EXPERTS, t), jnp.int32),
                   jax.ShapeDtypeStruct((N_EXPERTS, t), F32),
                   jax.ShapeDtypeStruct((nj, N_EXPERTS, LANES), jnp.int32)],
        compiler_params=_cparams(),
        name="moe_router",
    )(x, w_router.T, router_bias[:, None], tri)
    wgu = _cast_bf16(w_gate, w_up, layer=layer, name="moe_cast_gu")
    wd = _cast_bf16(w_down, layer=layer, name="moe_cast_d")
    acc = pl.pallas_call(
        _moe_kernel,
        grid_spec=pltpu.PrefetchScalarGridSpec(
            num_scalar_prefetch=1,
            grid=(nj, N_EXPERTS // MOE_UNROLL),
            in_specs=[pl.BlockSpec((tt, D_MODEL), lambda j, e, c: (j, 0)),
                      pl.BlockSpec((MOE_UNROLL, 1, tt), lambda j, e, c: (e, 0, j)),
                      pl.BlockSpec((MOE_UNROLL, 1, tt), lambda j, e, c: (e, 0, j)),
                      pl.BlockSpec((MOE_UNROLL, D_MODEL, 2 * EXPERT_DIM), lambda j, e, c: (e, 0, 0)),
                      pl.BlockSpec((MOE_UNROLL, EXPERT_DIM, D_MODEL), lambda j, e, c: (e, 0, 0))],
            out_specs=pl.BlockSpec((tt, D_MODEL), lambda j, e, c: (j, 0)),
            scratch_shapes=[pltpu.VMEM((MOE_GROUP * MOE_SLOTS, tt), BF16),
                            pltpu.VMEM((MOE_GROUP * MOE_SLOTS, D_MODEL), BF16)],
        ),
        out_shape=jax.ShapeDtypeStruct((t, D_MODEL), F32),
        compiler_params=_cparams(),
        name="moe_experts",
    )(cnt[:, :, 0].reshape(-1), xb, rank.reshape(N_EXPERTS, 1, t), gate.reshape(N_EXPERTS, 1, t),
      wgu, wd)
    tm = min(ROW_TILE, t)
    sgu = jnp.concatenate([sw_gate, sw_up], axis=1).astype(BF16)
    return pl.pallas_call(
        _moe_out_kernel,
        grid=(t // tm,),
        in_specs=[_row_spec(tm, D_MODEL), _row_spec(tm, D_MODEL), _row_spec(tm, D_MODEL),
                  _const_spec((D_MODEL, 2 * SHARED_DIM)), _const_spec((SHARED_DIM, D_MODEL)),
                  _const_spec((1, D_MODEL)), _const_spec((1, D_MODEL))],
        compiler_params=_cparams(),
        name="moe_out",
        **_x_outs(t, tm),
    )(x, xb, acc, sgu, sw_down.astype(BF16), ln_g[None], ln_b[None])


def kernel(x, attn_w_in, attn_w_out, ret_w_in, ret_gn_g, ret_gn_b, ret_w_out, ssm_a_re, ssm_a_im, ssm_b_re, ssm_b_im, ssm_c_re, ssm_c_im, ssm_d, ssm_log_dt, ssm_w_glu, moe_w_router, moe_router_bias, moe_w_gate, moe_w_up, moe_w_down, shared_w_gate, shared_w_up, shared_w_down, ln_g, ln_b):
    bsz, seq, d = x.shape
    depth = ln_g.shape[0]
    xf = x.reshape(bsz * seq, d)
    xb = xf.astype(BF16)
    for i in range(depth):
        j = i // N_MIXERS
        kind = i % N_MIXERS
        if kind == 0:
            xf, xb = _attention_layer(xf, xb, attn_w_in[j], attn_w_out[j], ln_g[i, 0], ln_b[i, 0],
                                      bsz, seq)
        elif kind == 1:
            xf, xb = _retention_layer(xf, xb, ret_w_in[j], ret_gn_g[j], ret_gn_b[j], ret_w_out[j],
                                      ln_g[i, 0], ln_b[i, 0], bsz, seq)
        else:
            xf, xb = _s5_layer(xf, ssm_a_re[j], ssm_a_im[j], ssm_b_re[j], ssm_b_im[j], ssm_c_re[j],
                               ssm_c_im[j], ssm_d[j], ssm_log_dt[j], ssm_w_glu[j],
                               ln_g[i, 0], ln_b[i, 0], bsz, seq)
        xf, xb = _moe_layer(xf, xb, moe_w_router[i], moe_router_bias[i], moe_w_gate, moe_w_up,
                            moe_w_down, shared_w_gate[i], shared_w_up[i], shared_w_down[i],
                            ln_g[i, 1], ln_b[i, 1], i)
    return xf.reshape(bsz, seq, d)
```
